```python
import jax, jax.numpy as jnp
from jax import lax
import numpy as np

D_MODEL = 1024
BATCH = 8
SEQ = 2048
DEPTH = 2
DEC_BATCH = 128
DEC_SEQ = 8
PAST_LEN = 16384
PAGE_SIZE = 128

N_MIXERS = 2
N_RWKV = (DEPTH + 1) // 2
N_CONV = DEPTH // 2
N_META = 16
HEAD_DIM = 64
N_HEADS = D_MODEL // HEAD_DIM
D_DECAY_LORA = 64
D_AAA_LORA = 64
D_GATE_LORA = 160
D_FF = 2816
CONV_W = 3
RMS_EPS = 1e-6
GN_EPS = 64e-5

kernel_name = "rwkv7_shortconv_macaron_hybrid_step"


def _rmsnorm(x, g):
    x32 = x.astype(jnp.float32)
    y = x32 * lax.rsqrt(jnp.mean(x32 * x32, axis=-1, keepdims=True) + RMS_EPS)
    return (y * g.astype(jnp.float32)).astype(x.dtype)


def _swiglu(x, w_gu, w_down):
    gate, up = jnp.split(x @ w_gu, 2, axis=-1)
    return (jax.nn.silu(gate) * up) @ w_down


def _rwkv7_time_mix(h, wkv0, shift0, mu, w_rkv, w0, w1, w2, a0, a1, a2, g1, g2,
                    k_k, k_a, r_k, ln_w, ln_b, w_o):
    bsz, t_len, _ = h.shape
    f32 = jnp.float32
    prev = jnp.concatenate([shift0[:, None, :].astype(h.dtype), h[:, :-1]], axis=1)
    xx = prev - h
    xr, xw, xk, xv, xa, xg = [h + xx * mu[c] for c in range(6)]
    r = xr @ w_rkv[0]
    k = xk @ w_rkv[1]
    v = xv @ w_rkv[2]
    w = -jax.nn.softplus(-(w0 + jnp.tanh(xw @ w1) @ w2)) - 0.5
    decay = jnp.exp(-jnp.exp(w.astype(f32)))
    a = jax.nn.sigmoid(a0 + (xa @ a1) @ a2)
    g = jax.nn.sigmoid(xg @ g1) @ g2
    kk = (k * k_k).astype(f32).reshape(bsz, t_len, N_HEADS, HEAD_DIM)
    kk = kk / jnp.maximum(jnp.linalg.norm(kk, axis=-1, keepdims=True), 1e-12)
    k = k * (1.0 + (a - 1.0) * k_a)

    def heads(z):
        return z.astype(f32).reshape(bsz, t_len, N_HEADS, HEAD_DIM)

    r_h, k_h, v_h, a_h, d_h = heads(r), heads(k), heads(v), heads(a), heads(decay)
    b_h = kk * a_h

    def step(S, inp):
        r_t, d_t, k_t, v_t, kk_t, b_t = inp
        S = (S * d_t[..., None, :]
             + jnp.einsum('bhij,bhj->bhi', S, -kk_t)[..., None] * b_t[..., None, :]
             + v_t[..., :, None] * k_t[..., None, :])
        return S, jnp.einsum('bhij,bhj->bhi', S, r_t)

    tm = lambda z: jnp.swapaxes(z, 0, 1)
    S_fin, ys = lax.scan(step, wkv0.astype(f32),
                         (tm(r_h), tm(d_h), tm(k_h), tm(v_h), tm(kk), tm(b_h)))
    y = tm(ys)
    mean = jnp.mean(y, axis=-1, keepdims=True)
    var = jnp.mean(jnp.square(y - mean), axis=-1, keepdims=True)
    yn = ((y - mean) * lax.rsqrt(var + GN_EPS)).reshape(bsz, t_len, D_MODEL)
    yn = yn * ln_w.astype(f32) + ln_b.astype(f32)
    bonus = jnp.sum(r_h * k_h * r_k.astype(f32), axis=-1, keepdims=True) * v_h
    o = ((yn + bonus.reshape(bsz, t_len, D_MODEL)) * g.astype(f32)).astype(h.dtype)
    return o @ w_o, S_fin.astype(wkv0.dtype), h[:, -1].astype(shift0.dtype)


def _short_conv_mix(h, conv0, w_in, conv_w, w_out):
    t_len = h.shape[1]
    gate_b, gate_c, xin = jnp.split(h @ w_in, 3, axis=-1)
    u = gate_c * xin
    pad = jnp.concatenate([conv0.astype(u.dtype), u], axis=1)
    conv = pad[:, 0:t_len] * conv_w[0] + pad[:, 1:t_len + 1] * conv_w[1] + pad[:, 2:t_len + 2] * conv_w[2]
    return (gate_b * conv) @ w_out, pad[:, -(CONV_W - 1):].astype(conv0.dtype)


def _trunk(x, wkv_in, shift_in, conv_in, p):
    new_wkv, new_shift, new_conv = [], [], []
    for i in range(DEPTH):
        x = x + 0.5 * _swiglu(_rmsnorm(x, p['ffn_norm'][i, 0]), p['ffn_w_gu'][i, 0], p['ffn_w_down'][i, 0])
        hn = _rmsnorm(x, p['mix_norm'][i])
        j = i // N_MIXERS
        if i % N_MIXERS == 0:
            out, s_wkv, s_shift = _rwkv7_time_mix(
                hn, wkv_in[j], shift_in[j], p['rk_mu'][j], p['rk_w_rkv'][j], p['rk_w0'][j],
                p['rk_w1'][j], p['rk_w2'][j], p['rk_a0'][j], p['rk_a1'][j], p['rk_a2'][j],
                p['rk_g1'][j], p['rk_g2'][j], p['rk_k_k'][j], p['rk_k_a'][j], p['rk_r_k'][j],
                p['rk_ln_w'][j], p['rk_ln_b'][j], p['rk_w_o'][j])
            new_wkv.append(s_wkv)
            new_shift.append(s_shift)
        else:
            out, s_conv = _short_conv_mix(hn, conv_in[j], p['sc_w_in'][j], p['sc_conv_w'][j], p['sc_w_out'][j])
            new_conv.append(s_conv)
        x = x + out
        x = x + 0.5 * _swiglu(_rmsnorm(x, p['ffn_norm'][i, 1]), p['ffn_w_gu'][i, 1], p['ffn_w_down'][i, 1])
    x = _rmsnorm(x, p['final_norm'])
    return x, jnp.stack(new_wkv), jnp.stack(new_shift), jnp.stack(new_conv)


def setup_inputs(seed: int = 0) -> dict:
    key = jax.random.key(seed)
    ks = iter(jax.random.split(key, 48))
    f32 = jnp.float32

    def nrm(shape, s):
        return jax.random.normal(next(ks), shape, f32) * s

    def unif(shape, lo, hi):
        return jax.random.uniform(next(ks), shape, f32, lo, hi)

    D, H, N, F = D_MODEL, N_HEADS, HEAD_DIM, D_FF
    return {
        'x_prompt': nrm((BATCH, SEQ, D), 1.0),
        'x_sample': nrm((DEC_BATCH, DEC_SEQ, D), 1.0),
        'state_wkv': nrm((N_RWKV, DEC_BATCH, H, N, N), 0.1),
        'state_shift': nrm((N_RWKV, DEC_BATCH, D), 1.0),
        'state_conv': nrm((N_CONV, DEC_BATCH, CONV_W - 1, D), 0.5),
        'meta': nrm((N_META, D), 1.0),
        'ffn_norm': 1.0 + nrm((DEPTH, 2, D), 0.01),
        'ffn_w_gu': nrm((DEPTH, 2, D, 2 * F), D ** -0.5),
        'ffn_w_down': nrm((DEPTH, 2, F, D), F ** -0.5),
        'mix_norm': 1.0 + nrm((DEPTH, D), 0.01),
        'final_norm': 1.0 + nrm((D,), 0.01),
        'rk_mu': unif((N_RWKV, 6, D), 0.0, 1.0),
        'rk_w_rkv': nrm((N_RWKV, 3, D, D), D ** -0.5),
        'rk_w0': unif((N_RWKV, D), -6.5, -1.5),
        'rk_w1': nrm((N_RWKV, D, D_DECAY_LORA), D ** -0.5),
        'rk_w2': nrm((N_RWKV, D_DECAY_LORA, D), 0.1 * D_DECAY_LORA ** -0.5),
        'rk_a0': nrm((N_RWKV, D), 0.1),
        'rk_a1': nrm((N_RWKV, D, D_AAA_LORA), D ** -0.5),
        'rk_a2': nrm((N_RWKV, D_AAA_LORA, D), 0.5 * D_AAA_LORA ** -0.5),
        'rk_g1': nrm((N_RWKV, D, D_GATE_LORA), D ** -0.5),
        'rk_g2': nrm((N_RWKV, D_GATE_LORA, D), D_GATE_LORA ** -0.5),
        'rk_k_k': 0.85 + nrm((N_RWKV, D), 0.05),
        'rk_k_a': 1.0 + nrm((N_RWKV, D), 0.05),
        'rk_r_k': -0.04 + nrm((N_RWKV, H, N), 0.05),
        'rk_ln_w': 1.0 + nrm((N_RWKV, D), 0.01),
        'rk_ln_b': nrm((N_RWKV, D), 0.01),
        'rk_w_o': nrm((N_RWKV, D, D), D ** -0.5),
        'sc_w_in': nrm((N_CONV, D, 3 * D), D ** -0.5),
        'sc_conv_w': nrm((N_CONV, CONV_W, D), CONV_W ** -0.5),
        'sc_w_out': nrm((N_CONV, D, D), D ** -0.5),
    }


def reference(x_prompt, x_sample, state_wkv, state_shift, state_conv, meta, ffn_norm, ffn_w_gu, ffn_w_down,
              mix_norm, final_norm, rk_mu, rk_w_rkv, rk_w0, rk_w1, rk_w2, rk_a0, rk_a1, rk_a2, rk_g1, rk_g2,
              rk_k_k, rk_k_a, rk_r_k, rk_ln_w, rk_ln_b, rk_w_o, sc_w_in, sc_conv_w, sc_w_out):
    p = {
        'ffn_norm': ffn_norm, 'ffn_w_gu': ffn_w_gu, 'ffn_w_down': ffn_w_down, 'mix_norm': mix_norm,
        'final_norm': final_norm, 'rk_mu': rk_mu, 'rk_w_rkv': rk_w_rkv, 'rk_w0': rk_w0, 'rk_w1': rk_w1,
        'rk_w2': rk_w2, 'rk_a0': rk_a0, 'rk_a1': rk_a1, 'rk_a2': rk_a2, 'rk_g1': rk_g1, 'rk_g2': rk_g2,
        'rk_k_k': rk_k_k, 'rk_k_a': rk_k_a, 'rk_r_k': rk_r_k, 'rk_ln_w': rk_ln_w, 'rk_ln_b': rk_ln_b,
        'rk_w_o': rk_w_o, 'sc_w_in': sc_w_in, 'sc_conv_w': sc_conv_w, 'sc_w_out': sc_w_out,
    }
    bp = x_prompt.shape[0]
    xp = jnp.concatenate([jnp.broadcast_to(meta.astype(x_prompt.dtype)[None], (bp, N_META, D_MODEL)), x_prompt], axis=1)
    wkv0 = jnp.zeros((N_RWKV, bp, N_HEADS, HEAD_DIM, HEAD_DIM), state_wkv.dtype)
    shift0 = jnp.zeros((N_RWKV, bp, D_MODEL), state_shift.dtype)
    conv0 = jnp.zeros((N_CONV, bp, CONV_W - 1, D_MODEL), state_conv.dtype)
    yp, wkv_p, shift_p, conv_p = _trunk(xp, wkv0, shift0, conv0, p)
    y_prompt = yp[:, N_META:]
    y_sample, wkv_s, shift_s, conv_s = _trunk(x_sample, state_wkv, state_shift, state_conv, p)
    return (y_prompt, y_sample, wkv_p, shift_p, conv_p, wkv_s, shift_s, conv_s)
```

```python
import functools

import jax
import jax.numpy as jnp
from jax import lax
from jax.experimental import pallas as pl
from jax.experimental.pallas import tpu as pltpu

D_MODEL = 1024
HEAD_DIM = 64
N_HEADS = D_MODEL // HEAD_DIM
D_FF = 2816
N_META = 16
DEPTH = 2
CONV_W = 3
RMS_EPS = 1e-6
GN_EPS = 64e-5

LANES = 128
SUBLANES = 8
MXU_DIM = 256
PAIR = 2 * HEAD_DIM
N_PAIRS = D_MODEL // PAIR
VMEM_LIMIT = 56 * 1024 * 1024

FFN_CHUNKS = ((0, 6 * MXU_DIM), (6 * MXU_DIM, D_FF))

F32 = jnp.float32
BF16 = jnp.bfloat16


def _dot(a, b):
    return jnp.dot(a, b, preferred_element_type=F32)


def _dot_nt(a, b):
    return lax.dot_general(a, b, (((1,), (1,)), ((), ())), preferred_element_type=F32)


def _dot_tn(a, b):
    return lax.dot_general(a, b, (((0,), (0,)), ((), ())), preferred_element_type=F32)


def _rms(x, g):
    ms = jnp.mean(x * x, axis=-1, keepdims=True)
    return x * lax.rsqrt(ms + RMS_EPS) * g


def _sigmoid(x):
    return 1.0 / (1.0 + jnp.exp(-x))


def _head_ones():
    r = lax.broadcasted_iota(jnp.int32, (MXU_DIM, MXU_DIM), 0) >> 6
    c = lax.broadcasted_iota(jnp.int32, (MXU_DIM, MXU_DIM), 1) >> 6
    return jnp.where(r == c, 1.0, 0.0).astype(BF16)


def _head_sum(x, ones):
    hi = x.astype(BF16)
    lo = (x - hi.astype(F32)).astype(BF16)
    cols = []
    for j in range(D_MODEL // MXU_DIM):
        sl = slice(j * MXU_DIM, (j + 1) * MXU_DIM)
        cols.append(_dot(hi[:, sl], ones) + _dot(lo[:, sl], ones))
    return jnp.concatenate(cols, axis=1)


def _const_spec(shape):
    nd = len(shape)
    return pl.BlockSpec(shape, lambda *_: (0,) * nd, pipeline_mode=pl.Buffered(1))


def _params(sem):
    return pltpu.CompilerParams(dimension_semantics=sem, vmem_limit_bytes=VMEM_LIMIT)


def _ffn_kernel(x_ref, g_ref, wgu_ref, wd_ref, g2_ref, *out_refs, emit_x, emit_n):
    x = x_ref[...]
    xn = _rms(x, g_ref[...]).astype(BF16)
    acc = None
    for lo, hi in FFN_CHUNKS:
        gate = _dot(xn, wgu_ref[:, lo:hi])
        up = _dot(xn, wgu_ref[:, D_FF + lo:D_FF + hi])
        act = (gate * _sigmoid(gate) * up).astype(BF16)
        part = _dot(act, wd_ref[lo:hi, :])
        acc = part if acc is None else acc + part
    out = x + 0.5 * acc
    i = 0
    if emit_x:
        out_refs[i][...] = out
        i += 1
    if emit_n:
        out_refs[i][...] = _rms(out, g2_ref[...])


def _ffn(x, g, wgu, wd, g2, tm, emit_x=True, emit_n=True):
    m = x.shape[0]
    row = pl.BlockSpec((tm, D_MODEL), lambda i: (i, 0))
    n_out = int(emit_x) + int(emit_n)
    outs = pl.pallas_call(
        functools.partial(_ffn_kernel, emit_x=emit_x, emit_n=emit_n),
        grid=(m // tm,),
        in_specs=[row, _const_spec((1, D_MODEL)), _const_spec((D_MODEL, 2 * D_FF)),
                  _const_spec((D_FF, D_MODEL)), _const_spec((1, D_MODEL))],
        out_specs=[row] * n_out,
        out_shape=[jax.ShapeDtypeStruct((m, D_MODEL), F32)] * n_out,
        compiler_params=_params(("parallel",)),
        name="ffn",
    )(x, g.reshape(1, D_MODEL), wgu, wd, g2.reshape(1, D_MODEL))
    return tuple(outs)


def _rk_pre_core(h, prev, p, out_refs):
    (mu, wr, wk, wv, w0, w1, w2, a0, a1, a2, g1, g2, k_k, k_a) = [r[...] for r in p]
    xx = prev - h
    mix = [(h + xx * mu[c:c + 1, :]).astype(BF16) for c in range(6)]
    xr, xw, xk, xv, xa, xg = mix
    r = _dot(xr, wr)
    k = _dot(xk, wk)
    v = _dot(xv, wv)
    z = w0 + _dot(jnp.tanh(_dot(xw, w1)).astype(BF16), w2)
    nz = -z
    sp = jnp.maximum(nz, 0.0) + jnp.log(1.0 + jnp.exp(-jnp.abs(nz)))
    lw = -jnp.exp(-sp - 0.5)
    a = _sigmoid(a0 + _dot(_dot(xa, a1).astype(BF16), a2))
    g = _dot(_sigmoid(_dot(xg, g1)).astype(BF16), g2)
    kk = k * k_k
    ss = _head_sum(kk * kk, _head_ones())
    kk = kk / jnp.maximum(jnp.sqrt(ss), 1e-12)
    kmod = k * (1.0 + (a - 1.0) * k_a)
    r_ref, lw_ref, k_ref, v_ref, kk_ref, b_ref, g_ref = out_refs
    r_ref[...] = r
    lw_ref[...] = lw
    k_ref[...] = kmod
    v_ref[...] = v
    kk_ref[...] = kk
    b_ref[...] = kk * a
    g_ref[...] = g


def _rk_pre_prompt_kernel(h_ref, s0_ref, *refs, tm):
    p, out_refs, carry_ref = refs[:14], refs[14:21], refs[21]

    @pl.when(pl.program_id(1) == 0)
    def _():
        carry_ref[0:1, :] = s0_ref[...]

    h = h_ref[...]
    row = lax.broadcasted_iota(jnp.int32, h.shape, 0)
    prev = jnp.where(row == 0, carry_ref[0:1, :], pltpu.roll(h, 1, axis=0))
    carry_ref[0:1, :] = h[tm - 1:tm, :]
    _rk_pre_core(h, prev, p, out_refs)


def _rk_pre_sample_kernel(h_ref, s0_ref, *refs, ns, t_len):
    p, out_refs = refs[:14], refs[14:21]
    h = h_ref[...]
    rows = ns * t_len
    s0 = jnp.broadcast_to(s0_ref[...], (ns, t_len, D_MODEL)).reshape(rows, D_MODEL)
    t = lax.broadcasted_iota(jnp.int32, h.shape, 0) & (t_len - 1)
    prev = jnp.where(t == 0, s0, pltpu.roll(h, 1, axis=0))
    _rk_pre_core(h, prev, p, out_refs)


def _rk_param_specs(p):
    return [_const_spec(a.shape) for a in p]


def _rk_pre(hn, shift0, p, bsz, t_len, tm):
    m = bsz * t_len
    out_shape = [jax.ShapeDtypeStruct((m, D_MODEL), F32)] * 7
    s0 = shift0.reshape(bsz, 1, D_MODEL)
    if t_len > tm:
        nt = t_len // tm
        row = pl.BlockSpec((tm, D_MODEL), lambda b, t: (b * nt + t, 0))
        return pl.pallas_call(
            functools.partial(_rk_pre_prompt_kernel, tm=tm),
            grid=(bsz, nt),
            in_specs=[row, pl.BlockSpec((None, 1, D_MODEL), lambda b, t: (b, 0, 0))] + _rk_param_specs(p),
            out_specs=[row] * 7,
            out_shape=out_shape,
            scratch_shapes=[pltpu.VMEM((SUBLANES, D_MODEL), F32)],
            compiler_params=_params(("arbitrary", "arbitrary")),
            name="rk_pre_prompt",
        )(hn, s0, *p)
    ns = tm // t_len
    row = pl.BlockSpec((tm, D_MODEL), lambda i: (i, 0))
    return pl.pallas_call(
        functools.partial(_rk_pre_sample_kernel, ns=ns, t_len=t_len),
        grid=(m // tm,),
        in_specs=[row, pl.BlockSpec((ns, 1, D_MODEL), lambda i: (i, 0, 0))] + _rk_param_specs(p),
        out_specs=[row] * 7,
        out_shape=out_shape,
        compiler_params=_params(("parallel",)),
        name="rk_pre_sample",
    )(hn, s0, *p)


def _wkv_masks(c):
    rows = 128
    sh = c.bit_length() - 1
    row = lax.broadcasted_iota(jnp.int32, (rows, rows), 0)
    col = lax.broadcasted_iota(jnp.int32, (rows, rows), 1)
    same = (row >> sh) == (col >> sh)
    strict = same & (row > col)
    incl = same & (row >= col)
    base = (row >> 1) == (col >> 1)
    merges = []
    s = 2
    while s < c:
        b = s.bit_length() - 1
        merges.append(((row >> (b + 1)) == (col >> (b + 1))) & (((row >> b) & 1) == 1) & (((col >> b) & 1) == 0))
        s *= 2
    eye = jnp.where(row == col, 1.0, 0.0).astype(F32)
    return strict, incl, base, merges, eye


def _cumsum_rows(x, c):
    t = lax.broadcasted_iota(jnp.int32, x.shape, 0)
    s = 1
    while s < c:
        x = x + jnp.where(t >= s, pltpu.roll(x, s, axis=0), 0.0)
        s *= 2
    return x


def _wkv_unit(r, lw, k, v, kk, b, s_list, c, masks):
    strict, incl, base, merges, eye = masks
    w = r.shape[1]
    nh = 128 // c
    lane_h = lax.broadcasted_iota(jnp.int32, (c, w), 1) >> 6

    def stack(x):
        return jnp.concatenate([jnp.where(lane_h == h, x, 0.0) for h in range(nh)], axis=0).astype(BF16)

    cum = _cumsum_rows(lw, c)
    last = cum[c - 1:c, :]
    p_c = jnp.exp(last)
    to_end = jnp.exp(last - cum)
    p_inv = jnp.exp(-cum)
    kk_s = stack(kk * jnp.exp(cum - lw))
    r_s = stack(r * jnp.exp(cum))
    k_s = stack(k * p_inv)
    b_s = stack(b * p_inv)
    v_s = stack(v)
    kd_s = stack(k * to_end)
    bd_s = stack(b * to_end)

    a_kk_k = _dot_nt(kk_s, k_s)
    a_kk_b = _dot_nt(kk_s, b_s)
    a_r_k = _dot_nt(r_s, k_s)
    a_r_b = _dot_nt(r_s, b_s)

    lm = jnp.where(strict, a_kk_b, 0.0)
    inv = eye - jnp.where(base, lm, 0.0)
    for m in merges:
        inv_b = inv.astype(BF16)
        inv = inv - _dot(inv_b, _dot(jnp.where(m, lm, 0.0).astype(BF16), inv_b).astype(BF16))

    npair = w // PAIR
    s_b = [s.astype(BF16) for s in s_list]
    kkh = [_dot_nt(kk_s[:, q * PAIR:(q + 1) * PAIR], s_b[q]) for q in range(npair)]
    rh = [_dot_nt(r_s[:, q * PAIR:(q + 1) * PAIR], s_b[q]) for q in range(npair)]
    kkh = kkh[0] if npair == 1 else jnp.concatenate(kkh, axis=1)
    rh = rh[0] if npair == 1 else jnp.concatenate(rh, axis=1)

    rhs = kkh + _dot(jnp.where(strict, a_kk_k, 0.0).astype(BF16), v_s)
    u = _dot(inv.astype(BF16), rhs.astype(BF16))
    u_b = u.astype(BF16)
    y_s = (rh + _dot(jnp.where(incl, a_r_k, 0.0).astype(BF16), v_s)
           - _dot(jnp.where(incl, a_r_b, 0.0).astype(BF16), u_b))
    y = y_s[0:c]
    for h in range(1, nh):
        y = y + y_s[h * c:(h + 1) * c]

    s_new = []
    for q in range(npair):
        sl = slice(q * PAIR, (q + 1) * PAIR)
        upd = _dot_tn(v_s[:, sl], kd_s[:, sl]) - _dot_tn(u_b[:, sl], bd_s[:, sl])
        s_new.append(s_list[q] * p_c[:, sl] + upd)
    return y, s_new


def _wkv_prompt_kernel(r_ref, lw_ref, k_ref, v_ref, kk_ref, b_ref, s0_ref, y_ref, sout_ref, s_ref,
                       *, c, t_len, units):
    ci = pl.program_id(2)

    @pl.when(ci == 0)
    def _():
        s_ref[...] = s0_ref[...]

    masks = _wkv_masks(c)
    rows = lax.broadcasted_iota(jnp.int32, (c, PAIR), 0) + ci * c
    valid = rows < t_len
    for q in range(units):
        sl = slice(q * PAIR, (q + 1) * PAIR)
        ins = [jnp.where(valid, ref[:, sl], 0.0) for ref in (r_ref, lw_ref, k_ref, v_ref, kk_ref, b_ref)]
        y, s_new = _wkv_unit(*ins, [s_ref[q]], c, masks)
        y_ref[:, sl] = y
        s_ref[q] = s_new[0]

    @pl.when(ci == pl.num_programs(2) - 1)
    def _():
        sout_ref[...] = s_ref[...]


def _wkv_sample_kernel(r_ref, lw_ref, k_ref, v_ref, kk_ref, b_ref, s0_ref, y_ref, sout_ref, *, c, units):
    masks = _wkv_masks(c)
    for i in range(units):
        sl = slice(i * c, (i + 1) * c)
        ins = [ref[sl, :] for ref in (r_ref, lw_ref, k_ref, v_ref, kk_ref, b_ref)]
        y, s_new = _wkv_unit(*ins, [s0_ref[i, q] for q in range(N_PAIRS)], c, masks)
        y_ref[sl, :] = y
        for q in range(N_PAIRS):
            sout_ref[i, q] = s_new[q]


def _wkv(ins, s_bd, bsz, t_len):
    m = bsz * t_len
    y_shape = jax.ShapeDtypeStruct((m, D_MODEL), F32)
    s_shape = jax.ShapeDtypeStruct(s_bd.shape, F32)
    if t_len > 64:
        c, units = 64, 4
        nc = pl.cdiv(t_len, c)
        ins3 = [a.reshape(bsz, t_len, D_MODEL) for a in ins]
        blk = pl.BlockSpec((None, c, units * PAIR), lambda b, g, i: (b, i, g))
        sblk = pl.BlockSpec((None, units, PAIR, PAIR), lambda b, g, i: (b, g, 0, 0))
        y, s_out = pl.pallas_call(
            functools.partial(_wkv_prompt_kernel, c=c, t_len=t_len, units=units),
            grid=(bsz, N_PAIRS // units, nc),
            in_specs=[blk] * 6 + [sblk],
            out_specs=[blk, sblk],
            out_shape=[jax.ShapeDtypeStruct((bsz, t_len, D_MODEL), F32), s_shape],
            scratch_shapes=[pltpu.VMEM((units, PAIR, PAIR), F32)],
            compiler_params=_params(("arbitrary", "arbitrary", "arbitrary")),
            name="wkv_prompt",
        )(*ins3, s_bd)
        return y.reshape(m, D_MODEL), s_out
    c, units = t_len, 2
    blk = pl.BlockSpec((units * c, D_MODEL), lambda i: (i, 0))
    sblk = pl.BlockSpec((units, N_PAIRS, PAIR, PAIR), lambda i: (i, 0, 0, 0))
    return pl.pallas_call(
        functools.partial(_wkv_sample_kernel, c=c, units=units),
        grid=(bsz // units,),
        in_specs=[blk] * 6 + [sblk],
        out_specs=[blk, sblk],
        out_shape=[y_shape, s_shape],
        compiler_params=_params(("parallel",)),
        name="wkv_sample",
    )(*ins, s_bd)


def _rk_post_kernel(x_ref, y_ref, r_ref, k_ref, v_ref, g_ref, lnw_ref, lnb_ref, rk_ref, wo_ref, o_ref):
    ones = _head_ones()
    y = y_ref[...]
    inv_n = 1.0 / HEAD_DIM
    mean = _head_sum(y, ones) * inv_n
    yc = y - mean
    var = _head_sum(yc * yc, ones) * inv_n
    yn = yc * lax.rsqrt(var + GN_EPS) * lnw_ref[...] + lnb_ref[...]
    bonus = _head_sum(r_ref[...] * k_ref[...] * rk_ref[...], ones) * v_ref[...]
    o = ((yn + bonus) * g_ref[...]).astype(BF16)
    o_ref[...] = x_ref[...] + _dot(o, wo_ref[...])


def _rk_post(x, y, r, k, v, g, ln_w, ln_b, r_k, w_o, tm):
    m = x.shape[0]
    row = pl.BlockSpec((tm, D_MODEL), lambda i: (i, 0))
    vec = _const_spec((1, D_MODEL))
    return pl.pallas_call(
        _rk_post_kernel,
        grid=(m // tm,),
        in_specs=[row] * 6 + [vec, vec, vec, _const_spec((D_MODEL, D_MODEL))],
        out_specs=row,
        out_shape=jax.ShapeDtypeStruct((m, D_MODEL), F32),
        compiler_params=_params(("parallel",)),
        name="rk_post",
    )(x, y, r, k, v, g, ln_w.reshape(1, D_MODEL), ln_b.reshape(1, D_MODEL), r_k.reshape(1, D_MODEL), w_o)


def _conv_core(x, h, shifted, win_ref, cw_ref, wout_ref):
    z = _dot(h.astype(BF16), win_ref[...])
    gate_b = z[:, 0:D_MODEL]
    u = z[:, D_MODEL:2 * D_MODEL] * z[:, 2 * D_MODEL:3 * D_MODEL]
    cw = cw_ref[...]
    u1, u2 = shifted(u)
    conv = u2 * cw[0:1, :] + u1 * cw[1:2, :] + u * cw[2:3, :]
    out = x + _dot((gate_b * conv).astype(BF16), wout_ref[...])
    return out, u


def _conv_prompt_kernel(x_ref, h_ref, c0_ref, win_ref, cw_ref, wout_ref, o_ref, st_ref, carry_ref, *, tm):
    @pl.when(pl.program_id(1) == 0)
    def _():
        carry_ref[0:2, :] = c0_ref[...]

    def shifted(u):
        row = lax.broadcasted_iota(jnp.int32, u.shape, 0)
        c0 = carry_ref[0:1, :]
        c1 = carry_ref[1:2, :]
        u1 = jnp.where(row == 0, c1, pltpu.roll(u, 1, axis=0))
        u2 = jnp.where(row == 0, c0, jnp.where(row == 1, c1, pltpu.roll(u, 2, axis=0)))
        return u1, u2

    out, u = _conv_core(x_ref[...], h_ref[...], shifted, win_ref, cw_ref, wout_ref)
    o_ref[...] = out
    carry_ref[0:2, :] = u[tm - 2:tm, :]
    st_ref[...] = u[tm - SUBLANES:tm, :]


def _conv_sample_kernel(x_ref, h_ref, c0_ref, win_ref, cw_ref, wout_ref, o_ref, u_ref, *, ns, t_len):
    rows = ns * t_len
    c0 = jnp.broadcast_to(c0_ref[:, 0:1, :], (ns, t_len, D_MODEL)).reshape(rows, D_MODEL)
    c1 = jnp.broadcast_to(c0_ref[:, 1:2, :], (ns, t_len, D_MODEL)).reshape(rows, D_MODEL)

    def shifted(u):
        t = lax.broadcasted_iota(jnp.int32, u.shape, 0) & (t_len - 1)
        u1 = jnp.where(t == 0, c1, pltpu.roll(u, 1, axis=0))
        u2 = jnp.where(t == 0, c0, jnp.where(t == 1, c1, pltpu.roll(u, 2, axis=0)))
        return u1, u2

    out, u = _conv_core(x_ref[...], h_ref[...], shifted, win_ref, cw_ref, wout_ref)
    o_ref[...] = out
    u_ref[...] = u


def _conv_mix(x, hn, conv0, w_in, conv_w, w_out, bsz, t_len, tm):
    m = bsz * t_len
    consts = [_const_spec((D_MODEL, 3 * D_MODEL)), _const_spec((CONV_W, D_MODEL)), _const_spec((D_MODEL, D_MODEL))]
    if t_len > tm:
        nt = t_len // tm
        row = pl.BlockSpec((tm, D_MODEL), lambda b, t: (b * nt + t, 0))
        out, tail = pl.pallas_call(
            functools.partial(_conv_prompt_kernel, tm=tm),
            grid=(bsz, nt),
            in_specs=[row, row, pl.BlockSpec((None, CONV_W - 1, D_MODEL), lambda b, t: (b, 0, 0))] + consts,
            out_specs=[row, pl.BlockSpec((None, SUBLANES, D_MODEL), lambda b, t: (b, 0, 0))],
            out_shape=[jax.ShapeDtypeStruct((m, D_MODEL), F32),
                       jax.ShapeDtypeStruct((bsz, SUBLANES, D_MODEL), F32)],
            scratch_shapes=[pltpu.VMEM((SUBLANES, D_MODEL), F32)],
            compiler_params=_params(("arbitrary", "arbitrary")),
            name="conv_prompt",
        )(x, hn, conv0, w_in, conv_w, w_out)
        return out, tail[:, SUBLANES - (CONV_W - 1):, :]
    ns = tm // t_len
    row = pl.BlockSpec((tm, D_MODEL), lambda i: (i, 0))
    out, u = pl.pallas_call(
        functools.partial(_conv_sample_kernel, ns=ns, t_len=t_len),
        grid=(m // tm,),
        in_specs=[row, row, pl.BlockSpec((ns, CONV_W - 1, D_MODEL), lambda i: (i, 0, 0))] + consts,
        out_specs=[row, row],
        out_shape=[jax.ShapeDtypeStruct((m, D_MODEL), F32)] * 2,
        compiler_params=_params(("parallel",)),
        name="conv_sample",
    )(x, hn, conv0, w_in, conv_w, w_out)
    return out, u.reshape(bsz, t_len, D_MODEL)[:, t_len - (CONV_W - 1):, :]


def _state_to_pairs(s):
    bsz = s.shape[0]
    s2 = s.reshape(bsz, N_PAIRS, 2, HEAD_DIM, HEAD_DIM)
    z = jnp.zeros_like(s2[:, :, 0])
    top = jnp.concatenate([s2[:, :, 0], z], axis=-1)
    bot = jnp.concatenate([z, s2[:, :, 1]], axis=-1)
    return jnp.concatenate([top, bot], axis=-2)


def _pairs_to_state(s_bd):
    bsz = s_bd.shape[0]
    h0 = s_bd[:, :, :HEAD_DIM, :HEAD_DIM]
    h1 = s_bd[:, :, HEAD_DIM:, HEAD_DIM:]
    return jnp.stack([h0, h1], axis=2).reshape(bsz, N_HEADS, HEAD_DIM, HEAD_DIM)


def _trunk(x, wkv_in, shift_in, conv_in, p, bsz, t_len, tm):
    x, hn = _ffn(x, p['ffn_norm'][0, 0], p['ffn_w_gu'][0, 0], p['ffn_w_down'][0, 0], p['mix_norm'][0], tm)
    new_shift = hn.reshape(bsz, t_len, D_MODEL)[:, -1]
    r, lw, k, v, kk, b, g = _rk_pre(hn, shift_in[0], p['rk_pre'], bsz, t_len, tm)
    y, s_bd = _wkv([r, lw, k, v, kk, b], _state_to_pairs(wkv_in[0]), bsz, t_len)
    x = _rk_post(x, y, r, k, v, g, p['rk_ln_w'][0], p['rk_ln_b'][0], p['rk_r_k'][0], p['rk_w_o'][0], tm)
    (x,) = _ffn(x, p['ffn_norm'][0, 1], p['ffn_w_gu'][0, 1], p['ffn_w_down'][0, 1], p['ffn_norm'][1, 0], tm,
                emit_n=False)
    x, hn = _ffn(x, p['ffn_norm'][1, 0], p['ffn_w_gu'][1, 0], p['ffn_w_down'][1, 0], p['mix_norm'][1], tm)
    x, new_conv = _conv_mix(x, hn, conv_in[0], p['sc_w_in'][0], p['sc_conv_w'][0], p['sc_w_out'][0],
                            bsz, t_len, tm)
    (y_out,) = _ffn(x, p['ffn_norm'][1, 1], p['ffn_w_gu'][1, 1], p['ffn_w_down'][1, 1], p['final_norm'], tm,
                    emit_x=False)
    return y_out, _pairs_to_state(s_bd)[None], new_shift[None], new_conv[None]


def kernel(x_prompt, x_sample, state_wkv, state_shift, state_conv, meta, ffn_norm, ffn_w_gu, ffn_w_down,
           mix_norm, final_norm, rk_mu, rk_w_rkv, rk_w0, rk_w1, rk_w2, rk_a0, rk_a1, rk_a2, rk_g1, rk_g2,
           rk_k_k, rk_k_a, rk_r_k, rk_ln_w, rk_ln_b, rk_w_o, sc_w_in, sc_conv_w, sc_w_out):
    assert DEPTH == 2 and rk_mu.shape[0] == 1 and sc_w_in.shape[0] == 1
    bf = lambda a: a.astype(BF16)
    vec = lambda a: a.reshape(1, D_MODEL)
    p = {
        'ffn_norm': ffn_norm, 'ffn_w_gu': bf(ffn_w_gu), 'ffn_w_down': bf(ffn_w_down), 'mix_norm': mix_norm,
        'final_norm': final_norm,
        'rk_pre': [rk_mu[0], bf(rk_w_rkv[0, 0]), bf(rk_w_rkv[0, 1]), bf(rk_w_rkv[0, 2]), vec(rk_w0[0]),
                   bf(rk_w1[0]), bf(rk_w2[0]), vec(rk_a0[0]), bf(rk_a1[0]), bf(rk_a2[0]), bf(rk_g1[0]),
                   bf(rk_g2[0]), vec(rk_k_k[0]), vec(rk_k_a[0])],
        'rk_r_k': rk_r_k.reshape(1, D_MODEL), 'rk_ln_w': rk_ln_w, 'rk_ln_b': rk_ln_b, 'rk_w_o': bf(rk_w_o),
        'sc_w_in': bf(sc_w_in), 'sc_conv_w': sc_conv_w, 'sc_w_out': bf(sc_w_out),
    }
    bp, seq, _ = x_prompt.shape
    t_p = seq + N_META
    xp = jnp.concatenate([jnp.broadcast_to(meta.astype(x_prompt.dtype)[None], (bp, N_META, D_MODEL)), x_prompt],
                         axis=1).reshape(bp * t_p, D_MODEL)
    wkv0 = jnp.zeros((1, bp, N_HEADS, HEAD_DIM, HEAD_DIM), state_wkv.dtype)
    shift0 = jnp.zeros((1, bp, D_MODEL), state_shift.dtype)
    conv0 = jnp.zeros((1, bp, CONV_W - 1, D_MODEL), state_conv.dtype)
    yp, wkv_p, shift_p, conv_p = _trunk(xp, wkv0, shift0, conv0, p, bp, t_p, 344)
    y_prompt = yp.reshape(bp, t_p, D_MODEL)[:, N_META:]
    bs, t_s, _ = x_sample.shape
    ys, wkv_s, shift_s, conv_s = _trunk(x_sample.reshape(bs * t_s, D_MODEL), state_wkv, state_shift, state_conv,
                                        p, bs, t_s, 256)
    return (y_prompt, ys.reshape(bs, t_s, D_MODEL), wkv_p, shift_p, conv_p, wkv_s, shift_s, conv_s)
```

```python
import functools

import jax
import jax.numpy as jnp
from jax import lax
from jax.experimental import pallas as pl
from jax.experimental.pallas import tpu as pltpu

D_MODEL = 1024
HEAD_DIM = 64
N_HEADS = D_MODEL // HEAD_DIM
D_FF = 2816
N_META = 16
DEPTH = 2
CONV_W = 3
RMS_EPS = 1e-6
GN_EPS = 64e-5

LANES = 128
SUBLANES = 8
MXU_DIM = 256
PAIR = 2 * HEAD_DIM
N_PAIRS = D_MODEL // PAIR
VMEM_LIMIT = 56 * 1024 * 1024

FFN_CHUNKS = ((0, 6 * MXU_DIM), (6 * MXU_DIM, D_FF))

F32 = jnp.float32
BF16 = jnp.bfloat16


def _dot(a, b):
    return jnp.dot(a, b, preferred_element_type=F32)


def _dot_nt(a, b):
    return lax.dot_general(a, b, (((1,), (1,)), ((), ())), preferred_element_type=F32)


def _dot_tn(a, b):
    return lax.dot_general(a, b, (((0,), (0,)), ((), ())), preferred_element_type=F32)


def _rms(x, g):
    ms = jnp.mean(x * x, axis=-1, keepdims=True)
    return x * lax.rsqrt(ms + RMS_EPS) * g


def _sigmoid(x):
    return 1.0 / (1.0 + jnp.exp(-x))


def _head_ones():
    r = lax.broadcasted_iota(jnp.int32, (MXU_DIM, MXU_DIM), 0) >> 6
    c = lax.broadcasted_iota(jnp.int32, (MXU_DIM, MXU_DIM), 1) >> 6
    return jnp.where(r == c, 1.0, 0.0).astype(BF16)


def _head_sum(x, ones):
    hi = x.astype(BF16)
    lo = (x - hi.astype(F32)).astype(BF16)
    cols = []
    for j in range(D_MODEL // MXU_DIM):
        sl = slice(j * MXU_DIM, (j + 1) * MXU_DIM)
        cols.append(_dot(hi[:, sl], ones) + _dot(lo[:, sl], ones))
    return jnp.concatenate(cols, axis=1)


def _const_spec(shape):
    nd = len(shape)
    return pl.BlockSpec(shape, lambda *_: (0,) * nd, pipeline_mode=pl.Buffered(1))


def _params(sem):
    return pltpu.CompilerParams(dimension_semantics=sem, vmem_limit_bytes=VMEM_LIMIT)


def _ffn_kernel(x_ref, g_ref, wgu_ref, wd_ref, g2_ref, *out_refs, emit_x, emit_n):
    x = x_ref[...]
    xn = _rms(x, g_ref[...]).astype(BF16)
    acc = None
    for lo, hi in FFN_CHUNKS:
        gate = _dot(xn, wgu_ref[:, lo:hi])
        up = _dot(xn, wgu_ref[:, D_FF + lo:D_FF + hi])
        act = (gate * _sigmoid(gate) * up).astype(BF16)
        part = _dot(act, wd_ref[lo:hi, :])
        acc = part if acc is None else acc + part
    out = x + 0.5 * acc
    i = 0
    if emit_x:
        out_refs[i][...] = out
        i += 1
    if emit_n:
        out_refs[i][...] = _rms(out, g2_ref[...])


def _ffn(x, g, wgu, wd, g2, tm, emit_x=True, emit_n=True):
    m = x.shape[0]
    row = pl.BlockSpec((tm, D_MODEL), lambda i: (i, 0))
    n_out = int(emit_x) + int(emit_n)
    outs = pl.pallas_call(
        functools.partial(_ffn_kernel, emit_x=emit_x, emit_n=emit_n),
        grid=(m // tm,),
        in_specs=[row, _const_spec((1, D_MODEL)), _const_spec((D_MODEL, 2 * D_FF)),
                  _const_spec((D_FF, D_MODEL)), _const_spec((1, D_MODEL))],
        out_specs=[row] * n_out,
        out_shape=[jax.ShapeDtypeStruct((m, D_MODEL), F32)] * n_out,
        compiler_params=_params(("parallel",)),
        name="ffn",
    )(x, g.reshape(1, D_MODEL), wgu, wd, g2.reshape(1, D_MODEL))
    return tuple(outs)


def _rk_pre_core(h, prev, p, out_refs):
    (mu, wr, wk, wv, w0, w1, w2, a0, a1, a2, g1, g2, k_k, k_a) = [r[...] for r in p]
    xx = prev - h
    mix = [(h + xx * mu[c:c + 1, :]).astype(BF16) for c in range(6)]
    xr, xw, xk, xv, xa, xg = mix
    r = _dot(xr, wr)
    k = _dot(xk, wk)
    v = _dot(xv, wv)
    z = w0 + _dot(jnp.tanh(_dot(xw, w1)).astype(BF16), w2)
    nz = -z
    sp = jnp.maximum(nz, 0.0) + jnp.log(1.0 + jnp.exp(-jnp.abs(nz)))
    lw = -jnp.exp(-sp - 0.5)
    a = _sigmoid(a0 + _dot(_dot(xa, a1).astype(BF16), a2))
    g = _dot(_sigmoid(_dot(xg, g1)).astype(BF16), g2)
    kk = k * k_k
    ss = _head_sum(kk * kk, _head_ones())
    kk = kk / jnp.maximum(jnp.sqrt(ss), 1e-12)
    kmod = k * (1.0 + (a - 1.0) * k_a)
    r_ref, lw_ref, k_ref, v_ref, kk_ref, b_ref, g_ref = out_refs
    r_ref[...] = r
    lw_ref[...] = lw
    k_ref[...] = kmod
    v_ref[...] = v
    kk_ref[...] = kk
    b_ref[...] = kk * a
    g_ref[...] = g


def _rk_pre_prompt_kernel(h_ref, s0_ref, *refs, tm):
    p, out_refs, carry_ref = refs[:14], refs[14:21], refs[21]

    @pl.when(pl.program_id(1) == 0)
    def _():
        carry_ref[0:1, :] = s0_ref[...]

    h = h_ref[...]
    row = lax.broadcasted_iota(jnp.int32, h.shape, 0)
    prev = jnp.where(row == 0, carry_ref[0:1, :], pltpu.roll(h, 1, axis=0))
    carry_ref[0:1, :] = h[tm - 1:tm, :]
    _rk_pre_core(h, prev, p, out_refs)


def _rk_pre_sample_kernel(h_ref, s0_ref, *refs, ns, t_len):
    p, out_refs = refs[:14], refs[14:21]
    h = h_ref[...]
    rows = ns * t_len
    s0 = jnp.broadcast_to(s0_ref[...], (ns, t_len, D_MODEL)).reshape(rows, D_MODEL)
    t = lax.broadcasted_iota(jnp.int32, h.shape, 0) & (t_len - 1)
    prev = jnp.where(t == 0, s0, pltpu.roll(h, 1, axis=0))
    _rk_pre_core(h, prev, p, out_refs)


def _rk_param_specs(p):
    return [_const_spec(a.shape) for a in p]


def _rk_pre(hn, shift0, p, bsz, t_len, tm):
    m = bsz * t_len
    out_shape = [jax.ShapeDtypeStruct((m, D_MODEL), F32)] * 7
    s0 = shift0.reshape(bsz, 1, D_MODEL)
    if t_len > tm:
        nt = t_len // tm
        row = pl.BlockSpec((tm, D_MODEL), lambda b, t: (b * nt + t, 0))
        return pl.pallas_call(
            functools.partial(_rk_pre_prompt_kernel, tm=tm),
            grid=(bsz, nt),
            in_specs=[row, pl.BlockSpec((None, 1, D_MODEL), lambda b, t: (b, 0, 0))] + _rk_param_specs(p),
            out_specs=[row] * 7,
            out_shape=out_shape,
            scratch_shapes=[pltpu.VMEM((SUBLANES, D_MODEL), F32)],
            compiler_params=_params(("arbitrary", "arbitrary")),
            name="rk_pre_prompt",
        )(hn, s0, *p)
    ns = tm // t_len
    row = pl.BlockSpec((tm, D_MODEL), lambda i: (i, 0))
    return pl.pallas_call(
        functools.partial(_rk_pre_sample_kernel, ns=ns, t_len=t_len),
        grid=(m // tm,),
        in_specs=[row, pl.BlockSpec((ns, 1, D_MODEL), lambda i: (i, 0, 0))] + _rk_param_specs(p),
        out_specs=[row] * 7,
        out_shape=out_shape,
        compiler_params=_params(("parallel",)),
        name="rk_pre_sample",
    )(hn, s0, *p)


def _wkv_masks(c):
    rows = 128
    sh = c.bit_length() - 1
    row = lax.broadcasted_iota(jnp.int32, (rows, rows), 0)
    col = lax.broadcasted_iota(jnp.int32, (rows, rows), 1)
    same = (row >> sh) == (col >> sh)
    strict = same & (row > col)
    incl = same & (row >= col)
    base = (row >> 1) == (col >> 1)
    merges = []
    s = 2
    while s < c:
        b = s.bit_length() - 1
        merges.append(((row >> (b + 1)) == (col >> (b + 1))) & (((row >> b) & 1) == 1) & (((col >> b) & 1) == 0))
        s *= 2
    eye = jnp.where(row == col, 1.0, 0.0).astype(F32)
    return strict, incl, base, merges, eye


def _cumsum_rows(x, c):
    t = lax.broadcasted_iota(jnp.int32, x.shape, 0)
    s = 1
    while s < c:
        x = x + jnp.where(t >= s, pltpu.roll(x, s, axis=0), 0.0)
        s *= 2
    return x


def _wkv_units(ins, states, c, masks):
    strict, incl, base, merges, eye = masks
    w = ins[0][0].shape[1]
    nh = 128 // c
    npair = w // PAIR
    lane_h = lax.broadcasted_iota(jnp.int32, (c, w), 1) >> 6
    pair_lanes = [slice(q * PAIR, (q + 1) * PAIR) for q in range(npair)]

    def stack(x):
        return jnp.concatenate([jnp.where(lane_h == h, x, 0.0) for h in range(nh)], axis=0).astype(BF16)

    def each(f, *lists):
        return [f(*xs) for xs in zip(*lists)]

    def prep(r, lw, k, v, kk, b):
        cum = _cumsum_rows(lw, c)
        last = cum[c - 1:c, :]
        to_end = jnp.exp(last - cum)
        p_inv = jnp.exp(-cum)
        return dict(p_c=jnp.exp(last), kk=stack(kk * jnp.exp(cum - lw)), r=stack(r * jnp.exp(cum)),
                    k=stack(k * p_inv), b=stack(b * p_inv), v=stack(v), kd=stack(k * to_end),
                    bd=stack(b * to_end))

    o = [prep(*x) for x in ins]
    a_kk_b = [_dot_nt(x['kk'], x['b']) for x in o]
    a_kk_k = [_dot_nt(x['kk'], x['k']) for x in o]
    a_r_k = [_dot_nt(x['r'], x['k']) for x in o]
    a_r_b = [_dot_nt(x['r'], x['b']) for x in o]

    lm = [jnp.where(strict, a, 0.0) for a in a_kk_b]
    inv = [eye - jnp.where(base, l, 0.0) for l in lm]
    for m in merges:
        inv_b = [t.astype(BF16) for t in inv]
        mid = each(lambda l, t: _dot(jnp.where(m, l, 0.0).astype(BF16), t).astype(BF16), lm, inv_b)
        inv = each(lambda t, tb, md: t - _dot(tb, md), inv, inv_b, mid)

    def state_dot(x_s, s_b):
        cols = [_dot_nt(x_s[:, sl], s) for sl, s in zip(pair_lanes, s_b)]
        return cols[0] if npair == 1 else jnp.concatenate(cols, axis=1)

    s_b = [[s.astype(BF16) for s in sl] for sl in states]
    kkh = each(lambda x, s: state_dot(x['kk'], s), o, s_b)
    rh = each(lambda x, s: state_dot(x['r'], s), o, s_b)
    rhs = each(lambda h, a, x: h + _dot(jnp.where(strict, a, 0.0).astype(BF16), x['v']), kkh, a_kk_k, o)
    u_b = each(lambda t, z: _dot(t.astype(BF16), z.astype(BF16)).astype(BF16), inv, rhs)
    y_s = each(lambda h, ak, ab, x, u: h + _dot(jnp.where(incl, ak, 0.0).astype(BF16), x['v'])
               - _dot(jnp.where(incl, ab, 0.0).astype(BF16), u), rh, a_r_k, a_r_b, o, u_b)

    def fold(ys):
        y = ys[0:c]
        for h in range(1, nh):
            y = y + ys[h * c:(h + 1) * c]
        return y

    def new_state(x, u, s_list):
        return [s * x['p_c'][:, sl] + _dot_tn(x['v'][:, sl], x['kd'][:, sl]) - _dot_tn(u[:, sl], x['bd'][:, sl])
                for sl, s in zip(pair_lanes, s_list)]

    return [fold(ys) for ys in y_s], each(new_state, o, u_b, states)


def _wkv_prompt_kernel(r_ref, lw_ref, k_ref, v_ref, kk_ref, b_ref, s0_ref, y_ref, sout_ref, s_ref,
                       *, c, t_len, units):
    ci = pl.program_id(2)

    @pl.when(ci == 0)
    def _():
        s_ref[...] = s0_ref[...]

    masks = _wkv_masks(c)
    rows = lax.broadcasted_iota(jnp.int32, (c, PAIR), 0) + ci * c
    valid = rows < t_len
    lanes = [slice(q * PAIR, (q + 1) * PAIR) for q in range(units)]
    ins = [[jnp.where(valid, ref[:, sl], 0.0) for ref in (r_ref, lw_ref, k_ref, v_ref, kk_ref, b_ref)]
           for sl in lanes]
    ys, s_new = _wkv_units(ins, [[s_ref[q]] for q in range(units)], c, masks)
    for q in range(units):
        y_ref[:, lanes[q]] = ys[q]
        s_ref[q] = s_new[q][0]

    @pl.when(ci == pl.num_programs(2) - 1)
    def _():
        sout_ref[...] = s_ref[...]


def _wkv_sample_kernel(r_ref, lw_ref, k_ref, v_ref, kk_ref, b_ref, s0_ref, y_ref, sout_ref, *, c, units):
    masks = _wkv_masks(c)
    rows = [slice(i * c, (i + 1) * c) for i in range(units)]
    ins = [[ref[sl, :] for ref in (r_ref, lw_ref, k_ref, v_ref, kk_ref, b_ref)] for sl in rows]
    ys, s_new = _wkv_units(ins, [[s0_ref[i, q] for q in range(N_PAIRS)] for i in range(units)], c, masks)
    for i in range(units):
        y_ref[rows[i], :] = ys[i]
        for q in range(N_PAIRS):
            sout_ref[i, q] = s_new[i][q]


def _wkv(ins, s_bd, bsz, t_len):
    m = bsz * t_len
    y_shape = jax.ShapeDtypeStruct((m, D_MODEL), F32)
    s_shape = jax.ShapeDtypeStruct(s_bd.shape, F32)
    if t_len > 64:
        c, units = 64, 8
        nc = pl.cdiv(t_len, c)
        ins3 = [a.reshape(bsz, t_len, D_MODEL) for a in ins]
        blk = pl.BlockSpec((None, c, units * PAIR), lambda b, g, i: (b, i, g))
        sblk = pl.BlockSpec((None, units, PAIR, PAIR), lambda b, g, i: (b, g, 0, 0))
        y, s_out = pl.pallas_call(
            functools.partial(_wkv_prompt_kernel, c=c, t_len=t_len, units=units),
            grid=(bsz, N_PAIRS // units, nc),
            in_specs=[blk] * 6 + [sblk],
            out_specs=[blk, sblk],
            out_shape=[jax.ShapeDtypeStruct((bsz, t_len, D_MODEL), F32), s_shape],
            scratch_shapes=[pltpu.VMEM((units, PAIR, PAIR), F32)],
            compiler_params=_params(("arbitrary", "arbitrary", "arbitrary")),
            name="wkv_prompt",
        )(*ins3, s_bd)
        return y.reshape(m, D_MODEL), s_out
    c, units = t_len, 4
    blk = pl.BlockSpec((units * c, D_MODEL), lambda i: (i, 0))
    sblk = pl.BlockSpec((units, N_PAIRS, PAIR, PAIR), lambda i: (i, 0, 0, 0))
    return pl.pallas_call(
        functools.partial(_wkv_sample_kernel, c=c, units=units),
        grid=(bsz // units,),
        in_specs=[blk] * 6 + [sblk],
        out_specs=[blk, sblk],
        out_shape=[y_shape, s_shape],
        compiler_params=_params(("parallel",)),
        name="wkv_sample",
    )(*ins, s_bd)


def _rk_post_kernel(x_ref, y_ref, r_ref, k_ref, v_ref, g_ref, lnw_ref, lnb_ref, rk_ref, wo_ref, o_ref):
    ones = _head_ones()
    y = y_ref[...]
    inv_n = 1.0 / HEAD_DIM
    mean = _head_sum(y, ones) * inv_n
    yc = y - mean
    var = _head_sum(yc * yc, ones) * inv_n
    yn = yc * lax.rsqrt(var + GN_EPS) * lnw_ref[...] + lnb_ref[...]
    bonus = _head_sum(r_ref[...] * k_ref[...] * rk_ref[...], ones) * v_ref[...]
    o = ((yn + bonus) * g_ref[...]).astype(BF16)
    o_ref[...] = x_ref[...] + _dot(o, wo_ref[...])


def _rk_post(x, y, r, k, v, g, ln_w, ln_b, r_k, w_o, tm):
    m = x.shape[0]
    row = pl.BlockSpec((tm, D_MODEL), lambda i: (i, 0))
    vec = _const_spec((1, D_MODEL))
    return pl.pallas_call(
        _rk_post_kernel,
        grid=(m // tm,),
        in_specs=[row] * 6 + [vec, vec, vec, _const_spec((D_MODEL, D_MODEL))],
        out_specs=row,
        out_shape=jax.ShapeDtypeStruct((m, D_MODEL), F32),
        compiler_params=_params(("parallel",)),
        name="rk_post",
    )(x, y, r, k, v, g, ln_w.reshape(1, D_MODEL), ln_b.reshape(1, D_MODEL), r_k.reshape(1, D_MODEL), w_o)


def _conv_core(x, h, shifted, win_ref, cw_ref, wout_ref):
    z = _dot(h.astype(BF16), win_ref[...])
    gate_b = z[:, 0:D_MODEL]
    u = z[:, D_MODEL:2 * D_MODEL] * z[:, 2 * D_MODEL:3 * D_MODEL]
    cw = cw_ref[...]
    u1, u2 = shifted(u)
    conv = u2 * cw[0:1, :] + u1 * cw[1:2, :] + u * cw[2:3, :]
    out = x + _dot((gate_b * conv).astype(BF16), wout_ref[...])
    return out, u


def _conv_prompt_kernel(x_ref, h_ref, c0_ref, win_ref, cw_ref, wout_ref, o_ref, st_ref, carry_ref, *, tm):
    @pl.when(pl.program_id(1) == 0)
    def _():
        carry_ref[0:2, :] = c0_ref[...]

    def shifted(u):
        row = lax.broadcasted_iota(jnp.int32, u.shape, 0)
        c0 = carry_ref[0:1, :]
        c1 = carry_ref[1:2, :]
        u1 = jnp.where(row == 0, c1, pltpu.roll(u, 1, axis=0))
        u2 = jnp.where(row == 0, c0, jnp.where(row == 1, c1, pltpu.roll(u, 2, axis=0)))
        return u1, u2

    out, u = _conv_core(x_ref[...], h_ref[...], shifted, win_ref, cw_ref, wout_ref)
    o_ref[...] = out
    carry_ref[0:2, :] = u[tm - 2:tm, :]
    st_ref[...] = u[tm - SUBLANES:tm, :]


def _conv_sample_kernel(x_ref, h_ref, c0_ref, win_ref, cw_ref, wout_ref, o_ref, u_ref, *, ns, t_len):
    rows = ns * t_len
    c0 = jnp.broadcast_to(c0_ref[:, 0:1, :], (ns, t_len, D_MODEL)).reshape(rows, D_MODEL)
    c1 = jnp.broadcast_to(c0_ref[:, 1:2, :], (ns, t_len, D_MODEL)).reshape(rows, D_MODEL)

    def shifted(u):
        t = lax.broadcasted_iota(jnp.int32, u.shape, 0) & (t_len - 1)
        u1 = jnp.where(t == 0, c1, pltpu.roll(u, 1, axis=0))
        u2 = jnp.where(t == 0, c0, jnp.where(t == 1, c1, pltpu.roll(u, 2, axis=0)))
        return u1, u2

    out, u = _conv_core(x_ref[...], h_ref[...], shifted, win_ref, cw_ref, wout_ref)
    o_ref[...] = out
    u_ref[...] = u


def _conv_mix(x, hn, conv0, w_in, conv_w, w_out, bsz, t_len, tm):
    m = bsz * t_len
    consts = [_const_spec((D_MODEL, 3 * D_MODEL)), _const_spec((CONV_W, D_MODEL)), _const_spec((D_MODEL, D_MODEL))]
    if t_len > tm:
        nt = t_len // tm
        row = pl.BlockSpec((tm, D_MODEL), lambda b, t: (b * nt + t, 0))
        out, tail = pl.pallas_call(
            functools.partial(_conv_prompt_kernel, tm=tm),
            grid=(bsz, nt),
            in_specs=[row, row, pl.BlockSpec((None, CONV_W - 1, D_MODEL), lambda b, t: (b, 0, 0))] + consts,
            out_specs=[row, pl.BlockSpec((None, SUBLANES, D_MODEL), lambda b, t: (b, 0, 0))],
            out_shape=[jax.ShapeDtypeStruct((m, D_MODEL), F32),
                       jax.ShapeDtypeStruct((bsz, SUBLANES, D_MODEL), F32)],
            scratch_shapes=[pltpu.VMEM((SUBLANES, D_MODEL), F32)],
            compiler_params=_params(("arbitrary", "arbitrary")),
            name="conv_prompt",
        )(x, hn, conv0, w_in, conv_w, w_out)
        return out, tail[:, SUBLANES - (CONV_W - 1):, :]
    ns = tm // t_len
    row = pl.BlockSpec((tm, D_MODEL), lambda i: (i, 0))
    out, u = pl.pallas_call(
        functools.partial(_conv_sample_kernel, ns=ns, t_len=t_len),
        grid=(m // tm,),
        in_specs=[row, row, pl.BlockSpec((ns, CONV_W - 1, D_MODEL), lambda i: (i, 0, 0))] + consts,
        out_specs=[row, row],
        out_shape=[jax.ShapeDtypeStruct((m, D_MODEL), F32)] * 2,
        compiler_params=_params(("parallel",)),
        name="conv_sample",
    )(x, hn, conv0, w_in, conv_w, w_out)
    return out, u.reshape(bsz, t_len, D_MODEL)[:, t_len - (CONV_W - 1):, :]


def _state_to_pairs(s):
    bsz = s.shape[0]
    s2 = s.reshape(bsz, N_PAIRS, 2, HEAD_DIM, HEAD_DIM)
    z = jnp.zeros_like(s2[:, :, 0])
    top = jnp.concatenate([s2[:, :, 0], z], axis=-1)
    bot = jnp.concatenate([z, s2[:, :, 1]], axis=-1)
    return jnp.concatenate([top, bot], axis=-2)


def _pairs_to_state(s_bd):
    bsz = s_bd.shape[0]
    h0 = s_bd[:, :, :HEAD_DIM, :HEAD_DIM]
    h1 = s_bd[:, :, HEAD_DIM:, HEAD_DIM:]
    return jnp.stack([h0, h1], axis=2).reshape(bsz, N_HEADS, HEAD_DIM, HEAD_DIM)


def _trunk(x, wkv_in, shift_in, conv_in, p, bsz, t_len, tm):
    x, hn = _ffn(x, p['ffn_norm'][0, 0], p['ffn_w_gu'][0, 0], p['ffn_w_down'][0, 0], p['mix_norm'][0], tm)
    new_shift = hn.reshape(bsz, t_len, D_MODEL)[:, -1]
    r, lw, k, v, kk, b, g = _rk_pre(hn, shift_in[0], p['rk_pre'], bsz, t_len, tm)
    y, s_bd = _wkv([r, lw, k, v, kk, b], _state_to_pairs(wkv_in[0]), bsz, t_len)
    x = _rk_post(x, y, r, k, v, g, p['rk_ln_w'][0], p['rk_ln_b'][0], p['rk_r_k'][0], p['rk_w_o'][0], tm)
    (x,) = _ffn(x, p['ffn_norm'][0, 1], p['ffn_w_gu'][0, 1], p['ffn_w_down'][0, 1], p['ffn_norm'][1, 0], tm,
                emit_n=False)
    x, hn = _ffn(x, p['ffn_norm'][1, 0], p['ffn_w_gu'][1, 0], p['ffn_w_down'][1, 0], p['mix_norm'][1], tm)
    x, new_conv = _conv_mix(x, hn, conv_in[0], p['sc_w_in'][0], p['sc_conv_w'][0], p['sc_w_out'][0],
                            bsz, t_len, tm)
    (y_out,) = _ffn(x, p['ffn_norm'][1, 1], p['ffn_w_gu'][1, 1], p['ffn_w_down'][1, 1], p['final_norm'], tm,
                    emit_x=False)
    return y_out, _pairs_to_state(s_bd)[None], new_shift[None], new_conv[None]


def kernel(x_prompt, x_sample, state_wkv, state_shift, state_conv, meta, ffn_norm, ffn_w_gu, ffn_w_down,
           mix_norm, final_norm, rk_mu, rk_w_rkv, rk_w0, rk_w1, rk_w2, rk_a0, rk_a1, rk_a2, rk_g1, rk_g2,
           rk_k_k, rk_k_a, rk_r_k, rk_ln_w, rk_ln_b, rk_w_o, sc_w_in, sc_conv_w, sc_w_out):
    assert DEPTH == 2 and rk_mu.shape[0] == 1 and sc_w_in.shape[0] == 1
    bf = lambda a: a.astype(BF16)
    vec = lambda a: a.reshape(1, D_MODEL)
    p = {
        'ffn_norm': ffn_norm, 'ffn_w_gu': bf(ffn_w_gu), 'ffn_w_down': bf(ffn_w_down), 'mix_norm': mix_norm,
        'final_norm': final_norm,
        'rk_pre': [rk_mu[0], bf(rk_w_rkv[0, 0]), bf(rk_w_rkv[0, 1]), bf(rk_w_rkv[0, 2]), vec(rk_w0[0]),
                   bf(rk_w1[0]), bf(rk_w2[0]), vec(rk_a0[0]), bf(rk_a1[0]), bf(rk_a2[0]), bf(rk_g1[0]),
                   bf(rk_g2[0]), vec(rk_k_k[0]), vec(rk_k_a[0])],
        'rk_r_k': rk_r_k.reshape(1, D_MODEL), 'rk_ln_w': rk_ln_w, 'rk_ln_b': rk_ln_b, 'rk_w_o': bf(rk_w_o),
        'sc_w_in': bf(sc_w_in), 'sc_conv_w': sc_conv_w, 'sc_w_out': bf(sc_w_out),
    }
    bp, seq, _ = x_prompt.shape
    t_p = seq + N_META
    xp = jnp.concatenate([jnp.broadcast_to(meta.astype(x_prompt.dtype)[None], (bp, N_META, D_MODEL)), x_prompt],
                         axis=1).reshape(bp * t_p, D_MODEL)
    wkv0 = jnp.zeros((1, bp, N_HEADS, HEAD_DIM, HEAD_DIM), state_wkv.dtype)
    shift0 = jnp.zeros((1, bp, D_MODEL), state_shift.dtype)
    conv0 = jnp.zeros((1, bp, CONV_W - 1, D_MODEL), state_conv.dtype)
    yp, wkv_p, shift_p, conv_p = _trunk(xp, wkv0, shift0, conv0, p, bp, t_p, 344)
    y_prompt = yp.reshape(bp, t_p, D_MODEL)[:, N_META:]
    bs, t_s, _ = x_sample.shape
    ys, wkv_s, shift_s, conv_s = _trunk(x_sample.reshape(bs * t_s, D_MODEL), state_wkv, state_shift, state_conv,
                                        p, bs, t_s, 256)
    return (y_prompt, ys.reshape(bs, t_s, D_MODEL), wkv_p, shift_p, conv_p, wkv_s, shift_s, conv_s)
```

```python
import functools

import jax
import jax.numpy as jnp
from jax import lax
from jax.experimental import pallas as pl
from jax.experimental.pallas import tpu as pltpu

D_MODEL = 1024
HEAD_DIM = 64
N_HEADS = D_MODEL // HEAD_DIM
D_FF = 2816
N_META = 16
DEPTH = 2
CONV_W = 3
RMS_EPS = 1e-6
GN_EPS = 64e-5

LANES = 128
SUBLANES = 8
MXU_DIM = 256
PAIR = 2 * HEAD_DIM
N_PAIRS = D_MODEL // PAIR
VMEM_LIMIT = 56 * 1024 * 1024

FFN_CHUNKS = ((0, 6 * MXU_DIM), (6 * MXU_DIM, D_FF))

F32 = jnp.float32
BF16 = jnp.bfloat16


def _dot(a, b):
    return jnp.dot(a, b, preferred_element_type=F32)


def _dot_nt(a, b):
    return lax.dot_general(a, b, (((1,), (1,)), ((), ())), preferred_element_type=F32)


def _dot_tn(a, b):
    return lax.dot_general(a, b, (((0,), (0,)), ((), ())), preferred_element_type=F32)


def _rms(x, g):
    ms = jnp.mean(x * x, axis=-1, keepdims=True)
    return x * lax.rsqrt(ms + RMS_EPS) * g


def _sigmoid(x):
    return 1.0 / (1.0 + jnp.exp(-x))


def _head_ones():
    r = lax.broadcasted_iota(jnp.int32, (MXU_DIM, MXU_DIM), 0) >> 6
    c = lax.broadcasted_iota(jnp.int32, (MXU_DIM, MXU_DIM), 1) >> 6
    return jnp.where(r == c, 1.0, 0.0).astype(BF16)


def _head_sum(x, ones):
    hi = x.astype(BF16)
    lo = (x - hi.astype(F32)).astype(BF16)
    cols = []
    for j in range(D_MODEL // MXU_DIM):
        sl = slice(j * MXU_DIM, (j + 1) * MXU_DIM)
        cols.append(_dot(hi[:, sl], ones) + _dot(lo[:, sl], ones))
    return jnp.concatenate(cols, axis=1)


def _const_spec(shape):
    nd = len(shape)
    return pl.BlockSpec(shape, lambda *_: (0,) * nd, pipeline_mode=pl.Buffered(1))


def _params(sem):
    return pltpu.CompilerParams(dimension_semantics=sem, vmem_limit_bytes=VMEM_LIMIT)


def _ffn_kernel(x_ref, g_ref, wgu_ref, wd_ref, g2_ref, *out_refs, emit_x, emit_n):
    x = x_ref[...]
    xn = _rms(x, g_ref[...]).astype(BF16)
    acc = None
    for lo, hi in FFN_CHUNKS:
        gate = _dot(xn, wgu_ref[:, lo:hi])
        up = _dot(xn, wgu_ref[:, D_FF + lo:D_FF + hi])
        act = (gate * _sigmoid(gate) * up).astype(BF16)
        part = _dot(act, wd_ref[lo:hi, :])
        acc = part if acc is None else acc + part
    out = x + 0.5 * acc
    i = 0
    if emit_x:
        out_refs[i][...] = out
        i += 1
    if emit_n:
        out_refs[i][...] = _rms(out, g2_ref[...]).astype(out_refs[i].dtype)


def _ffn(x, g, wgu, wd, g2, tm, emit_x=True, emit_n=True, n_dtype=F32):
    m = x.shape[0]
    row = pl.BlockSpec((tm, D_MODEL), lambda i: (i, 0))
    n_out = int(emit_x) + int(emit_n)
    dtypes = [F32] * int(emit_x) + [n_dtype] * int(emit_n)
    outs = pl.pallas_call(
        functools.partial(_ffn_kernel, emit_x=emit_x, emit_n=emit_n),
        grid=(m // tm,),
        in_specs=[row, _const_spec((1, D_MODEL)), _const_spec((D_MODEL, 2 * D_FF)),
                  _const_spec((D_FF, D_MODEL)), _const_spec((1, D_MODEL))],
        out_specs=[row] * n_out,
        out_shape=[jax.ShapeDtypeStruct((m, D_MODEL), dt) for dt in dtypes],
        compiler_params=_params(("parallel",)),
        name="ffn",
    )(x, g.reshape(1, D_MODEL), wgu, wd, g2.reshape(1, D_MODEL))
    return tuple(outs)


def _rk_pre_core(h, prev, p, out_refs):
    (mu, wr, wk, wv, w0, w1, w2, a0, a1, a2, g1, g2, k_k, k_a) = [r[...] for r in p]
    xx = prev - h
    mix = [(h + xx * mu[c:c + 1, :]).astype(BF16) for c in range(6)]
    xr, xw, xk, xv, xa, xg = mix
    r = _dot(xr, wr)
    k = _dot(xk, wk)
    v = _dot(xv, wv)
    z = w0 + _dot(jnp.tanh(_dot(xw, w1)).astype(BF16), w2)
    nz = -z
    sp = jnp.maximum(nz, 0.0) + jnp.log(1.0 + jnp.exp(-jnp.abs(nz)))
    lw = -jnp.exp(-sp - 0.5)
    a = _sigmoid(a0 + _dot(_dot(xa, a1).astype(BF16), a2))
    g = _dot(_sigmoid(_dot(xg, g1)).astype(BF16), g2)
    kk = k * k_k
    ss = _head_sum(kk * kk, _head_ones())
    kk = kk / jnp.maximum(jnp.sqrt(ss), 1e-12)
    kmod = k * (1.0 + (a - 1.0) * k_a)
    for ref, val in zip(out_refs, (r, lw, kmod, v, kk, kk * a, g)):
        ref[...] = val.astype(ref.dtype)


def _rk_pre_prompt_kernel(h_ref, s0_ref, *refs, tm):
    p, out_refs, carry_ref = refs[:14], refs[14:21], refs[21]

    @pl.when(pl.program_id(1) == 0)
    def _():
        carry_ref[0:1, :] = s0_ref[...]

    h = h_ref[...]
    row = lax.broadcasted_iota(jnp.int32, h.shape, 0)
    prev = jnp.where(row == 0, carry_ref[0:1, :], pltpu.roll(h, 1, axis=0))
    carry_ref[0:1, :] = h[tm - 1:tm, :]
    _rk_pre_core(h, prev, p, out_refs)


def _rk_pre_sample_kernel(h_ref, s0_ref, *refs, ns, t_len):
    p, out_refs = refs[:14], refs[14:21]
    h = h_ref[...]
    rows = ns * t_len
    s0 = jnp.broadcast_to(s0_ref[...], (ns, t_len, D_MODEL)).reshape(rows, D_MODEL)
    t = lax.broadcasted_iota(jnp.int32, h.shape, 0) & (t_len - 1)
    prev = jnp.where(t == 0, s0, pltpu.roll(h, 1, axis=0))
    _rk_pre_core(h, prev, p, out_refs)


def _rk_param_specs(p):
    return [_const_spec(a.shape) for a in p]


def _rk_pre(hn, shift0, p, bsz, t_len, tm):
    m = bsz * t_len
    out_shape = [jax.ShapeDtypeStruct((m, D_MODEL), F32 if i == 1 else BF16) for i in range(7)]
    s0 = shift0.reshape(bsz, 1, D_MODEL)
    if t_len > tm:
        nt = t_len // tm
        row = pl.BlockSpec((tm, D_MODEL), lambda b, t: (b * nt + t, 0))
        return pl.pallas_call(
            functools.partial(_rk_pre_prompt_kernel, tm=tm),
            grid=(bsz, nt),
            in_specs=[row, pl.BlockSpec((None, 1, D_MODEL), lambda b, t: (b, 0, 0))] + _rk_param_specs(p),
            out_specs=[row] * 7,
            out_shape=out_shape,
            scratch_shapes=[pltpu.VMEM((SUBLANES, D_MODEL), F32)],
            compiler_params=_params(("arbitrary", "arbitrary")),
            name="rk_pre_prompt",
        )(hn, s0, *p)
    ns = tm // t_len
    row = pl.BlockSpec((tm, D_MODEL), lambda i: (i, 0))
    return pl.pallas_call(
        functools.partial(_rk_pre_sample_kernel, ns=ns, t_len=t_len),
        grid=(m // tm,),
        in_specs=[row, pl.BlockSpec((ns, 1, D_MODEL), lambda i: (i, 0, 0))] + _rk_param_specs(p),
        out_specs=[row] * 7,
        out_shape=out_shape,
        compiler_params=_params(("parallel",)),
        name="rk_pre_sample",
    )(hn, s0, *p)


def _wkv_masks(c):
    rows = 128
    sh = c.bit_length() - 1
    row = lax.broadcasted_iota(jnp.int32, (rows, rows), 0)
    col = lax.broadcasted_iota(jnp.int32, (rows, rows), 1)
    same = (row >> sh) == (col >> sh)
    strict = same & (row > col)
    incl = same & (row >= col)
    base = (row >> 1) == (col >> 1)
    merges = []
    s = 2
    while s < c:
        b = s.bit_length() - 1
        merges.append(((row >> (b + 1)) == (col >> (b + 1))) & (((row >> b) & 1) == 1) & (((col >> b) & 1) == 0))
        s *= 2
    eye = jnp.where(row == col, 1.0, 0.0).astype(F32)
    return strict, incl, base, merges, eye


def _cumsum_rows(x, c):
    t = lax.broadcasted_iota(jnp.int32, x.shape, 0)
    s = 1
    while s < c:
        x = x + jnp.where(t >= s, pltpu.roll(x, s, axis=0), 0.0)
        s *= 2
    return x


def _wkv_units(ins, states, c, masks):
    strict, incl, base, merges, eye = masks
    w = ins[0][0].shape[1]
    nh = 128 // c
    assert w == nh * HEAD_DIM
    lane_h = lax.broadcasted_iota(jnp.int32, (c, w), 1) >> 6
    state_lanes = [slice(j * PAIR, (j + 1) * PAIR) for j in range(w // PAIR)]

    def stack(x):
        return jnp.concatenate([jnp.where(lane_h == h, x, 0.0) for h in range(nh)], axis=0).astype(BF16)

    def each(f, *lists):
        return [f(*xs) for xs in zip(*lists)]

    def prep(r, lw, k, v, kk, b):
        cum = _cumsum_rows(lw, c)
        last = cum[c - 1:c, :]
        to_end = jnp.exp(last - cum)
        p_inv = jnp.exp(-cum)
        return dict(p_c=jnp.exp(last), kk=stack(kk * jnp.exp(cum - lw)), r=stack(r * jnp.exp(cum)),
                    k=stack(k * p_inv), b=stack(b * p_inv), v=stack(v), kd=stack(k * to_end),
                    bd=stack(b * to_end))

    o = [prep(*x) for x in ins]
    kb = [jnp.concatenate([x['k'], x['b']], axis=0) for x in o]
    a_kk = each(lambda x, w_: _dot_nt(x['kk'], w_), o, kb)
    a_r = each(lambda x, w_: _dot_nt(x['r'], w_), o, kb)
    a_kk_k, a_kk_b = [a[:, :128] for a in a_kk], [a[:, 128:] for a in a_kk]
    a_r_k, a_r_b = [a[:, :128] for a in a_r], [a[:, 128:] for a in a_r]

    lm = [jnp.where(strict, a, 0.0) for a in a_kk_b]
    inv = [eye - jnp.where(base, l, 0.0) for l in lm]
    for m in merges:
        inv_b = [t.astype(BF16) for t in inv]
        mid = each(lambda l, t: _dot(jnp.where(m, l, 0.0).astype(BF16), t).astype(BF16), lm, inv_b)
        inv = each(lambda t, tb, md: t - _dot(tb, md), inv, inv_b, mid)

    def state_dot(x_s, s_b):
        cols = [_dot_nt(x_s[:, sl], s) for sl, s in zip(state_lanes, s_b)]
        return cols[0] if len(cols) == 1 else jnp.concatenate(cols, axis=1)

    s_b = [[s.astype(BF16) for s in sl] for sl in states]
    kkh = each(lambda x, s: state_dot(x['kk'], s), o, s_b)
    rh = each(lambda x, s: state_dot(x['r'], s), o, s_b)
    rhs = each(lambda h, a, x: h + _dot(jnp.where(strict, a, 0.0).astype(BF16), x['v']), kkh, a_kk_k, o)
    u_b = each(lambda t, z: _dot(t.astype(BF16), z.astype(BF16)).astype(BF16), inv, rhs)
    y_s = each(lambda h, ak, ab, x, u: h + _dot(jnp.where(incl, ak, 0.0).astype(BF16), x['v'])
               - _dot(jnp.where(incl, ab, 0.0).astype(BF16), u), rh, a_r_k, a_r_b, o, u_b)

    def fold(ys):
        y = ys[0:c]
        for h in range(1, nh):
            y = y + ys[h * c:(h + 1) * c]
        return y

    def new_state(x, u, s_list):
        return [s * x['p_c'][:, sl] + _dot_tn(x['v'][:, sl], x['kd'][:, sl]) - _dot_tn(u[:, sl], x['bd'][:, sl])
                for sl, s in zip(state_lanes, s_list)]

    return [fold(ys) for ys in y_s], each(new_state, o, u_b, states)


def _rows_to_blockdiag(s):
    ext = jnp.concatenate([s, jnp.zeros_like(s)], axis=1)
    row = lax.broadcasted_iota(jnp.int32, ext.shape, 0)
    return jnp.where(row < HEAD_DIM, ext, pltpu.roll(ext, HEAD_DIM, axis=1))


def _blockdiag_to_rows(s_bd):
    row = lax.broadcasted_iota(jnp.int32, s_bd.shape, 0)
    return jnp.where(row < HEAD_DIM, s_bd, pltpu.roll(s_bd, HEAD_DIM, axis=1))[:, :HEAD_DIM]


def _wkv_prompt_kernel(r_ref, lw_ref, k_ref, v_ref, kk_ref, b_ref, s0_ref, y_ref, sout_ref, s_ref,
                       *, c, t_len, nb):
    ci = pl.program_id(1)
    units = [(i, q) for i in range(nb) for q in range(N_PAIRS)]

    @pl.when(ci == 0)
    def _():
        for i, q in units:
            s_ref[i, q] = _rows_to_blockdiag(s0_ref[i, q])

    masks = _wkv_masks(c)
    rows = lax.broadcasted_iota(jnp.int32, (c, PAIR), 0) + ci * c
    valid = rows < t_len
    lanes = [slice(q * PAIR, (q + 1) * PAIR) for q in range(N_PAIRS)]
    ins = [[jnp.where(valid, ref[i, :, lanes[q]].astype(F32), 0.0)
            for ref in (r_ref, lw_ref, k_ref, v_ref, kk_ref, b_ref)] for i, q in units]
    ys, s_new = _wkv_units(ins, [[s_ref[i, q]] for i, q in units], c, masks)
    for (i, q), y, s in zip(units, ys, s_new):
        y_ref[i, :, lanes[q]] = y
        s_ref[i, q] = s[0]

    @pl.when(ci == pl.num_programs(1) - 1)
    def _():
        for i, q in units:
            sout_ref[i, q] = _blockdiag_to_rows(s_ref[i, q])


def _wkv_sample_kernel(r_ref, lw_ref, k_ref, v_ref, kk_ref, b_ref, s0_ref, y_ref, sout_ref, *, c, units):
    masks = _wkv_masks(c)
    rows = [slice(i * c, (i + 1) * c) for i in range(units)]
    full = [ref[...].astype(F32) for ref in (r_ref, lw_ref, k_ref, v_ref, kk_ref, b_ref)]
    ins = [[a[sl, :] for a in full] for sl in rows]
    states = [[_rows_to_blockdiag(s0_ref[i, q]) for q in range(N_PAIRS)] for i in range(units)]
    ys, s_new = _wkv_units(ins, states, c, masks)
    for i in range(units):
        y_ref[rows[i], :] = ys[i]
        for q in range(N_PAIRS):
            sout_ref[i, q] = _blockdiag_to_rows(s_new[i][q])


def _wkv(ins, state, bsz, t_len):
    m = bsz * t_len
    y_shape = jax.ShapeDtypeStruct((m, D_MODEL), F32)
    s_rows = state.reshape(bsz, N_PAIRS, PAIR, HEAD_DIM)
    s_shape = jax.ShapeDtypeStruct(s_rows.shape, F32)
    if t_len > 64:
        c, nb = 64, 2
        nc = pl.cdiv(t_len, c)
        ins3 = [a.reshape(bsz, t_len, D_MODEL) for a in ins]
        blk = pl.BlockSpec((nb, c, D_MODEL), lambda b, i: (b, i, 0))
        sblk = pl.BlockSpec((nb, N_PAIRS, PAIR, HEAD_DIM), lambda b, i: (b, 0, 0, 0))
        y, s_out = pl.pallas_call(
            functools.partial(_wkv_prompt_kernel, c=c, t_len=t_len, nb=nb),
            grid=(bsz // nb, nc),
            in_specs=[blk] * 6 + [sblk],
            out_specs=[blk, sblk],
            out_shape=[jax.ShapeDtypeStruct((bsz, t_len, D_MODEL), F32), s_shape],
            scratch_shapes=[pltpu.VMEM((nb, N_PAIRS, PAIR, PAIR), F32)],
            compiler_params=_params(("arbitrary", "arbitrary")),
            name="wkv_prompt",
        )(*ins3, s_rows)
        return y.reshape(m, D_MODEL), s_out.reshape(state.shape)
    c, units = t_len, 4
    blk = pl.BlockSpec((units * c, D_MODEL), lambda i: (i, 0))
    sblk = pl.BlockSpec((units, N_PAIRS, PAIR, HEAD_DIM), lambda i: (i, 0, 0, 0))
    y, s_out = pl.pallas_call(
        functools.partial(_wkv_sample_kernel, c=c, units=units),
        grid=(bsz // units,),
        in_specs=[blk] * 6 + [sblk],
        out_specs=[blk, sblk],
        out_shape=[y_shape, s_shape],
        compiler_params=_params(("parallel",)),
        name="wkv_sample",
    )(*ins, s_rows)
    return y, s_out.reshape(state.shape)


def _rk_post_kernel(x_ref, y_ref, r_ref, k_ref, v_ref, g_ref, lnw_ref, lnb_ref, rk_ref, wo_ref, o_ref):
    ones = _head_ones()
    y = y_ref[...]
    inv_n = 1.0 / HEAD_DIM
    mean = _head_sum(y, ones) * inv_n
    yc = y - mean
    var = _head_sum(yc * yc, ones) * inv_n
    yn = yc * lax.rsqrt(var + GN_EPS) * lnw_ref[...] + lnb_ref[...]
    rk = r_ref[...].astype(F32) * k_ref[...].astype(F32) * rk_ref[...]
    bonus = _head_sum(rk, ones) * v_ref[...].astype(F32)
    o = ((yn + bonus) * g_ref[...].astype(F32)).astype(BF16)
    o_ref[...] = x_ref[...] + _dot(o, wo_ref[...])


def _rk_post(x, y, r, k, v, g, ln_w, ln_b, r_k, w_o, tm):
    m = x.shape[0]
    row = pl.BlockSpec((tm, D_MODEL), lambda i: (i, 0))
    vec = _const_spec((1, D_MODEL))
    return pl.pallas_call(
        _rk_post_kernel,
        grid=(m // tm,),
        in_specs=[row] * 6 + [vec, vec, vec, _const_spec((D_MODEL, D_MODEL))],
        out_specs=row,
        out_shape=jax.ShapeDtypeStruct((m, D_MODEL), F32),
        compiler_params=_params(("parallel",)),
        name="rk_post",
    )(x, y, r, k, v, g, ln_w.reshape(1, D_MODEL), ln_b.reshape(1, D_MODEL), r_k.reshape(1, D_MODEL), w_o)


def _conv_core(x, h, shifted, win_ref, cw_ref, wout_ref):
    z = _dot(h.astype(BF16), win_ref[...])
    gate_b = z[:, 0:D_MODEL]
    u = z[:, D_MODEL:2 * D_MODEL] * z[:, 2 * D_MODEL:3 * D_MODEL]
    cw = cw_ref[...]
    u1, u2 = shifted(u)
    conv = u2 * cw[0:1, :] + u1 * cw[1:2, :] + u * cw[2:3, :]
    out = x + _dot((gate_b * conv).astype(BF16), wout_ref[...])
    return out, u


def _conv_prompt_kernel(x_ref, h_ref, c0_ref, win_ref, cw_ref, wout_ref, o_ref, st_ref, carry_ref, *, tm):
    @pl.when(pl.program_id(1) == 0)
    def _():
        carry_ref[0:2, :] = c0_ref[...]

    def shifted(u):
        row = lax.broadcasted_iota(jnp.int32, u.shape, 0)
        c0 = carry_ref[0:1, :]
        c1 = carry_ref[1:2, :]
        u1 = jnp.where(row == 0, c1, pltpu.roll(u, 1, axis=0))
        u2 = jnp.where(row == 0, c0, jnp.where(row == 1, c1, pltpu.roll(u, 2, axis=0)))
        return u1, u2

    out, u = _conv_core(x_ref[...], h_ref[...], shifted, win_ref, cw_ref, wout_ref)
    o_ref[...] = out
    carry_ref[0:2, :] = u[tm - 2:tm, :]
    st_ref[...] = u[tm - SUBLANES:tm, :]


def _conv_sample_kernel(x_ref, h_ref, c0_ref, win_ref, cw_ref, wout_ref, o_ref, u_ref, *, ns, t_len):
    rows = ns * t_len
    c0 = jnp.broadcast_to(c0_ref[:, 0:1, :], (ns, t_len, D_MODEL)).reshape(rows, D_MODEL)
    c1 = jnp.broadcast_to(c0_ref[:, 1:2, :], (ns, t_len, D_MODEL)).reshape(rows, D_MODEL)

    def shifted(u):
        t = lax.broadcasted_iota(jnp.int32, u.shape, 0) & (t_len - 1)
        u1 = jnp.where(t == 0, c1, pltpu.roll(u, 1, axis=0))
        u2 = jnp.where(t == 0, c0, jnp.where(t == 1, c1, pltpu.roll(u, 2, axis=0)))
        return u1, u2

    out, u = _conv_core(x_ref[...], h_ref[...], shifted, win_ref, cw_ref, wout_ref)
    o_ref[...] = out
    u_ref[...] = u


def _conv_mix(x, hn, conv0, w_in, conv_w, w_out, bsz, t_len, tm):
    m = bsz * t_len
    consts = [_const_spec((D_MODEL, 3 * D_MODEL)), _const_spec((CONV_W, D_MODEL)), _const_spec((D_MODEL, D_MODEL))]
    if t_len > tm:
        nt = t_len // tm
        row = pl.BlockSpec((tm, D_MODEL), lambda b, t: (b * nt + t, 0))
        out, tail = pl.pallas_call(
            functools.partial(_conv_prompt_kernel, tm=tm),
            grid=(bsz, nt),
            in_specs=[row, row, pl.BlockSpec((None, CONV_W - 1, D_MODEL), lambda b, t: (b, 0, 0))] + consts,
            out_specs=[row, pl.BlockSpec((None, SUBLANES, D_MODEL), lambda b, t: (b, 0, 0))],
            out_shape=[jax.ShapeDtypeStruct((m, D_MODEL), F32),
                       jax.ShapeDtypeStruct((bsz, SUBLANES, D_MODEL), F32)],
            scratch_shapes=[pltpu.VMEM((SUBLANES, D_MODEL), F32)],
            compiler_params=_params(("arbitrary", "arbitrary")),
            name="conv_prompt",
        )(x, hn, conv0, w_in, conv_w, w_out)
        return out, tail[:, SUBLANES - (CONV_W - 1):, :]
    ns = tm // t_len
    row = pl.BlockSpec((tm, D_MODEL), lambda i: (i, 0))
    out, u = pl.pallas_call(
        functools.partial(_conv_sample_kernel, ns=ns, t_len=t_len),
        grid=(m // tm,),
        in_specs=[row, row, pl.BlockSpec((ns, CONV_W - 1, D_MODEL), lambda i: (i, 0, 0))] + consts,
        out_specs=[row, row],
        out_shape=[jax.ShapeDtypeStruct((m, D_MODEL), F32)] * 2,
        compiler_params=_params(("parallel",)),
        name="conv_sample",
    )(x, hn, conv0, w_in, conv_w, w_out)
    return out, u.reshape(bsz, t_len, D_MODEL)[:, t_len - (CONV_W - 1):, :]


def _trunk(x, wkv_in, shift_in, conv_in, p, bsz, t_len, tm):
    x, hn = _ffn(x, p['ffn_norm'][0, 0], p['ffn_w_gu'][0, 0], p['ffn_w_down'][0, 0], p['mix_norm'][0], tm)
    new_shift = hn.reshape(bsz, t_len, D_MODEL)[:, -1]
    r, lw, k, v, kk, b, g = _rk_pre(hn, shift_in[0], p['rk_pre'], bsz, t_len, tm)
    y, new_wkv = _wkv([r, lw, k, v, kk, b], wkv_in[0], bsz, t_len)
    x = _rk_post(x, y, r, k, v, g, p['rk_ln_w'][0], p['rk_ln_b'][0], p['rk_r_k'][0], p['rk_w_o'][0], tm)
    (x,) = _ffn(x, p['ffn_norm'][0, 1], p['ffn_w_gu'][0, 1], p['ffn_w_down'][0, 1], p['ffn_norm'][1, 0], tm,
                emit_n=False)
    x, hn = _ffn(x, p['ffn_norm'][1, 0], p['ffn_w_gu'][1, 0], p['ffn_w_down'][1, 0], p['mix_norm'][1], tm,
                 n_dtype=BF16)
    x, new_conv = _conv_mix(x, hn, conv_in[0], p['sc_w_in'][0], p['sc_conv_w'][0], p['sc_w_out'][0],
                            bsz, t_len, tm)
    (y_out,) = _ffn(x, p['ffn_norm'][1, 1], p['ffn_w_gu'][1, 1], p['ffn_w_down'][1, 1], p['final_norm'], tm,
                    emit_x=False)
    return y_out, new_wkv[None], new_shift[None], new_conv[None]


def kernel(x_prompt, x_sample, state_wkv, state_shift, state_conv, meta, ffn_norm, ffn_w_gu, ffn_w_down,
           mix_norm, final_norm, rk_mu, rk_w_rkv, rk_w0, rk_w1, rk_w2, rk_a0, rk_a1, rk_a2, rk_g1, rk_g2,
           rk_k_k, rk_k_a, rk_r_k, rk_ln_w, rk_ln_b, rk_w_o, sc_w_in, sc_conv_w, sc_w_out):
    assert DEPTH == 2 and rk_mu.shape[0] == 1 and sc_w_in.shape[0] == 1
    bf = lambda a: a.astype(BF16)
    vec = lambda a: a.reshape(1, D_MODEL)
    p = {
        'ffn_norm': ffn_norm, 'ffn_w_gu': bf(ffn_w_gu), 'ffn_w_down': bf(ffn_w_down), 'mix_norm': mix_norm,
        'final_norm': final_norm,
        'rk_pre': [rk_mu[0], bf(rk_w_rkv[0, 0]), bf(rk_w_rkv[0, 1]), bf(rk_w_rkv[0, 2]), vec(rk_w0[0]),
                   bf(rk_w1[0]), bf(rk_w2[0]), vec(rk_a0[0]), bf(rk_a1[0]), bf(rk_a2[0]), bf(rk_g1[0]),
                   bf(rk_g2[0]), vec(rk_k_k[0]), vec(rk_k_a[0])],
        'rk_r_k': rk_r_k.reshape(1, D_MODEL), 'rk_ln_w': rk_ln_w, 'rk_ln_b': rk_ln_b, 'rk_w_o': bf(rk_w_o),
        'sc_w_in': bf(sc_w_in), 'sc_conv_w': sc_conv_w, 'sc_w_out': bf(sc_w_out),
    }
    bp, seq, _ = x_prompt.shape
    t_p = seq + N_META
    xp = jnp.concatenate([jnp.broadcast_to(meta.astype(x_prompt.dtype)[None], (bp, N_META, D_MODEL)), x_prompt],
                         axis=1).reshape(bp * t_p, D_MODEL)
    wkv0 = jnp.zeros((1, bp, N_HEADS, HEAD_DIM, HEAD_DIM), state_wkv.dtype)
    shift0 = jnp.zeros((1, bp, D_MODEL), state_shift.dtype)
    conv0 = jnp.zeros((1, bp, CONV_W - 1, D_MODEL), state_conv.dtype)
    yp, wkv_p, shift_p, conv_p = _trunk(xp, wkv0, shift0, conv0, p, bp, t_p, 344)
    y_prompt = yp.reshape(bp, t_p, D_MODEL)[:, N_META:]
    bs, t_s, _ = x_sample.shape
    ys, wkv_s, shift_s, conv_s = _trunk(x_sample.reshape(bs * t_s, D_MODEL), state_wkv, state_shift, state_conv,
                                        p, bs, t_s, 256)
    return (y_prompt, ys.reshape(bs, t_s, D_MODEL), wkv_p, shift_p, conv_p, wkv_s, shift_s, conv_s)
```

```python
import functools

import jax
import jax.numpy as jnp
from jax import lax
from jax.experimental import pallas as pl
from jax.experimental.pallas import tpu as pltpu

D_MODEL = 1024
HEAD_DIM = 64
N_HEADS = D_MODEL // HEAD_DIM
D_FF = 2816
N_META = 16
DEPTH = 2
CONV_W = 3
RMS_EPS = 1e-6
GN_EPS = 64e-5

LANES = 128
SUBLANES = 8
MXU_DIM = 256
PAIR = 2 * HEAD_DIM
N_PAIRS = D_MODEL // PAIR
VMEM_LIMIT = 56 * 1024 * 1024

FFN_CHUNKS = ((0, 6 * MXU_DIM), (6 * MXU_DIM, D_FF))

F32 = jnp.float32
BF16 = jnp.bfloat16


def _dot(a, b):
    return jnp.dot(a, b, preferred_element_type=F32)


def _dot_nt(a, b):
    return lax.dot_general(a, b, (((1,), (1,)), ((), ())), preferred_element_type=F32)


def _dot_tn(a, b):
    return lax.dot_general(a, b, (((0,), (0,)), ((), ())), preferred_element_type=F32)


def _rms(x, g):
    ms = jnp.mean(x * x, axis=-1, keepdims=True)
    return x * lax.rsqrt(ms + RMS_EPS) * g


def _sigmoid(x):
    return 1.0 / (1.0 + jnp.exp(-x))


def _head_ones():
    r = lax.broadcasted_iota(jnp.int32, (MXU_DIM, MXU_DIM), 0) >> 6
    c = lax.broadcasted_iota(jnp.int32, (MXU_DIM, MXU_DIM), 1) >> 6
    return jnp.where(r == c, 1.0, 0.0).astype(BF16)


def _head_sum(x, ones):
    hi = x.astype(BF16)
    lo = (x - hi.astype(F32)).astype(BF16)
    cols = []
    for j in range(D_MODEL // MXU_DIM):
        sl = slice(j * MXU_DIM, (j + 1) * MXU_DIM)
        cols.append(_dot(hi[:, sl], ones) + _dot(lo[:, sl], ones))
    return jnp.concatenate(cols, axis=1)


def _const_spec(shape):
    nd = len(shape)
    return pl.BlockSpec(shape, lambda *_: (0,) * nd, pipeline_mode=pl.Buffered(1))


def _params(sem):
    return pltpu.CompilerParams(dimension_semantics=sem, vmem_limit_bytes=VMEM_LIMIT)


def _ffn_kernel(x_ref, g_ref, wgu_ref, wd_ref, g2_ref, *out_refs, emit_x, emit_n):
    x = x_ref[...]
    xn = _rms(x, g_ref[...]).astype(BF16)
    acc = None
    for lo, hi in FFN_CHUNKS:
        gate = _dot(xn, wgu_ref[:, lo:hi])
        up = _dot(xn, wgu_ref[:, D_FF + lo:D_FF + hi])
        act = (gate * _sigmoid(gate) * up).astype(BF16)
        part = _dot(act, wd_ref[lo:hi, :])
        acc = part if acc is None else acc + part
    out = x + 0.5 * acc
    i = 0
    if emit_x:
        out_refs[i][...] = out
        i += 1
    if emit_n:
        out_refs[i][...] = _rms(out, g2_ref[...]).astype(out_refs[i].dtype)


def _ffn(x, g, wgu, wd, layer, pos, g2, tm, emit_x=True, emit_n=True, n_dtype=F32):
    m = x.shape[0]
    row = pl.BlockSpec((tm, D_MODEL), lambda i: (i, 0))
    n_out = int(emit_x) + int(emit_n)
    dtypes = [F32] * int(emit_x) + [n_dtype] * int(emit_n)

    def weight_spec(rows, cols):
        return pl.BlockSpec((None, None, rows, cols), lambda i: (layer, pos, 0, 0), pipeline_mode=pl.Buffered(1))

    outs = pl.pallas_call(
        functools.partial(_ffn_kernel, emit_x=emit_x, emit_n=emit_n),
        grid=(m // tm,),
        in_specs=[row, _const_spec((1, D_MODEL)), weight_spec(D_MODEL, 2 * D_FF),
                  weight_spec(D_FF, D_MODEL), _const_spec((1, D_MODEL))],
        out_specs=[row] * n_out,
        out_shape=[jax.ShapeDtypeStruct((m, D_MODEL), dt) for dt in dtypes],
        compiler_params=_params(("parallel",)),
        name="ffn",
    )(x, g.reshape(1, D_MODEL), wgu, wd, g2.reshape(1, D_MODEL))
    return tuple(outs)


def _rk_pre_core(h, prev, p, out_refs):
    (mu, wr, wk, wv, w0, w1, w2, a0, a1, a2, g1, g2, k_k, k_a) = [r[...] for r in p]
    xx = prev - h
    mix = [(h + xx * mu[c:c + 1, :]).astype(BF16) for c in range(6)]
    xr, xw, xk, xv, xa, xg = mix
    r = _dot(xr, wr)
    k = _dot(xk, wk)
    v = _dot(xv, wv)
    z = w0 + _dot(jnp.tanh(_dot(xw, w1)).astype(BF16), w2)
    nz = -z
    sp = jnp.maximum(nz, 0.0) + jnp.log(1.0 + jnp.exp(-jnp.abs(nz)))
    lw = -jnp.exp(-sp - 0.5)
    a = _sigmoid(a0 + _dot(_dot(xa, a1).astype(BF16), a2))
    g = _dot(_sigmoid(_dot(xg, g1)).astype(BF16), g2)
    kk = k * k_k
    ss = _head_sum(kk * kk, _head_ones())
    kk = kk / jnp.maximum(jnp.sqrt(ss), 1e-12)
    kmod = k * (1.0 + (a - 1.0) * k_a)
    for ref, val in zip(out_refs, (r, lw, kmod, v, kk, kk * a, g)):
        ref[...] = val.astype(ref.dtype)


def _rk_pre_prompt_kernel(h_ref, s0_ref, *refs, tm):
    p, out_refs, carry_ref = refs[:14], refs[14:21], refs[21]

    @pl.when(pl.program_id(1) == 0)
    def _():
        carry_ref[0:1, :] = s0_ref[...]

    h = h_ref[...]
    row = lax.broadcasted_iota(jnp.int32, h.shape, 0)
    prev = jnp.where(row == 0, carry_ref[0:1, :], pltpu.roll(h, 1, axis=0))
    carry_ref[0:1, :] = h[tm - 1:tm, :]
    _rk_pre_core(h, prev, p, out_refs)


def _rk_pre_sample_kernel(h_ref, s0_ref, *refs, ns, t_len):
    p, out_refs = refs[:14], refs[14:21]
    h = h_ref[...]
    rows = ns * t_len
    s0 = jnp.broadcast_to(s0_ref[...], (ns, t_len, D_MODEL)).reshape(rows, D_MODEL)
    t = lax.broadcasted_iota(jnp.int32, h.shape, 0) & (t_len - 1)
    prev = jnp.where(t == 0, s0, pltpu.roll(h, 1, axis=0))
    _rk_pre_core(h, prev, p, out_refs)


def _rk_param_specs(p):
    return [_const_spec(a.shape) for a in p]


def _rk_pre(hn, shift0, p, bsz, t_len, tm):
    m = bsz * t_len
    out_shape = [jax.ShapeDtypeStruct((m, D_MODEL), F32 if i == 1 else BF16) for i in range(7)]
    s0 = shift0.reshape(bsz, 1, D_MODEL)
    if t_len > tm:
        nt = t_len // tm
        row = pl.BlockSpec((tm, D_MODEL), lambda b, t: (b * nt + t, 0))
        return pl.pallas_call(
            functools.partial(_rk_pre_prompt_kernel, tm=tm),
            grid=(bsz, nt),
            in_specs=[row, pl.BlockSpec((None, 1, D_MODEL), lambda b, t: (b, 0, 0))] + _rk_param_specs(p),
            out_specs=[row] * 7,
            out_shape=out_shape,
            scratch_shapes=[pltpu.VMEM((SUBLANES, D_MODEL), F32)],
            compiler_params=_params(("arbitrary", "arbitrary")),
            name="rk_pre_prompt",
        )(hn, s0, *p)
    ns = tm // t_len
    row = pl.BlockSpec((tm, D_MODEL), lambda i: (i, 0))
    return pl.pallas_call(
        functools.partial(_rk_pre_sample_kernel, ns=ns, t_len=t_len),
        grid=(m // tm,),
        in_specs=[row, pl.BlockSpec((ns, 1, D_MODEL), lambda i: (i, 0, 0))] + _rk_param_specs(p),
        out_specs=[row] * 7,
        out_shape=out_shape,
        compiler_params=_params(("parallel",)),
        name="rk_pre_sample",
    )(hn, s0, *p)


def _wkv_masks(c):
    rows = 128
    sh = c.bit_length() - 1
    row = lax.broadcasted_iota(jnp.int32, (rows, rows), 0)
    col = lax.broadcasted_iota(jnp.int32, (rows, rows), 1)
    same = (row >> sh) == (col >> sh)
    strict = same & (row > col)
    incl = same & (row >= col)
    base = (row >> 1) == (col >> 1)
    merges = []
    s = 2
    while s < c:
        b = s.bit_length() - 1
        merges.append(((row >> (b + 1)) == (col >> (b + 1))) & (((row >> b) & 1) == 1) & (((col >> b) & 1) == 0))
        s *= 2
    eye = jnp.where(row == col, 1.0, 0.0).astype(F32)
    return strict, incl, base, merges, eye


def _cumsum_rows(x, c):
    t = lax.broadcasted_iota(jnp.int32, x.shape, 0)
    s = 1
    while s < c:
        x = x + jnp.where(t >= s, pltpu.roll(x, s, axis=0), 0.0)
        s *= 2
    return x


def _wkv_units(ins, states, c, masks):
    strict, incl, base, merges, eye = masks
    w = ins[0][0].shape[1]
    nh = 128 // c
    assert w == nh * HEAD_DIM
    lane_h = lax.broadcasted_iota(jnp.int32, (c, w), 1) >> 6
    state_lanes = [slice(j * PAIR, (j + 1) * PAIR) for j in range(w // PAIR)]

    def stack(x):
        return jnp.concatenate([jnp.where(lane_h == h, x, 0.0) for h in range(nh)], axis=0).astype(BF16)

    def each(f, *lists):
        return [f(*xs) for xs in zip(*lists)]

    def prep(r, lw, k, v, kk, b):
        cum = _cumsum_rows(lw, c)
        last = cum[c - 1:c, :]
        to_end = jnp.exp(last - cum)
        p_inv = jnp.exp(-cum)
        return dict(p_c=jnp.exp(last), kk=stack(kk * jnp.exp(cum - lw)), r=stack(r * jnp.exp(cum)),
                    k=stack(k * p_inv), b=stack(b * p_inv), v=stack(v), kd=stack(k * to_end),
                    bd=stack(b * to_end))

    o = [prep(*x) for x in ins]
    kb = [jnp.concatenate([x['k'], x['b']], axis=0) for x in o]
    a_kk = each(lambda x, w_: _dot_nt(x['kk'], w_), o, kb)
    a_r = each(lambda x, w_: _dot_nt(x['r'], w_), o, kb)
    a_kk_k, a_kk_b = [a[:, :128] for a in a_kk], [a[:, 128:] for a in a_kk]
    a_r_k, a_r_b = [a[:, :128] for a in a_r], [a[:, 128:] for a in a_r]

    lm = [jnp.where(strict, a, 0.0) for a in a_kk_b]
    inv = [eye - jnp.where(base, l, 0.0) for l in lm]
    for m in merges:
        inv_b = [t.astype(BF16) for t in inv]
        mid = each(lambda l, t: _dot(jnp.where(m, l, 0.0).astype(BF16), t).astype(BF16), lm, inv_b)
        inv = each(lambda t, tb, md: t - _dot(tb, md), inv, inv_b, mid)

    def state_dot(x_s, s_b):
        cols = [_dot_nt(x_s[:, sl], s) for sl, s in zip(state_lanes, s_b)]
        return cols[0] if len(cols) == 1 else jnp.concatenate(cols, axis=1)

    s_b = [[s.astype(BF16) for s in sl] for sl in states]
    kkh = each(lambda x, s: state_dot(x['kk'], s), o, s_b)
    rh = each(lambda x, s: state_dot(x['r'], s), o, s_b)
    rhs = each(lambda h, a, x: h + _dot(jnp.where(strict, a, 0.0).astype(BF16), x['v']), kkh, a_kk_k, o)
    u_b = each(lambda t, z: _dot(t.astype(BF16), z.astype(BF16)).astype(BF16), inv, rhs)
    y_s = each(lambda h, ak, ab, x, u: h + _dot(jnp.where(incl, ak, 0.0).astype(BF16), x['v'])
               - _dot(jnp.where(incl, ab, 0.0).astype(BF16), u), rh, a_r_k, a_r_b, o, u_b)

    def fold(ys):
        y = ys[0:c]
        for h in range(1, nh):
            y = y + ys[h * c:(h + 1) * c]
        return y

    def new_state(x, u, s_list):
        return [s * x['p_c'][:, sl] + _dot_tn(x['v'][:, sl], x['kd'][:, sl]) - _dot_tn(u[:, sl], x['bd'][:, sl])
                for sl, s in zip(state_lanes, s_list)]

    return [fold(ys) for ys in y_s], each(new_state, o, u_b, states)


def _rows_to_blockdiag(s):
    ext = jnp.concatenate([s, jnp.zeros_like(s)], axis=1)
    row = lax.broadcasted_iota(jnp.int32, ext.shape, 0)
    return jnp.where(row < HEAD_DIM, ext, pltpu.roll(ext, HEAD_DIM, axis=1))


def _blockdiag_to_rows(s_bd):
    row = lax.broadcasted_iota(jnp.int32, s_bd.shape, 0)
    return jnp.where(row < HEAD_DIM, s_bd, pltpu.roll(s_bd, HEAD_DIM, axis=1))[:, :HEAD_DIM]


def _wkv_prompt_kernel(r_ref, lw_ref, k_ref, v_ref, kk_ref, b_ref, s0_ref, y_ref, sout_ref, s_ref,
                       *, c, t_len, nb):
    ci = pl.program_id(1)
    units = [(i, q) for i in range(nb) for q in range(N_PAIRS)]

    @pl.when(ci == 0)
    def _():
        for i, q in units:
            s_ref[i, q] = _rows_to_blockdiag(s0_ref[i, q])

    masks = _wkv_masks(c)
    rows = lax.broadcasted_iota(jnp.int32, (c, PAIR), 0) + ci * c
    valid = rows < t_len
    lanes = [slice(q * PAIR, (q + 1) * PAIR) for q in range(N_PAIRS)]
    ins = [[jnp.where(valid, ref[i, :, lanes[q]].astype(F32), 0.0)
            for ref in (r_ref, lw_ref, k_ref, v_ref, kk_ref, b_ref)] for i, q in units]
    ys, s_new = _wkv_units(ins, [[s_ref[i, q]] for i, q in units], c, masks)
    for (i, q), y, s in zip(units, ys, s_new):
        y_ref[i, :, lanes[q]] = y
        s_ref[i, q] = s[0]

    @pl.when(ci == pl.num_programs(1) - 1)
    def _():
        for i, q in units:
            sout_ref[i, q] = _blockdiag_to_rows(s_ref[i, q])


def _wkv_sample_kernel(r_ref, lw_ref, k_ref, v_ref, kk_ref, b_ref, s0_ref, y_ref, sout_ref, *, c, nseq):
    masks = _wkv_masks(c)
    gw = (128 // c) * HEAD_DIM
    gp = gw // PAIR
    units = [(i, j) for i in range(nseq) for j in range(D_MODEL // gw)]
    full = [ref[...].astype(F32) for ref in (r_ref, lw_ref, k_ref, v_ref, kk_ref, b_ref)]
    ins = [[a[i * c:(i + 1) * c, j * gw:(j + 1) * gw] for a in full] for i, j in units]
    states = [[_rows_to_blockdiag(s0_ref[i, j * gp + q]) for q in range(gp)] for i, j in units]
    ys, s_new = _wkv_units(ins, states, c, masks)
    for (i, j), y, s in zip(units, ys, s_new):
        y_ref[i * c:(i + 1) * c, j * gw:(j + 1) * gw] = y
        for q in range(gp):
            sout_ref[i, j * gp + q] = _blockdiag_to_rows(s[q])


def _wkv(ins, state, bsz, t_len):
    m = bsz * t_len
    y_shape = jax.ShapeDtypeStruct((m, D_MODEL), F32)
    s_rows = state.reshape(bsz, N_PAIRS, PAIR, HEAD_DIM)
    s_shape = jax.ShapeDtypeStruct(s_rows.shape, F32)
    if t_len > 64:
        c, nb = 64, 2
        nc = pl.cdiv(t_len, c)
        ins3 = [a.reshape(bsz, t_len, D_MODEL) for a in ins]
        blk = pl.BlockSpec((nb, c, D_MODEL), lambda b, i: (b, i, 0))
        sblk = pl.BlockSpec((nb, N_PAIRS, PAIR, HEAD_DIM), lambda b, i: (b, 0, 0, 0))
        y, s_out = pl.pallas_call(
            functools.partial(_wkv_prompt_kernel, c=c, t_len=t_len, nb=nb),
            grid=(bsz // nb, nc),
            in_specs=[blk] * 6 + [sblk],
            out_specs=[blk, sblk],
            out_shape=[jax.ShapeDtypeStruct((bsz, t_len, D_MODEL), F32), s_shape],
            scratch_shapes=[pltpu.VMEM((nb, N_PAIRS, PAIR, PAIR), F32)],
            compiler_params=_params(("arbitrary", "arbitrary")),
            name="wkv_prompt",
        )(*ins3, s_rows)
        return y.reshape(m, D_MODEL), s_out.reshape(state.shape)
    c, nseq = t_len, min(bsz, 4)
    blk = pl.BlockSpec((nseq * c, D_MODEL), lambda i: (i, 0))
    sblk = pl.BlockSpec((nseq, N_PAIRS, PAIR, HEAD_DIM), lambda i: (i, 0, 0, 0))
    y, s_out = pl.pallas_call(
        functools.partial(_wkv_sample_kernel, c=c, nseq=nseq),
        grid=(bsz // nseq,),
        in_specs=[blk] * 6 + [sblk],
        out_specs=[blk, sblk],
        out_shape=[y_shape, s_shape],
        compiler_params=_params(("parallel",)),
        name="wkv_sample",
    )(*ins, s_rows)
    return y, s_out.reshape(state.shape)


def _rk_post_kernel(x_ref, y_ref, r_ref, k_ref, v_ref, g_ref, lnw_ref, lnb_ref, rk_ref, wo_ref, o_ref):
    ones = _head_ones()
    y = y_ref[...]
    inv_n = 1.0 / HEAD_DIM
    mean = _head_sum(y, ones) * inv_n
    yc = y - mean
    var = _head_sum(yc * yc, ones) * inv_n
    yn = yc * lax.rsqrt(var + GN_EPS) * lnw_ref[...] + lnb_ref[...]
    rk = r_ref[...].astype(F32) * k_ref[...].astype(F32) * rk_ref[...]
    bonus = _head_sum(rk, ones) * v_ref[...].astype(F32)
    o = ((yn + bonus) * g_ref[...].astype(F32)).astype(BF16)
    o_ref[...] = x_ref[...] + _dot(o, wo_ref[...])


def _rk_post(x, y, r, k, v, g, ln_w, ln_b, r_k, w_o, tm):
    m = x.shape[0]
    row = pl.BlockSpec((tm, D_MODEL), lambda i: (i, 0))
    vec = _const_spec((1, D_MODEL))
    return pl.pallas_call(
        _rk_post_kernel,
        grid=(m // tm,),
        in_specs=[row] * 6 + [vec, vec, vec, _const_spec((D_MODEL, D_MODEL))],
        out_specs=row,
        out_shape=jax.ShapeDtypeStruct((m, D_MODEL), F32),
        compiler_params=_params(("parallel",)),
        name="rk_post",
    )(x, y, r, k, v, g, ln_w.reshape(1, D_MODEL), ln_b.reshape(1, D_MODEL), r_k.reshape(1, D_MODEL), w_o)


def _conv_core(x, h, shifted, win_ref, cw_ref, wout_ref):
    z = _dot(h.astype(BF16), win_ref[...])
    gate_b = z[:, 0:D_MODEL]
    u = z[:, D_MODEL:2 * D_MODEL] * z[:, 2 * D_MODEL:3 * D_MODEL]
    cw = cw_ref[...]
    u1, u2 = shifted(u)
    conv = u2 * cw[0:1, :] + u1 * cw[1:2, :] + u * cw[2:3, :]
    out = x + _dot((gate_b * conv).astype(BF16), wout_ref[...])
    return out, u


def _conv_prompt_kernel(x_ref, h_ref, c0_ref, win_ref, cw_ref, wout_ref, o_ref, st_ref, carry_ref, *, tm):
    @pl.when(pl.program_id(1) == 0)
    def _():
        carry_ref[0:2, :] = c0_ref[...]

    def shifted(u):
        row = lax.broadcasted_iota(jnp.int32, u.shape, 0)
        c0 = carry_ref[0:1, :]
        c1 = carry_ref[1:2, :]
        u1 = jnp.where(row == 0, c1, pltpu.roll(u, 1, axis=0))
        u2 = jnp.where(row == 0, c0, jnp.where(row == 1, c1, pltpu.roll(u, 2, axis=0)))
        return u1, u2

    out, u = _conv_core(x_ref[...], h_ref[...], shifted, win_ref, cw_ref, wout_ref)
    o_ref[...] = out
    carry_ref[0:2, :] = u[tm - 2:tm, :]
    st_ref[...] = u[tm - SUBLANES:tm, :]


def _conv_sample_kernel(x_ref, h_ref, c0_ref, win_ref, cw_ref, wout_ref, o_ref, u_ref, *, ns, t_len):
    rows = ns * t_len
    c0 = jnp.broadcast_to(c0_ref[:, 0:1, :], (ns, t_len, D_MODEL)).reshape(rows, D_MODEL)
    c1 = jnp.broadcast_to(c0_ref[:, 1:2, :], (ns, t_len, D_MODEL)).reshape(rows, D_MODEL)

    def shifted(u):
        t = lax.broadcasted_iota(jnp.int32, u.shape, 0) & (t_len - 1)
        u1 = jnp.where(t == 0, c1, pltpu.roll(u, 1, axis=0))
        u2 = jnp.where(t == 0, c0, jnp.where(t == 1, c1, pltpu.roll(u, 2, axis=0)))
        return u1, u2

    out, u = _conv_core(x_ref[...], h_ref[...], shifted, win_ref, cw_ref, wout_ref)
    o_ref[...] = out
    u_ref[...] = u


def _conv_mix(x, hn, conv0, w_in, conv_w, w_out, bsz, t_len, tm):
    m = bsz * t_len
    consts = [_const_spec((D_MODEL, 3 * D_MODEL)), _const_spec((CONV_W, D_MODEL)), _const_spec((D_MODEL, D_MODEL))]
    if t_len > tm:
        nt = t_len // tm
        row = pl.BlockSpec((tm, D_MODEL), lambda b, t: (b * nt + t, 0))
        out, tail = pl.pallas_call(
            functools.partial(_conv_prompt_kernel, tm=tm),
            grid=(bsz, nt),
            in_specs=[row, row, pl.BlockSpec((None, CONV_W - 1, D_MODEL), lambda b, t: (b, 0, 0))] + consts,
            out_specs=[row, pl.BlockSpec((None, SUBLANES, D_MODEL), lambda b, t: (b, 0, 0))],
            out_shape=[jax.ShapeDtypeStruct((m, D_MODEL), F32),
                       jax.ShapeDtypeStruct((bsz, SUBLANES, D_MODEL), F32)],
            scratch_shapes=[pltpu.VMEM((SUBLANES, D_MODEL), F32)],
            compiler_params=_params(("arbitrary", "arbitrary")),
            name="conv_prompt",
        )(x, hn, conv0, w_in, conv_w, w_out)
        return out, tail[:, SUBLANES - (CONV_W - 1):, :]
    ns = tm // t_len
    row = pl.BlockSpec((tm, D_MODEL), lambda i: (i, 0))
    out, u = pl.pallas_call(
        functools.partial(_conv_sample_kernel, ns=ns, t_len=t_len),
        grid=(m // tm,),
        in_specs=[row, row, pl.BlockSpec((ns, CONV_W - 1, D_MODEL), lambda i: (i, 0, 0))] + consts,
        out_specs=[row, row],
        out_shape=[jax.ShapeDtypeStruct((m, D_MODEL), F32)] * 2,
        compiler_params=_params(("parallel",)),
        name="conv_sample",
    )(x, hn, conv0, w_in, conv_w, w_out)
    return out, u.reshape(bsz, t_len, D_MODEL)[:, t_len - (CONV_W - 1):, :]


def _trunk(x, wkv_in, shift_in, conv_in, p, bsz, t_len, tm, tm_rk):
    ffn = functools.partial(_ffn, wgu=p['ffn_w_gu'], wd=p['ffn_w_down'], tm=tm)
    x, hn = ffn(x, p['ffn_norm'][0, 0], layer=0, pos=0, g2=p['mix_norm'][0])
    new_shift = hn.reshape(bsz, t_len, D_MODEL)[:, -1]
    r, lw, k, v, kk, b, g = _rk_pre(hn, shift_in[0], p['rk_pre'], bsz, t_len, tm_rk)
    y, new_wkv = _wkv([r, lw, k, v, kk, b], wkv_in[0], bsz, t_len)
    x = _rk_post(x, y, r, k, v, g, p['rk_ln_w'][0], p['rk_ln_b'][0], p['rk_r_k'][0], p['rk_w_o'][0], tm)
    (x,) = ffn(x, p['ffn_norm'][0, 1], layer=0, pos=1, g2=p['ffn_norm'][1, 0], emit_n=False)
    x, hn = ffn(x, p['ffn_norm'][1, 0], layer=1, pos=0, g2=p['mix_norm'][1], n_dtype=BF16)
    x, new_conv = _conv_mix(x, hn, conv_in[0], p['sc_w_in'][0], p['sc_conv_w'][0], p['sc_w_out'][0],
                            bsz, t_len, tm)
    (y_out,) = ffn(x, p['ffn_norm'][1, 1], layer=1, pos=1, g2=p['final_norm'], emit_x=False)
    return y_out, new_wkv[None], new_shift[None], new_conv[None]


def kernel(x_prompt, x_sample, state_wkv, state_shift, state_conv, meta, ffn_norm, ffn_w_gu, ffn_w_down,
           mix_norm, final_norm, rk_mu, rk_w_rkv, rk_w0, rk_w1, rk_w2, rk_a0, rk_a1, rk_a2, rk_g1, rk_g2,
           rk_k_k, rk_k_a, rk_r_k, rk_ln_w, rk_ln_b, rk_w_o, sc_w_in, sc_conv_w, sc_w_out):
    assert DEPTH == 2 and rk_mu.shape[0] == 1 and sc_w_in.shape[0] == 1
    bf = lambda a: a.astype(BF16)
    vec = lambda a: a.reshape(1, D_MODEL)
    p = {
        'ffn_norm': ffn_norm, 'ffn_w_gu': bf(ffn_w_gu), 'ffn_w_down': bf(ffn_w_down), 'mix_norm': mix_norm,
        'final_norm': final_norm,
        'rk_pre': [rk_mu[0], bf(rk_w_rkv[0, 0]), bf(rk_w_rkv[0, 1]), bf(rk_w_rkv[0, 2]), vec(rk_w0[0]),
                   bf(rk_w1[0]), bf(rk_w2[0]), vec(rk_a0[0]), bf(rk_a1[0]), bf(rk_a2[0]), bf(rk_g1[0]),
                   bf(rk_g2[0]), vec(rk_k_k[0]), vec(rk_k_a[0])],
        'rk_r_k': rk_r_k.reshape(1, D_MODEL), 'rk_ln_w': rk_ln_w, 'rk_ln_b': rk_ln_b, 'rk_w_o': bf(rk_w_o),
        'sc_w_in': bf(sc_w_in), 'sc_conv_w': sc_conv_w, 'sc_w_out': bf(sc_w_out),
    }
    bs, t_s, _ = x_sample.shape
    ys, wkv_s, shift_s, conv_s = _trunk(x_sample.reshape(bs * t_s, D_MODEL), state_wkv, state_shift, state_conv,
                                        p, bs, t_s, 256, 256)
    bp, seq, _ = x_prompt.shape
    zeros = (jnp.zeros((1, 1, N_HEADS, HEAD_DIM, HEAD_DIM), state_wkv.dtype),
             jnp.zeros((1, 1, D_MODEL), state_shift.dtype),
             jnp.zeros((1, 1, CONV_W - 1, D_MODEL), state_conv.dtype))
    _, *meta_states = _trunk(meta.astype(x_prompt.dtype), *zeros, p, 1, N_META, N_META, N_META)
    wkv0, shift0, conv0 = [jnp.broadcast_to(s, (1, bp) + s.shape[2:]) for s in meta_states]
    yp, wkv_p, shift_p, conv_p = _trunk(x_prompt.reshape(bp * seq, D_MODEL), wkv0, shift0, conv0, p, bp, seq,
                                        512, 512)
    return (yp.reshape(bp, seq, D_MODEL), ys.reshape(bs, t_s, D_MODEL), wkv_p, shift_p, conv_p,
            wkv_s, shift_s, conv_s)
```

```python
import functools
import math

import jax
import jax.numpy as jnp
from jax import lax
from jax.experimental import pallas as pl
from jax.experimental.pallas import tpu as pltpu

D_MODEL = 1024
HEAD_DIM = 64
N_HEADS = D_MODEL // HEAD_DIM
D_FF = 2816
N_META = 16
DEPTH = 2
CONV_W = 3
RMS_EPS = 1e-6
GN_EPS = 64e-5
DECAY_SCALE = math.exp(-0.5)

LANES = 128
SUBLANES = 8
MXU_DIM = 256
PAIR = 2 * HEAD_DIM
WKV_CHUNK = 64
N_PAIRS = D_MODEL // PAIR
VMEM_LIMIT = 56 * 1024 * 1024

FFN_CHUNKS = ((0, 6 * MXU_DIM), (6 * MXU_DIM, D_FF))

F32 = jnp.float32
BF16 = jnp.bfloat16


def _dot(a, b):
    return jnp.dot(a, b, preferred_element_type=F32)


def _dot_nt(a, b):
    return lax.dot_general(a, b, (((1,), (1,)), ((), ())), preferred_element_type=F32)


def _dot_tn(a, b):
    return lax.dot_general(a, b, (((0,), (0,)), ((), ())), preferred_element_type=F32)


def _rms(x, g):
    ms = jnp.mean(x * x, axis=-1, keepdims=True)
    return x * lax.rsqrt(ms + RMS_EPS) * g


def _sigmoid(x):
    return 1.0 / (1.0 + jnp.exp(-x))


def _head_ones():
    r = lax.broadcasted_iota(jnp.int32, (PAIR, PAIR), 0) >> 6
    c = lax.broadcasted_iota(jnp.int32, (PAIR, PAIR), 1) >> 6
    return jnp.where(r == c, 1.0, 0.0).astype(BF16)


def _head_sum(x, ones, split=False):
    hi = x.astype(BF16)
    lo = (x - hi.astype(F32)).astype(BF16) if split else None
    cols = []
    for j in range(N_PAIRS):
        sl = slice(j * PAIR, (j + 1) * PAIR)
        s = _dot(hi[:, sl], ones)
        cols.append(s + _dot(lo[:, sl], ones) if split else s)
    return jnp.concatenate(cols, axis=1)


def _const_spec(shape):
    nd = len(shape)
    return pl.BlockSpec(shape, lambda *_: (0,) * nd, pipeline_mode=pl.Buffered(1))


def _params(sem):
    return pltpu.CompilerParams(dimension_semantics=sem, vmem_limit_bytes=VMEM_LIMIT)


def _ffn_kernel(x_ref, g_ref, wgu_ref, wd_ref, g2_ref, *out_refs, emit_x, emit_n):
    x = x_ref[...]
    xn = _rms(x, g_ref[...]).astype(BF16)
    acc = None
    for lo, hi in FFN_CHUNKS:
        gate = _dot(xn, wgu_ref[:, lo:hi])
        up = _dot(xn, wgu_ref[:, D_FF + lo:D_FF + hi])
        act = (gate * _sigmoid(gate) * up).astype(BF16)
        part = _dot(act, wd_ref[lo:hi, :])
        acc = part if acc is None else acc + part
    out = x + 0.5 * acc
    i = 0
    if emit_x:
        out_refs[i][...] = out
        i += 1
    if emit_n:
        out_refs[i][...] = _rms(out, g2_ref[...]).astype(out_refs[i].dtype)


def _ffn(x, g, wgu, wd, layer, pos, g2, tm, emit_x=True, emit_n=True, n_dtype=F32):
    m = x.shape[0]
    row = pl.BlockSpec((tm, D_MODEL), lambda i: (i, 0))
    n_out = int(emit_x) + int(emit_n)
    dtypes = [F32] * int(emit_x) + [n_dtype] * int(emit_n)

    def weight_spec(rows, cols):
        return pl.BlockSpec((None, None, rows, cols), lambda i: (layer, pos, 0, 0), pipeline_mode=pl.Buffered(1))

    outs = pl.pallas_call(
        functools.partial(_ffn_kernel, emit_x=emit_x, emit_n=emit_n),
        grid=(m // tm,),
        in_specs=[row, _const_spec((1, D_MODEL)), weight_spec(D_MODEL, 2 * D_FF),
                  weight_spec(D_FF, D_MODEL), _const_spec((1, D_MODEL))],
        out_specs=[row] * n_out,
        out_shape=[jax.ShapeDtypeStruct((m, D_MODEL), dt) for dt in dtypes],
        compiler_params=_params(("parallel",)),
        name="ffn",
    )(x, g.reshape(1, D_MODEL), wgu, wd, g2.reshape(1, D_MODEL))
    return tuple(outs)


def _rk_pre_core(h, prev, p, out_refs):
    (mu, wr, wk, wv, w0, w1, w2, a0, a1, a2, g1, g2, k_k, k_a) = [r[...] for r in p]
    xx = prev - h
    xr, xw, xk, xv, xa, xg = [(h + xx * mu[c:c + 1, :]).astype(BF16) for c in range(6)]
    r = _dot(xr, wr)
    k = _dot(xk, wk)
    v = _dot(xv, wv)
    z = w0 + _dot(jnp.tanh(_dot(xw, w1)).astype(BF16), w2)
    lw = -DECAY_SCALE * _sigmoid(z)
    a = _sigmoid(a0 + _dot(_dot(xa, a1).astype(BF16), a2))
    g = _dot(_sigmoid(_dot(xg, g1)).astype(BF16), g2)
    kk = k * k_k
    ss = _head_sum(kk * kk, _head_ones())
    kk = kk / jnp.maximum(jnp.sqrt(ss), 1e-12)
    kmod = k * (1.0 + (a - 1.0) * k_a)
    for ref, val in zip(out_refs, (r, lw, kmod, v, kk, kk * a, g)):
        ref[...] = val.astype(ref.dtype)


def _rk_pre_prompt_kernel(h_ref, s0_ref, *refs, tm):
    p, out_refs, carry_ref = refs[:14], refs[14:21], refs[21]

    @pl.when(pl.program_id(1) == 0)
    def _():
        carry_ref[0:1, :] = s0_ref[...]

    h = h_ref[...]
    row = lax.broadcasted_iota(jnp.int32, h.shape, 0)
    prev = jnp.where(row == 0, carry_ref[0:1, :], pltpu.roll(h, 1, axis=0))
    carry_ref[0:1, :] = h[tm - 1:tm, :]
    _rk_pre_core(h, prev, p, out_refs)


def _rk_pre_sample_kernel(h_ref, s0_ref, *refs, ns, t_len):
    p, out_refs = refs[:14], refs[14:21]
    h = h_ref[...]
    rows = ns * t_len
    s0 = jnp.broadcast_to(s0_ref[...], (ns, t_len, D_MODEL)).reshape(rows, D_MODEL)
    t = lax.broadcasted_iota(jnp.int32, h.shape, 0) & (t_len - 1)
    prev = jnp.where(t == 0, s0, pltpu.roll(h, 1, axis=0))
    _rk_pre_core(h, prev, p, out_refs)


def _rk_param_specs(p):
    return [_const_spec(a.shape) for a in p]


def _rk_pre(hn, shift0, p, bsz, t_len, tm):
    m = bsz * t_len
    out_shape = [jax.ShapeDtypeStruct((m, D_MODEL), F32 if i == 1 else BF16) for i in range(7)]
    s0 = shift0.reshape(bsz, 1, D_MODEL)
    if t_len > tm:
        nt = t_len // tm
        row = pl.BlockSpec((tm, D_MODEL), lambda b, t: (b * nt + t, 0))
        return pl.pallas_call(
            functools.partial(_rk_pre_prompt_kernel, tm=tm),
            grid=(bsz, nt),
            in_specs=[row, pl.BlockSpec((None, 1, D_MODEL), lambda b, t: (b, 0, 0))] + _rk_param_specs(p),
            out_specs=[row] * 7,
            out_shape=out_shape,
            scratch_shapes=[pltpu.VMEM((SUBLANES, D_MODEL), F32)],
            compiler_params=_params(("arbitrary", "arbitrary")),
            name="rk_pre_prompt",
        )(hn, s0, *p)
    ns = tm // t_len
    row = pl.BlockSpec((tm, D_MODEL), lambda i: (i, 0))
    return pl.pallas_call(
        functools.partial(_rk_pre_sample_kernel, ns=ns, t_len=t_len),
        grid=(m // tm,),
        in_specs=[row, pl.BlockSpec((ns, 1, D_MODEL), lambda i: (i, 0, 0))] + _rk_param_specs(p),
        out_specs=[row] * 7,
        out_shape=out_shape,
        compiler_params=_params(("parallel",)),
        name="rk_pre_sample",
    )(hn, s0, *p)


def _wkv_masks(c):
    rows = 128
    sh = c.bit_length() - 1
    row = lax.broadcasted_iota(jnp.int32, (rows, rows), 0)
    col = lax.broadcasted_iota(jnp.int32, (rows, rows), 1)
    same = (row >> sh) == (col >> sh)
    strict = same & (row > col)
    incl = same & (row >= col)
    base = (row >> 1) == (col >> 1)
    merges = []
    s = 2
    while s < c:
        b = s.bit_length() - 1
        merges.append(((row >> (b + 1)) == (col >> (b + 1))) & (((row >> b) & 1) == 1) & (((col >> b) & 1) == 0))
        s *= 2
    eye = jnp.where(row == col, 1.0, 0.0).astype(F32)
    return strict, incl, base, merges, eye


def _cumsum_rows(x, c):
    t = lax.broadcasted_iota(jnp.int32, x.shape, 0)
    s = 1
    while s < c:
        x = x + jnp.where(t >= s, pltpu.roll(x, s, axis=0), 0.0)
        s *= 2
    return x


def _wkv_units(ins, states, c, masks):
    strict, incl, base, merges, eye = masks
    nseq = len(ins[0])
    rows = 128
    assert nseq * 2 * c == rows
    lane_h = lax.broadcasted_iota(jnp.int32, (c, PAIR), 1) >> 6
    seq_rows = [slice(i * 2 * c, (i + 1) * 2 * c) for i in range(nseq)]

    def stack(xs):
        return jnp.concatenate([jnp.where(lane_h == h, x, 0.0) for x in xs for h in range(2)],
                               axis=0).astype(BF16)

    def each(f, *lists):
        return [f(*xs) for xs in zip(*lists)]

    def prep(seqs):
        per = []
        for r, lw, k, v, kk, b in seqs:
            cum = _cumsum_rows(lw, c)
            last = cum[c - 1:c, :]
            to_end = jnp.exp(last - cum)
            p_inv = jnp.exp(-cum)
            per.append(dict(p_c=jnp.exp(last), kk=kk * jnp.exp(cum - lw), r=r * jnp.exp(cum), k=k * p_inv,
                            b=b * p_inv, v=v, kd=k * to_end, nbd=-(b * to_end)))
        out = {name: stack([d[name] for d in per]) for name in ('kk', 'r', 'k', 'b', 'v', 'kd', 'nbd')}
        out['p_c'] = [d['p_c'] for d in per]
        return out

    o = [prep(x) for x in ins]
    kb = [jnp.concatenate([x['k'], x['b']], axis=0) for x in o]
    a_kk = each(lambda x, w_: _dot_nt(x['kk'], w_), o, kb)
    a_r = each(lambda x, w_: _dot_nt(x['r'], w_), o, kb)
    a_kk_k, a_kk_b = [a[:, :rows] for a in a_kk], [a[:, rows:] for a in a_kk]
    a_r_k, a_r_b = [a[:, :rows] for a in a_r], [a[:, rows:] for a in a_r]

    lm = [jnp.where(strict, a, 0.0) for a in a_kk_b]
    inv = [eye - jnp.where(base, l, 0.0) for l in lm]
    size = 2
    for m in merges:
        if size >= SUBLANES:
            starts = range(size, rows, 2 * size)

            def gather(x, starts=starts, size=size):
                return jnp.concatenate([x[r0:r0 + size] for r0 in starts], axis=0)

            def scatter(xh, n=len(starts), size=size):
                zero = jnp.zeros((size, rows), F32)
                return jnp.concatenate([p_ for j in range(n) for p_ in (zero, xh[j * size:(j + 1) * size])],
                                       axis=0)

            inv_b = [t.astype(BF16) for t in inv]
            mid = each(lambda l, t: scatter(_dot(gather(jnp.where(m, l, 0.0)).astype(BF16), t)).astype(BF16),
                       lm, inv_b)
            inv = each(lambda t, md: t - scatter(_dot(gather(t).astype(BF16), md)), inv, mid)
        else:
            inv_b = [t.astype(BF16) for t in inv]
            mid = each(lambda l, t: _dot(jnp.where(m, l, 0.0).astype(BF16), t).astype(BF16), lm, inv_b)
            inv = each(lambda t, tb, md: t - _dot(tb, md), inv, inv_b, mid)
        size *= 2

    def state_dot(x, s_list):
        outs = [_dot_nt(jnp.concatenate([x['kk'][rs], x['r'][rs]], axis=0), s.astype(BF16))
                for rs, s in zip(seq_rows, s_list)]
        h = 2 * c
        if nseq == 1:
            return outs[0][:h], outs[0][h:]
        return (jnp.concatenate([t[:h] for t in outs], axis=0), jnp.concatenate([t[h:] for t in outs], axis=0))

    sd = each(state_dot, o, states)
    kkh, rh = [t[0] for t in sd], [t[1] for t in sd]
    rhs = each(lambda h, a, x: h + _dot(jnp.where(strict, a, 0.0).astype(BF16), x['v']), kkh, a_kk_k, o)
    u_b = each(lambda t, z: _dot(t.astype(BF16), z.astype(BF16)).astype(BF16), inv, rhs)
    y_s = each(lambda h, ak, ab, x, u: h + _dot(jnp.where(incl, ak, 0.0).astype(BF16), x['v'])
               - _dot(jnp.where(incl, ab, 0.0).astype(BF16), u), rh, a_r_k, a_r_b, o, u_b)

    def fold(ys):
        return [ys[rs][:c] + ys[rs][c:] for rs in seq_rows]

    def new_state(x, u, s_list):
        return [s * pc + _dot_tn(jnp.concatenate([x['v'][rs], u[rs]], axis=0),
                                 jnp.concatenate([x['kd'][rs], x['nbd'][rs]], axis=0))
                for rs, s, pc in zip(seq_rows, s_list, x['p_c'])]

    return [fold(ys) for ys in y_s], each(new_state, o, u_b, states)


def _rows_to_blockdiag(s):
    ext = jnp.concatenate([s, jnp.zeros_like(s)], axis=1)
    row = lax.broadcasted_iota(jnp.int32, ext.shape, 0)
    return jnp.where(row < HEAD_DIM, ext, pltpu.roll(ext, HEAD_DIM, axis=1))


def _blockdiag_to_rows(s_bd):
    row = lax.broadcasted_iota(jnp.int32, s_bd.shape, 0)
    return jnp.where(row < HEAD_DIM, s_bd, pltpu.roll(s_bd, HEAD_DIM, axis=1))[:, :HEAD_DIM]


def _wkv_prompt_kernel(r_ref, lw_ref, k_ref, v_ref, kk_ref, b_ref, s0_ref, y_ref, sout_ref, s_ref,
                       *, c, t_len, nb):
    ci = pl.program_id(1)
    units = [(i, q) for i in range(nb) for q in range(N_PAIRS)]

    @pl.when(ci == 0)
    def _():
        for i, q in units:
            s_ref[i, q] = _rows_to_blockdiag(s0_ref[i, q])

    masks = _wkv_masks(c)
    rows = lax.broadcasted_iota(jnp.int32, (c, PAIR), 0) + ci * c
    valid = rows < t_len
    lanes = [slice(q * PAIR, (q + 1) * PAIR) for q in range(N_PAIRS)]
    ins = [[[jnp.where(valid, ref[i, :, lanes[q]].astype(F32), 0.0)
             for ref in (r_ref, lw_ref, k_ref, v_ref, kk_ref, b_ref)]] for i, q in units]
    ys, s_new = _wkv_units(ins, [[s_ref[i, q]] for i, q in units], c, masks)
    for (i, q), y, s in zip(units, ys, s_new):
        y_ref[i, :, lanes[q]] = y[0]
        s_ref[i, q] = s[0]

    @pl.when(ci == pl.num_programs(1) - 1)
    def _():
        for i, q in units:
            sout_ref[i, q] = _blockdiag_to_rows(s_ref[i, q])


def _wkv_sample_kernel(r_ref, lw_ref, k_ref, v_ref, kk_ref, b_ref, s0_ref, y_ref, sout_ref, *, c, nseq):
    masks = _wkv_masks(c)
    lanes = [slice(q * PAIR, (q + 1) * PAIR) for q in range(N_PAIRS)]
    full = [ref[...].astype(F32) for ref in (r_ref, lw_ref, k_ref, v_ref, kk_ref, b_ref)]
    ins = [[[a[i * c:(i + 1) * c, lanes[q]] for a in full] for i in range(nseq)] for q in range(N_PAIRS)]
    states = [[_rows_to_blockdiag(s0_ref[i, q]) for i in range(nseq)] for q in range(N_PAIRS)]
    ys, s_new = _wkv_units(ins, states, c, masks)
    for q in range(N_PAIRS):
        for i in range(nseq):
            y_ref[i * c:(i + 1) * c, lanes[q]] = ys[q][i]
            sout_ref[i, q] = _blockdiag_to_rows(s_new[q][i])


def _wkv(ins, state, bsz, t_len, t_valid):
    m = bsz * t_len
    y_shape = jax.ShapeDtypeStruct((m, D_MODEL), F32)
    s_rows = state.reshape(bsz, N_PAIRS, PAIR, HEAD_DIM)
    s_shape = jax.ShapeDtypeStruct(s_rows.shape, F32)
    if t_len >= WKV_CHUNK:
        c, nb = WKV_CHUNK, min(bsz, 2)
        nc = t_len // c
        ins3 = [a.reshape(bsz, t_len, D_MODEL) for a in ins]
        blk = pl.BlockSpec((nb, c, D_MODEL), lambda b, i: (b, i, 0))
        sblk = pl.BlockSpec((nb, N_PAIRS, PAIR, HEAD_DIM), lambda b, i: (b, 0, 0, 0))
        y, s_out = pl.pallas_call(
            functools.partial(_wkv_prompt_kernel, c=c, t_len=t_valid, nb=nb),
            grid=(bsz // nb, nc),
            in_specs=[blk] * 6 + [sblk],
            out_specs=[blk, sblk],
            out_shape=[jax.ShapeDtypeStruct((bsz, t_len, D_MODEL), F32), s_shape],
            scratch_shapes=[pltpu.VMEM((nb, N_PAIRS, PAIR, PAIR), F32)],
            compiler_params=_params(("arbitrary", "arbitrary")),
            name="wkv_prompt",
        )(*ins3, s_rows)
        return y.reshape(m, D_MODEL), s_out.reshape(state.shape)
    assert t_valid == t_len
    c, nseq = t_len, WKV_CHUNK // t_len
    blk = pl.BlockSpec((nseq * c, D_MODEL), lambda i: (i, 0))
    sblk = pl.BlockSpec((nseq, N_PAIRS, PAIR, HEAD_DIM), lambda i: (i, 0, 0, 0))
    y, s_out = pl.pallas_call(
        functools.partial(_wkv_sample_kernel, c=c, nseq=nseq),
        grid=(bsz // nseq,),
        in_specs=[blk] * 6 + [sblk],
        out_specs=[blk, sblk],
        out_shape=[y_shape, s_shape],
        compiler_params=_params(("parallel",)),
        name="wkv_sample",
    )(*ins, s_rows)
    return y, s_out.reshape(state.shape)


def _rk_post_kernel(x_ref, y_ref, r_ref, k_ref, v_ref, g_ref, lnw_ref, lnb_ref, rk_ref, wo_ref, o_ref):
    ones = _head_ones()
    y = y_ref[...]
    inv_n = 1.0 / HEAD_DIM
    mean = _head_sum(y, ones, split=True) * inv_n
    yc = y - mean
    var = _head_sum(yc * yc, ones) * inv_n
    yn = yc * lax.rsqrt(var + GN_EPS) * lnw_ref[...] + lnb_ref[...]
    rk = r_ref[...].astype(F32) * k_ref[...].astype(F32) * rk_ref[...]
    bonus = _head_sum(rk, ones) * v_ref[...].astype(F32)
    o = ((yn + bonus) * g_ref[...].astype(F32)).astype(BF16)
    o_ref[...] = x_ref[...] + _dot(o, wo_ref[...])


def _rk_post(x, y, r, k, v, g, ln_w, ln_b, r_k, w_o, tm):
    m = x.shape[0]
    row = pl.BlockSpec((tm, D_MODEL), lambda i: (i, 0))
    vec = _const_spec((1, D_MODEL))
    return pl.pallas_call(
        _rk_post_kernel,
        grid=(m // tm,),
        in_specs=[row] * 6 + [vec, vec, vec, _const_spec((D_MODEL, D_MODEL))],
        out_specs=row,
        out_shape=jax.ShapeDtypeStruct((m, D_MODEL), F32),
        compiler_params=_params(("parallel",)),
        name="rk_post",
    )(x, y, r, k, v, g, ln_w.reshape(1, D_MODEL), ln_b.reshape(1, D_MODEL), r_k.reshape(1, D_MODEL), w_o)


def _conv_core(x, h, shifted, win_ref, cw_ref, wout_ref):
    z = _dot(h.astype(BF16), win_ref[...])
    gate_b = z[:, 0:D_MODEL]
    u = z[:, D_MODEL:2 * D_MODEL] * z[:, 2 * D_MODEL:3 * D_MODEL]
    cw = cw_ref[...]
    u1, u2 = shifted(u)
    conv = u2 * cw[0:1, :] + u1 * cw[1:2, :] + u * cw[2:3, :]
    out = x + _dot((gate_b * conv).astype(BF16), wout_ref[...])
    return out, u


def _conv_prompt_kernel(x_ref, h_ref, c0_ref, win_ref, cw_ref, wout_ref, o_ref, st_ref, carry_ref, *, tm):
    @pl.when(pl.program_id(1) == 0)
    def _():
        carry_ref[0:2, :] = c0_ref[...]

    def shifted(u):
        row = lax.broadcasted_iota(jnp.int32, u.shape, 0)
        c0 = carry_ref[0:1, :]
        c1 = carry_ref[1:2, :]
        u1 = jnp.where(row == 0, c1, pltpu.roll(u, 1, axis=0))
        u2 = jnp.where(row == 0, c0, jnp.where(row == 1, c1, pltpu.roll(u, 2, axis=0)))
        return u1, u2

    out, u = _conv_core(x_ref[...], h_ref[...], shifted, win_ref, cw_ref, wout_ref)
    o_ref[...] = out
    carry_ref[0:2, :] = u[tm - 2:tm, :]
    st_ref[...] = u[tm - SUBLANES:tm, :]


def _conv_sample_kernel(x_ref, h_ref, c0_ref, win_ref, cw_ref, wout_ref, o_ref, u_ref, *, ns, t_len):
    rows = ns * t_len
    c0 = jnp.broadcast_to(c0_ref[:, 0:1, :], (ns, t_len, D_MODEL)).reshape(rows, D_MODEL)
    c1 = jnp.broadcast_to(c0_ref[:, 1:2, :], (ns, t_len, D_MODEL)).reshape(rows, D_MODEL)

    def shifted(u):
        t = lax.broadcasted_iota(jnp.int32, u.shape, 0) & (t_len - 1)
        u1 = jnp.where(t == 0, c1, pltpu.roll(u, 1, axis=0))
        u2 = jnp.where(t == 0, c0, jnp.where(t == 1, c1, pltpu.roll(u, 2, axis=0)))
        return u1, u2

    out, u = _conv_core(x_ref[...], h_ref[...], shifted, win_ref, cw_ref, wout_ref)
    o_ref[...] = out
    u_ref[...] = u


def _conv_mix(x, hn, conv0, w_in, conv_w, w_out, bsz, t_len, t_valid, tm):
    m = bsz * t_len
    consts = [_const_spec((D_MODEL, 3 * D_MODEL)), _const_spec((CONV_W, D_MODEL)), _const_spec((D_MODEL, D_MODEL))]
    if t_len > tm:
        assert t_valid == t_len
        nt = t_len // tm
        row = pl.BlockSpec((tm, D_MODEL), lambda b, t: (b * nt + t, 0))
        out, tail = pl.pallas_call(
            functools.partial(_conv_prompt_kernel, tm=tm),
            grid=(bsz, nt),
            in_specs=[row, row, pl.BlockSpec((None, CONV_W - 1, D_MODEL), lambda b, t: (b, 0, 0))] + consts,
            out_specs=[row, pl.BlockSpec((None, SUBLANES, D_MODEL), lambda b, t: (b, 0, 0))],
            out_shape=[jax.ShapeDtypeStruct((m, D_MODEL), F32),
                       jax.ShapeDtypeStruct((bsz, SUBLANES, D_MODEL), F32)],
            scratch_shapes=[pltpu.VMEM((SUBLANES, D_MODEL), F32)],
            compiler_params=_params(("arbitrary", "arbitrary")),
            name="conv_prompt",
        )(x, hn, conv0, w_in, conv_w, w_out)
        return out, tail[:, SUBLANES - (CONV_W - 1):, :]
    ns = tm // t_len
    row = pl.BlockSpec((tm, D_MODEL), lambda i: (i, 0))
    out, u = pl.pallas_call(
        functools.partial(_conv_sample_kernel, ns=ns, t_len=t_len),
        grid=(m // tm,),
        in_specs=[row, row, pl.BlockSpec((ns, CONV_W - 1, D_MODEL), lambda i: (i, 0, 0))] + consts,
        out_specs=[row, row],
        out_shape=[jax.ShapeDtypeStruct((m, D_MODEL), F32)] * 2,
        compiler_params=_params(("parallel",)),
        name="conv_sample",
    )(x, hn, conv0, w_in, conv_w, w_out)
    return out, u.reshape(bsz, t_len, D_MODEL)[:, t_valid - (CONV_W - 1):t_valid, :]


def _trunk(x, wkv_in, shift_in, conv_in, p, bsz, t_len, t_valid, tm, tm_rk):
    ffn = functools.partial(_ffn, wgu=p['ffn_w_gu'], wd=p['ffn_w_down'], tm=tm)
    x, hn = ffn(x, p['ffn_norm'][0, 0], layer=0, pos=0, g2=p['mix_norm'][0])
    new_shift = hn.reshape(bsz, t_len, D_MODEL)[:, t_valid - 1]
    r, lw, k, v, kk, b, g = _rk_pre(hn, shift_in[0], p['rk_pre'], bsz, t_len, tm_rk)
    y, new_wkv = _wkv([r, lw, k, v, kk, b], wkv_in[0], bsz, t_len, t_valid)
    x = _rk_post(x, y, r, k, v, g, p['rk_ln_w'][0], p['rk_ln_b'][0], p['rk_r_k'][0], p['rk_w_o'][0], tm)
    (x,) = ffn(x, p['ffn_norm'][0, 1], layer=0, pos=1, g2=p['ffn_norm'][1, 0], emit_n=False)
    x, hn = ffn(x, p['ffn_norm'][1, 0], layer=1, pos=0, g2=p['mix_norm'][1], n_dtype=BF16)
    x, new_conv = _conv_mix(x, hn, conv_in[0], p['sc_w_in'][0], p['sc_conv_w'][0], p['sc_w_out'][0],
                            bsz, t_len, t_valid, tm)
    (y_out,) = ffn(x, p['ffn_norm'][1, 1], layer=1, pos=1, g2=p['final_norm'], emit_x=False)
    return y_out, new_wkv[None], new_shift[None], new_conv[None]


def kernel(x_prompt, x_sample, state_wkv, state_shift, state_conv, meta, ffn_norm, ffn_w_gu, ffn_w_down,
           mix_norm, final_norm, rk_mu, rk_w_rkv, rk_w0, rk_w1, rk_w2, rk_a0, rk_a1, rk_a2, rk_g1, rk_g2,
           rk_k_k, rk_k_a, rk_r_k, rk_ln_w, rk_ln_b, rk_w_o, sc_w_in, sc_conv_w, sc_w_out):
    assert DEPTH == 2 and rk_mu.shape[0] == 1 and sc_w_in.shape[0] == 1
    bf = lambda a: a.astype(BF16)
    vec = lambda a: a.reshape(1, D_MODEL)
    p = {
        'ffn_norm': ffn_norm, 'ffn_w_gu': bf(ffn_w_gu), 'ffn_w_down': bf(ffn_w_down), 'mix_norm': mix_norm,
        'final_norm': final_norm,
        'rk_pre': [rk_mu[0], bf(rk_w_rkv[0, 0]), bf(rk_w_rkv[0, 1]), bf(rk_w_rkv[0, 2]), vec(rk_w0[0]),
                   bf(rk_w1[0]), bf(rk_w2[0]), vec(rk_a0[0]), bf(rk_a1[0]), bf(rk_a2[0]), bf(rk_g1[0]),
                   bf(rk_g2[0]), vec(rk_k_k[0]), vec(rk_k_a[0])],
        'rk_r_k': rk_r_k.reshape(1, D_MODEL), 'rk_ln_w': rk_ln_w, 'rk_ln_b': rk_ln_b, 'rk_w_o': bf(rk_w_o),
        'sc_w_in': bf(sc_w_in), 'sc_conv_w': sc_conv_w, 'sc_w_out': bf(sc_w_out),
    }
    bs, t_s, _ = x_sample.shape
    ys, wkv_s, shift_s, conv_s = _trunk(x_sample.reshape(bs * t_s, D_MODEL), state_wkv, state_shift, state_conv,
                                        p, bs, t_s, t_s, 256, 256)
    bp, seq, _ = x_prompt.shape
    zeros = (jnp.zeros((1, 1, N_HEADS, HEAD_DIM, HEAD_DIM), state_wkv.dtype),
             jnp.zeros((1, 1, D_MODEL), state_shift.dtype),
             jnp.zeros((1, 1, CONV_W - 1, D_MODEL), state_conv.dtype))
    meta_rows = jnp.pad(meta.astype(x_prompt.dtype), ((0, WKV_CHUNK - N_META), (0, 0)))
    _, *meta_states = _trunk(meta_rows, *zeros, p, 1, WKV_CHUNK, N_META, WKV_CHUNK, WKV_CHUNK)
    wkv0, shift0, conv0 = [jnp.broadcast_to(s, (1, bp) + s.shape[2:]) for s in meta_states]
    yp, wkv_p, shift_p, conv_p = _trunk(x_prompt.reshape(bp * seq, D_MODEL), wkv0, shift0, conv0, p, bp, seq,
                                        seq, 512, 512)
    return (yp.reshape(bp, seq, D_MODEL), ys.reshape(bs, t_s, D_MODEL), wkv_p, shift_p, conv_p,
            wkv_s, shift_s, conv_s)
```

```python
import functools
import math

import jax
import jax.numpy as jnp
from jax import lax
from jax.experimental import pallas as pl
from jax.experimental.pallas import tpu as pltpu

D_MODEL = 1024
HEAD_DIM = 64
N_HEADS = D_MODEL // HEAD_DIM
D_FF = 2816
N_META = 16
DEPTH = 2
CONV_W = 3
RMS_EPS = 1e-6
GN_EPS = 64e-5
DECAY_SCALE = math.exp(-0.5)

LANES = 128
SUBLANES = 8
BF16_ROWS = 16
MXU_DIM = 256
PAIR = 2 * HEAD_DIM
WKV_CHUNK = 64
N_PAIRS = D_MODEL // PAIR
VMEM_LIMIT = 56 * 1024 * 1024

FFN_CHUNKS = ((0, 6 * MXU_DIM), (6 * MXU_DIM, D_FF))

F32 = jnp.float32
BF16 = jnp.bfloat16


def _dot(a, b):
    return jnp.dot(a, b, preferred_element_type=F32)


def _dot_nt(a, b):
    return lax.dot_general(a, b, (((1,), (1,)), ((), ())), preferred_element_type=F32)


def _dot_tn(a, b):
    return lax.dot_general(a, b, (((0,), (0,)), ((), ())), preferred_element_type=F32)


def _rms(x, g):
    ms = jnp.mean(x * x, axis=-1, keepdims=True)
    return x * lax.rsqrt(ms + RMS_EPS) * g


def _sigmoid(x):
    return 1.0 / (1.0 + jnp.exp(-x))


def _head_ones():
    r = lax.broadcasted_iota(jnp.int32, (PAIR, PAIR), 0) >> 6
    c = lax.broadcasted_iota(jnp.int32, (PAIR, PAIR), 1) >> 6
    return jnp.where(r == c, 1.0, 0.0).astype(BF16)


def _head_sum(x, ones, split=False):
    hi = x.astype(BF16)
    lo = (x - hi.astype(F32)).astype(BF16) if split else None
    cols = []
    for j in range(N_PAIRS):
        sl = slice(j * PAIR, (j + 1) * PAIR)
        s = _dot(hi[:, sl], ones)
        cols.append(s + _dot(lo[:, sl], ones) if split else s)
    return jnp.concatenate(cols, axis=1)


def _const_spec(shape):
    nd = len(shape)
    return pl.BlockSpec(shape, lambda *_: (0,) * nd, pipeline_mode=pl.Buffered(1))


def _params(sem):
    return pltpu.CompilerParams(dimension_semantics=sem, vmem_limit_bytes=VMEM_LIMIT)


def _ffn_kernel(x_ref, g_ref, wgu_ref, wd_ref, g2_ref, *out_refs, emit_x, emit_n):
    x = x_ref[...]
    xn = _rms(x, g_ref[...]).astype(BF16)
    acc = None
    for lo, hi in FFN_CHUNKS:
        gate = _dot(xn, wgu_ref[:, lo:hi])
        up = _dot(xn, wgu_ref[:, D_FF + lo:D_FF + hi])
        act = (gate * _sigmoid(gate) * up).astype(BF16)
        part = _dot(act, wd_ref[lo:hi, :])
        acc = part if acc is None else acc + part
    out = x + 0.5 * acc
    i = 0
    if emit_x:
        out_refs[i][...] = out
        i += 1
    if emit_n:
        out_refs[i][...] = _rms(out, g2_ref[...]).astype(out_refs[i].dtype)


def _ffn(x, g, w, g2, tm, emit_x=True, emit_n=True, n_dtype=F32):
    wgu, wd, idx = w
    m = x.shape[0]
    row = pl.BlockSpec((tm, D_MODEL), lambda i: (i, 0))
    n_out = int(emit_x) + int(emit_n)
    dtypes = [F32] * int(emit_x) + [n_dtype] * int(emit_n)

    def weight_spec(rows, cols):
        return pl.BlockSpec((None, rows, cols), lambda i: (idx, 0, 0), pipeline_mode=pl.Buffered(1))

    outs = pl.pallas_call(
        functools.partial(_ffn_kernel, emit_x=emit_x, emit_n=emit_n),
        grid=(m // tm,),
        in_specs=[row, _const_spec((1, D_MODEL)), weight_spec(D_MODEL, 2 * D_FF),
                  weight_spec(D_FF, D_MODEL), _const_spec((1, D_MODEL))],
        out_specs=[row] * n_out,
        out_shape=[jax.ShapeDtypeStruct((m, D_MODEL), dt) for dt in dtypes],
        compiler_params=_params(("parallel",)),
        name="ffn",
    )(x, g.reshape(1, D_MODEL), wgu, wd, g2.reshape(1, D_MODEL))
    return tuple(outs)


def _rk_pre_core(h, prev, p, out_refs):
    (mu, wr, wk, wv, w0, w1, w2, a0, a1, a2, g1, g2, k_k, k_a) = [r[...] for r in p]
    xx = prev - h
    xr, xw, xk, xv, xa, xg = [(h + xx * mu[c:c + 1, :]).astype(BF16) for c in range(6)]
    r = _dot(xr, wr)
    k = _dot(xk, wk)
    v = _dot(xv, wv)
    z = w0 + _dot(jnp.tanh(_dot(xw, w1)).astype(BF16), w2)
    lw = -DECAY_SCALE * _sigmoid(z)
    a = _sigmoid(a0 + _dot(_dot(xa, a1).astype(BF16), a2))
    g = _dot(_sigmoid(_dot(xg, g1)).astype(BF16), g2)
    kk = k * k_k
    ss = _head_sum(kk * kk, _head_ones())
    kk = kk / jnp.maximum(jnp.sqrt(ss), 1e-12)
    kmod = k * (1.0 + (a - 1.0) * k_a)
    for ref, val in zip(out_refs, (r, lw, kmod, v, kk, kk * a, g)):
        ref[...] = val.astype(ref.dtype)


def _rk_pre_prompt_kernel(h_ref, s0_ref, *refs, tm):
    p, out_refs, carry_ref = refs[:14], refs[14:21], refs[21]

    @pl.when(pl.program_id(1) == 0)
    def _():
        carry_ref[0:1, :] = s0_ref[...]

    h = h_ref[...]
    row = lax.broadcasted_iota(jnp.int32, h.shape, 0)
    prev = jnp.where(row == 0, carry_ref[0:1, :], pltpu.roll(h, 1, axis=0))
    carry_ref[0:1, :] = h[tm - 1:tm, :]
    _rk_pre_core(h, prev, p, out_refs)


def _rk_pre_sample_kernel(h_ref, s0_ref, *refs, ns, t_len):
    p, out_refs = refs[:14], refs[14:21]
    h = h_ref[...]
    rows = ns * t_len
    s0 = jnp.broadcast_to(s0_ref[...], (ns, t_len, D_MODEL)).reshape(rows, D_MODEL)
    t = lax.broadcasted_iota(jnp.int32, h.shape, 0) & (t_len - 1)
    prev = jnp.where(t == 0, s0, pltpu.roll(h, 1, axis=0))
    _rk_pre_core(h, prev, p, out_refs)


def _rk_param_specs(p):
    return [_const_spec(a.shape) for a in p]


def _rk_pre(hn, shift0, p, bsz, t_len, tm):
    m = bsz * t_len
    out_shape = [jax.ShapeDtypeStruct((m, D_MODEL), F32 if i == 1 else BF16) for i in range(7)]
    s0 = shift0.reshape(bsz, 1, D_MODEL)
    if t_len > tm:
        nt = t_len // tm
        row = pl.BlockSpec((tm, D_MODEL), lambda b, t: (b * nt + t, 0))
        return pl.pallas_call(
            functools.partial(_rk_pre_prompt_kernel, tm=tm),
            grid=(bsz, nt),
            in_specs=[row, pl.BlockSpec((None, 1, D_MODEL), lambda b, t: (b, 0, 0))] + _rk_param_specs(p),
            out_specs=[row] * 7,
            out_shape=out_shape,
            scratch_shapes=[pltpu.VMEM((SUBLANES, D_MODEL), F32)],
            compiler_params=_params(("arbitrary", "arbitrary")),
            name="rk_pre_prompt",
        )(hn, s0, *p)
    ns = tm // t_len
    row = pl.BlockSpec((tm, D_MODEL), lambda i: (i, 0))
    return pl.pallas_call(
        functools.partial(_rk_pre_sample_kernel, ns=ns, t_len=t_len),
        grid=(m // tm,),
        in_specs=[row, pl.BlockSpec((ns, 1, D_MODEL), lambda i: (i, 0, 0))] + _rk_param_specs(p),
        out_specs=[row] * 7,
        out_shape=out_shape,
        compiler_params=_params(("parallel",)),
        name="rk_pre_sample",
    )(hn, s0, *p)


def _wkv_masks(c):
    rows = 128
    sh = c.bit_length() - 1
    row = lax.broadcasted_iota(jnp.int32, (rows, rows), 0)
    col = lax.broadcasted_iota(jnp.int32, (rows, rows), 1)
    same = (row >> sh) == (col >> sh)
    strict = same & (row > col)
    incl = same & (row >= col)
    base = (row >> 1) == (col >> 1)
    merges = []
    s = 2
    while s < c:
        b = s.bit_length() - 1
        merges.append(((row >> (b + 1)) == (col >> (b + 1))) & (((row >> b) & 1) == 1) & (((col >> b) & 1) == 0))
        s *= 2
    eye = jnp.where(row == col, 1.0, 0.0).astype(F32)
    return strict, incl, base, merges, eye


def _cumsum_rows(x, c):
    t = lax.broadcasted_iota(jnp.int32, x.shape, 0)
    s = 1
    while s < c:
        x = x + jnp.where(t >= s, pltpu.roll(x, s, axis=0), 0.0)
        s *= 2
    return x


def _wkv_units(ins, states, c, masks):
    strict, incl, base, merges, eye = masks
    nseq = len(ins[0])
    rows = 128
    assert nseq * 2 * c == rows
    lane_h = lax.broadcasted_iota(jnp.int32, (c, PAIR), 1) >> 6
    seq_rows = [slice(i * 2 * c, (i + 1) * 2 * c) for i in range(nseq)]

    def stack(xs):
        return jnp.concatenate([jnp.where(lane_h == h, x, 0.0) for x in xs for h in range(2)],
                               axis=0).astype(BF16)

    def each(f, *lists):
        return [f(*xs) for xs in zip(*lists)]

    def prep(seqs):
        per = []
        for r, lw, k, v, kk, b in seqs:
            cum = _cumsum_rows(lw, c)
            last = cum[c - 1:c, :]
            to_end = jnp.exp(last - cum)
            p_inv = jnp.exp(-cum)
            per.append(dict(p_c=jnp.exp(last), kk=kk * jnp.exp(cum - lw), r=r * jnp.exp(cum), k=k * p_inv,
                            b=b * p_inv, v=v, kd=k * to_end, nbd=-(b * to_end)))
        out = {name: stack([d[name] for d in per]) for name in ('kk', 'r', 'k', 'b', 'v', 'kd', 'nbd')}
        out['p_c'] = [d['p_c'] for d in per]
        return out

    o = [prep(x) for x in ins]
    kb = [jnp.concatenate([x['k'], x['b']], axis=0) for x in o]
    a_kk = each(lambda x, w_: _dot_nt(x['kk'], w_), o, kb)
    a_r = each(lambda x, w_: _dot_nt(x['r'], w_), o, kb)
    a_kk_k, a_kk_b = [a[:, :rows] for a in a_kk], [a[:, rows:] for a in a_kk]
    a_r_k, a_r_b = [a[:, :rows] for a in a_r], [a[:, rows:] for a in a_r]

    lm = [jnp.where(strict, a, 0.0) for a in a_kk_b]
    inv = [eye - jnp.where(base, l, 0.0) for l in lm]
    size = 2
    for m in merges:
        if size >= SUBLANES:
            starts = range(size, rows, 2 * size)

            def gather(x, starts=starts, size=size):
                return jnp.concatenate([x[r0:r0 + size] for r0 in starts], axis=0)

            def scatter(xh, n=len(starts), size=size):
                zero = jnp.zeros((size, rows), F32)
                return jnp.concatenate([p_ for j in range(n) for p_ in (zero, xh[j * size:(j + 1) * size])],
                                       axis=0)

            inv_b = [t.astype(BF16) for t in inv]
            mid = each(lambda l, t: scatter(_dot(gather(jnp.where(m, l, 0.0)).astype(BF16), t)).astype(BF16),
                       lm, inv_b)
            inv = each(lambda t, md: t - scatter(_dot(gather(t).astype(BF16), md)), inv, mid)
        else:
            inv_b = [t.astype(BF16) for t in inv]
            mid = each(lambda l, t: _dot(jnp.where(m, l, 0.0).astype(BF16), t).astype(BF16), lm, inv_b)
            inv = each(lambda t, tb, md: t - _dot(tb, md), inv, inv_b, mid)
        size *= 2

    def state_dot(x, s_list):
        outs = [_dot_nt(jnp.concatenate([x['kk'][rs], x['r'][rs]], axis=0), s.astype(BF16))
                for rs, s in zip(seq_rows, s_list)]
        h = 2 * c
        if nseq == 1:
            return outs[0][:h], outs[0][h:]
        return (jnp.concatenate([t[:h] for t in outs], axis=0), jnp.concatenate([t[h:] for t in outs], axis=0))

    sd = each(state_dot, o, states)
    kkh, rh = [t[0] for t in sd], [t[1] for t in sd]
    rhs = each(lambda h, a, x: h + _dot(jnp.where(strict, a, 0.0).astype(BF16), x['v']), kkh, a_kk_k, o)
    u_b = each(lambda t, z: _dot(t.astype(BF16), z.astype(BF16)).astype(BF16), inv, rhs)
    y_s = each(lambda h, ak, ab, x, u: h + _dot(jnp.where(incl, ak, 0.0).astype(BF16), x['v'])
               - _dot(jnp.where(incl, ab, 0.0).astype(BF16), u), rh, a_r_k, a_r_b, o, u_b)

    def fold(ys):
        return [ys[rs][:c] + ys[rs][c:] for rs in seq_rows]

    def new_state(x, u, s_list):
        return [s * pc + _dot_tn(jnp.concatenate([x['v'][rs], u[rs]], axis=0),
                                 jnp.concatenate([x['kd'][rs], x['nbd'][rs]], axis=0))
                for rs, s, pc in zip(seq_rows, s_list, x['p_c'])]

    return [fold(ys) for ys in y_s], each(new_state, o, u_b, states)


def _rows_to_blockdiag(s):
    ext = jnp.concatenate([s, jnp.zeros_like(s)], axis=1)
    row = lax.broadcasted_iota(jnp.int32, ext.shape, 0)
    return jnp.where(row < HEAD_DIM, ext, pltpu.roll(ext, HEAD_DIM, axis=1))


def _blockdiag_to_rows(s_bd):
    row = lax.broadcasted_iota(jnp.int32, s_bd.shape, 0)
    return jnp.where(row < HEAD_DIM, s_bd, pltpu.roll(s_bd, HEAD_DIM, axis=1))[:, :HEAD_DIM]


def _wkv_prompt_kernel(r_ref, lw_ref, k_ref, v_ref, kk_ref, b_ref, s0_ref, *refs, c, t_len, nb, n_cast):
    cast_in, (y_ref, sout_ref), cast_out, s_ref = (refs[:n_cast], refs[n_cast:n_cast + 2],
                                                   refs[n_cast + 2:2 * n_cast + 2], refs[2 * n_cast + 2])
    for w_in, w_out in zip(cast_in, cast_out):
        w_out[...] = w_in[...].astype(BF16)
    ci = pl.program_id(1)
    units = [(i, q) for i in range(nb) for q in range(N_PAIRS)]

    @pl.when(ci == 0)
    def _():
        for i, q in units:
            s_ref[i, q] = _rows_to_blockdiag(s0_ref[i, q])

    masks = _wkv_masks(c)
    rows = lax.broadcasted_iota(jnp.int32, (c, PAIR), 0) + ci * c
    valid = rows < t_len
    lanes = [slice(q * PAIR, (q + 1) * PAIR) for q in range(N_PAIRS)]
    ins = [[[jnp.where(valid, ref[i, :, lanes[q]].astype(F32), 0.0)
             for ref in (r_ref, lw_ref, k_ref, v_ref, kk_ref, b_ref)]] for i, q in units]
    ys, s_new = _wkv_units(ins, [[s_ref[i, q]] for i, q in units], c, masks)
    for (i, q), y, s in zip(units, ys, s_new):
        y_ref[i, :, lanes[q]] = y[0]
        s_ref[i, q] = s[0]

    @pl.when(ci == pl.num_programs(1) - 1)
    def _():
        for i, q in units:
            sout_ref[i, q] = _blockdiag_to_rows(s_ref[i, q])


def _wkv_sample_kernel(r_ref, lw_ref, k_ref, v_ref, kk_ref, b_ref, s0_ref, y_ref, sout_ref, *, c, nseq):
    masks = _wkv_masks(c)
    lanes = [slice(q * PAIR, (q + 1) * PAIR) for q in range(N_PAIRS)]
    full = [ref[...].astype(F32) for ref in (r_ref, lw_ref, k_ref, v_ref, kk_ref, b_ref)]
    ins = [[[a[i * c:(i + 1) * c, lanes[q]] for a in full] for i in range(nseq)] for q in range(N_PAIRS)]
    states = [[_rows_to_blockdiag(s0_ref[i, q]) for i in range(nseq)] for q in range(N_PAIRS)]
    ys, s_new = _wkv_units(ins, states, c, masks)
    for q in range(N_PAIRS):
        for i in range(nseq):
            y_ref[i * c:(i + 1) * c, lanes[q]] = ys[q][i]
            sout_ref[i, q] = _blockdiag_to_rows(s_new[q][i])


def _cast_slabs(w, skip, n_steps, nc):
    n_rows = w.shape[0] - skip
    slab = next(r for r in range(BF16_ROWS, n_rows + 1, BF16_ROWS)
                if n_rows % r == 0 and skip % r == 0 and n_rows // r <= n_steps)
    last = n_rows // slab - 1
    first = skip // slab
    in_spec = pl.BlockSpec((slab, w.shape[1]), lambda b, i: (first + jnp.minimum(b * nc + i, last), 0))
    out_spec = pl.BlockSpec((slab, w.shape[1]), lambda b, i: (jnp.minimum(b * nc + i, last), 0))
    return in_spec, out_spec, jax.ShapeDtypeStruct((n_rows, w.shape[1]), BF16)


def _wkv(ins, state, bsz, t_len, t_valid, cast=()):
    m = bsz * t_len
    y_shape = jax.ShapeDtypeStruct((m, D_MODEL), F32)
    s_rows = state.reshape(bsz, N_PAIRS, PAIR, HEAD_DIM)
    s_shape = jax.ShapeDtypeStruct(s_rows.shape, F32)
    if t_len >= WKV_CHUNK:
        c, nb = WKV_CHUNK, min(bsz, 2)
        nc = t_len // c
        ins3 = [a.reshape(bsz, t_len, D_MODEL) for a in ins]
        blk = pl.BlockSpec((nb, c, D_MODEL), lambda b, i: (b, i, 0))
        sblk = pl.BlockSpec((nb, N_PAIRS, PAIR, HEAD_DIM), lambda b, i: (b, 0, 0, 0))
        slabs = [_cast_slabs(w, skip, (bsz // nb) * nc, nc) for w, skip in cast]
        y, s_out, *cast_out = pl.pallas_call(
            functools.partial(_wkv_prompt_kernel, c=c, t_len=t_valid, nb=nb, n_cast=len(cast)),
            grid=(bsz // nb, nc),
            in_specs=[blk] * 6 + [sblk] + [s[0] for s in slabs],
            out_specs=[blk, sblk] + [s[1] for s in slabs],
            out_shape=[jax.ShapeDtypeStruct((bsz, t_len, D_MODEL), F32), s_shape] + [s[2] for s in slabs],
            scratch_shapes=[pltpu.VMEM((nb, N_PAIRS, PAIR, PAIR), F32)],
            compiler_params=_params(("arbitrary", "arbitrary")),
            name="wkv_prompt",
        )(*ins3, s_rows, *[w for w, _ in cast])
        return y.reshape(m, D_MODEL), s_out.reshape(state.shape), cast_out
    assert t_valid == t_len and not cast
    c, nseq = t_len, WKV_CHUNK // t_len
    blk = pl.BlockSpec((nseq * c, D_MODEL), lambda i: (i, 0))
    sblk = pl.BlockSpec((nseq, N_PAIRS, PAIR, HEAD_DIM), lambda i: (i, 0, 0, 0))
    y, s_out = pl.pallas_call(
        functools.partial(_wkv_sample_kernel, c=c, nseq=nseq),
        grid=(bsz // nseq,),
        in_specs=[blk] * 6 + [sblk],
        out_specs=[blk, sblk],
        out_shape=[y_shape, s_shape],
        compiler_params=_params(("parallel",)),
        name="wkv_sample",
    )(*ins, s_rows)
    return y, s_out.reshape(state.shape), []


def _rk_post_kernel(x_ref, y_ref, r_ref, k_ref, v_ref, g_ref, lnw_ref, lnb_ref, rk_ref, wo_ref, o_ref):
    ones = _head_ones()
    y = y_ref[...]
    inv_n = 1.0 / HEAD_DIM
    mean = _head_sum(y, ones, split=True) * inv_n
    yc = y - mean
    var = _head_sum(yc * yc, ones) * inv_n
    yn = yc * lax.rsqrt(var + GN_EPS) * lnw_ref[...] + lnb_ref[...]
    rk = r_ref[...].astype(F32) * k_ref[...].astype(F32) * rk_ref[...]
    bonus = _head_sum(rk, ones) * v_ref[...].astype(F32)
    o = ((yn + bonus) * g_ref[...].astype(F32)).astype(BF16)
    o_ref[...] = x_ref[...] + _dot(o, wo_ref[...])


def _rk_post(x, y, r, k, v, g, ln_w, ln_b, r_k, w_o, tm):
    m = x.shape[0]
    row = pl.BlockSpec((tm, D_MODEL), lambda i: (i, 0))
    vec = _const_spec((1, D_MODEL))
    return pl.pallas_call(
        _rk_post_kernel,
        grid=(m // tm,),
        in_specs=[row] * 6 + [vec, vec, vec, _const_spec((D_MODEL, D_MODEL))],
        out_specs=row,
        out_shape=jax.ShapeDtypeStruct((m, D_MODEL), F32),
        compiler_params=_params(("parallel",)),
        name="rk_post",
    )(x, y, r, k, v, g, ln_w.reshape(1, D_MODEL), ln_b.reshape(1, D_MODEL), r_k.reshape(1, D_MODEL), w_o)


def _conv_core(x, h, shifted, win_ref, cw_ref, wout_ref):
    z = _dot(h.astype(BF16), win_ref[...])
    gate_b = z[:, 0:D_MODEL]
    u = z[:, D_MODEL:2 * D_MODEL] * z[:, 2 * D_MODEL:3 * D_MODEL]
    cw = cw_ref[...]
    u1, u2 = shifted(u)
    conv = u2 * cw[0:1, :] + u1 * cw[1:2, :] + u * cw[2:3, :]
    out = x + _dot((gate_b * conv).astype(BF16), wout_ref[...])
    return out, u


def _conv_prompt_kernel(x_ref, h_ref, c0_ref, win_ref, cw_ref, wout_ref, o_ref, st_ref, carry_ref, *, tm):
    @pl.when(pl.program_id(1) == 0)
    def _():
        carry_ref[0:2, :] = c0_ref[...]

    def shifted(u):
        row = lax.broadcasted_iota(jnp.int32, u.shape, 0)
        c0 = carry_ref[0:1, :]
        c1 = carry_ref[1:2, :]
        u1 = jnp.where(row == 0, c1, pltpu.roll(u, 1, axis=0))
        u2 = jnp.where(row == 0, c0, jnp.where(row == 1, c1, pltpu.roll(u, 2, axis=0)))
        return u1, u2

    out, u = _conv_core(x_ref[...], h_ref[...], shifted, win_ref, cw_ref, wout_ref)
    o_ref[...] = out
    carry_ref[0:2, :] = u[tm - 2:tm, :]
    st_ref[...] = u[tm - SUBLANES:tm, :]


def _conv_sample_kernel(x_ref, h_ref, c0_ref, win_ref, cw_ref, wout_ref, o_ref, u_ref, *, ns, t_len):
    rows = ns * t_len
    c0 = jnp.broadcast_to(c0_ref[:, 0:1, :], (ns, t_len, D_MODEL)).reshape(rows, D_MODEL)
    c1 = jnp.broadcast_to(c0_ref[:, 1:2, :], (ns, t_len, D_MODEL)).reshape(rows, D_MODEL)

    def shifted(u):
        t = lax.broadcasted_iota(jnp.int32, u.shape, 0) & (t_len - 1)
        u1 = jnp.where(t == 0, c1, pltpu.roll(u, 1, axis=0))
        u2 = jnp.where(t == 0, c0, jnp.where(t == 1, c1, pltpu.roll(u, 2, axis=0)))
        return u1, u2

    out, u = _conv_core(x_ref[...], h_ref[...], shifted, win_ref, cw_ref, wout_ref)
    o_ref[...] = out
    u_ref[...] = u


def _conv_mix(x, hn, conv0, w_in, conv_w, w_out, bsz, t_len, t_valid, tm):
    m = bsz * t_len
    consts = [_const_spec((D_MODEL, 3 * D_MODEL)), _const_spec((CONV_W, D_MODEL)), _const_spec((D_MODEL, D_MODEL))]
    if t_len > tm:
        assert t_valid == t_len
        nt = t_len // tm
        row = pl.BlockSpec((tm, D_MODEL), lambda b, t: (b * nt + t, 0))
        out, tail = pl.pallas_call(
            functools.partial(_conv_prompt_kernel, tm=tm),
            grid=(bsz, nt),
            in_specs=[row, row, pl.BlockSpec((None, CONV_W - 1, D_MODEL), lambda b, t: (b, 0, 0))] + consts,
            out_specs=[row, pl.BlockSpec((None, SUBLANES, D_MODEL), lambda b, t: (b, 0, 0))],
            out_shape=[jax.ShapeDtypeStruct((m, D_MODEL), F32),
                       jax.ShapeDtypeStruct((bsz, SUBLANES, D_MODEL), F32)],
            scratch_shapes=[pltpu.VMEM((SUBLANES, D_MODEL), F32)],
            compiler_params=_params(("arbitrary", "arbitrary")),
            name="conv_prompt",
        )(x, hn, conv0, w_in, conv_w, w_out)
        return out, tail[:, SUBLANES - (CONV_W - 1):, :]
    ns = tm // t_len
    row = pl.BlockSpec((tm, D_MODEL), lambda i: (i, 0))
    out, u = pl.pallas_call(
        functools.partial(_conv_sample_kernel, ns=ns, t_len=t_len),
        grid=(m // tm,),
        in_specs=[row, row, pl.BlockSpec((ns, CONV_W - 1, D_MODEL), lambda i: (i, 0, 0))] + consts,
        out_specs=[row, row],
        out_shape=[jax.ShapeDtypeStruct((m, D_MODEL), F32)] * 2,
        compiler_params=_params(("parallel",)),
        name="conv_sample",
    )(x, hn, conv0, w_in, conv_w, w_out)
    return out, u.reshape(bsz, t_len, D_MODEL)[:, t_valid - (CONV_W - 1):t_valid, :]


def _trunk_head(x, wkv_in, shift_in, p, w00, bsz, t_len, t_valid, tm, cast=()):
    x, hn = _ffn(x, p['ffn_norm'][0, 0], w00, p['mix_norm'][0], tm)
    new_shift = hn.reshape(bsz, t_len, D_MODEL)[:, t_valid - 1]
    r, lw, k, v, kk, b, g = _rk_pre(hn, shift_in[0], p['rk_pre'], bsz, t_len, tm)
    y, new_wkv, cast_out = _wkv([r, lw, k, v, kk, b], wkv_in[0], bsz, t_len, t_valid, cast)
    return (x, y, r, k, v, g), new_wkv[None], new_shift[None], cast_out


def _trunk_tail(acts, conv_in, p, w_rest, bsz, t_len, t_valid, tm, want_y=True):
    x, y, r, k, v, g = acts
    norm = p['ffn_norm']
    x = _rk_post(x, y, r, k, v, g, p['rk_ln_w'][0], p['rk_ln_b'][0], p['rk_r_k'][0], p['rk_w_o'][0], tm)
    (x,) = _ffn(x, norm[0, 1], w_rest[0], norm[1, 0], tm, emit_n=False)
    x, hn = _ffn(x, norm[1, 0], w_rest[1], p['mix_norm'][1], tm, n_dtype=BF16)
    x, new_conv = _conv_mix(x, hn, conv_in[0], p['sc_w_in'][0], p['sc_conv_w'][0], p['sc_w_out'][0],
                            bsz, t_len, t_valid, tm)
    if not want_y:
        return None, new_conv[None]
    (y_out,) = _ffn(x, norm[1, 1], w_rest[2], p['final_norm'], tm, emit_x=False)
    return y_out, new_conv[None]


def kernel(x_prompt, x_sample, state_wkv, state_shift, state_conv, meta, ffn_norm, ffn_w_gu, ffn_w_down,
           mix_norm, final_norm, rk_mu, rk_w_rkv, rk_w0, rk_w1, rk_w2, rk_a0, rk_a1, rk_a2, rk_g1, rk_g2,
           rk_k_k, rk_k_a, rk_r_k, rk_ln_w, rk_ln_b, rk_w_o, sc_w_in, sc_conv_w, sc_w_out):
    assert DEPTH == 2 and rk_mu.shape[0] == 1 and sc_w_in.shape[0] == 1
    bf = lambda a: a.astype(BF16)
    vec = lambda a: a.reshape(1, D_MODEL)
    p = {
        'ffn_norm': ffn_norm, 'mix_norm': mix_norm, 'final_norm': final_norm,
        'rk_pre': [rk_mu[0], bf(rk_w_rkv[0, 0]), bf(rk_w_rkv[0, 1]), bf(rk_w_rkv[0, 2]), vec(rk_w0[0]),
                   bf(rk_w1[0]), bf(rk_w2[0]), vec(rk_a0[0]), bf(rk_a1[0]), bf(rk_a2[0]), bf(rk_g1[0]),
                   bf(rk_g2[0]), vec(rk_k_k[0]), vec(rk_k_a[0])],
        'rk_r_k': rk_r_k.reshape(1, D_MODEL), 'rk_ln_w': rk_ln_w, 'rk_ln_b': rk_ln_b, 'rk_w_o': bf(rk_w_o),
        'sc_w_in': bf(sc_w_in), 'sc_conv_w': sc_conv_w, 'sc_w_out': bf(sc_w_out),
    }
    w00 = (bf(ffn_w_gu[0, 0])[None], bf(ffn_w_down[0, 0])[None], 0)
    later = ((ffn_w_gu.reshape(2 * DEPTH * D_MODEL, 2 * D_FF), D_MODEL),
             (ffn_w_down.reshape(2 * DEPTH * D_FF, D_MODEL), D_FF))
    bs, t_s, _ = x_sample.shape
    bp, seq, _ = x_prompt.shape
    meta_rows = jnp.pad(meta.astype(x_prompt.dtype), ((0, WKV_CHUNK - N_META), (0, 0)))
    zero_wkv = jnp.zeros((1, 1, N_HEADS, HEAD_DIM, HEAD_DIM), state_wkv.dtype)
    zero_shift = jnp.zeros((1, 1, D_MODEL), state_shift.dtype)
    zero_conv = jnp.zeros((1, 1, CONV_W - 1, D_MODEL), state_conv.dtype)
    to_batch = lambda s: jnp.broadcast_to(s, (1, bp) + s.shape[2:])
    acts_m, wkv_m, shift_m, _ = _trunk_head(meta_rows, zero_wkv, zero_shift, p, w00, 1, WKV_CHUNK, N_META,
                                            WKV_CHUNK)
    acts_p, wkv_p, shift_p, (wgu_rest, wd_rest) = _trunk_head(
        x_prompt.reshape(bp * seq, D_MODEL), to_batch(wkv_m), to_batch(shift_m), p, w00, bp, seq, seq, 512,
        cast=later)
    acts_s, wkv_s, shift_s, _ = _trunk_head(x_sample.reshape(bs * t_s, D_MODEL), state_wkv, state_shift, p, w00,
                                            bs, t_s, t_s, 256)
    wgu_rest = wgu_rest.reshape(2 * DEPTH - 1, D_MODEL, 2 * D_FF)
    wd_rest = wd_rest.reshape(2 * DEPTH - 1, D_FF, D_MODEL)
    w_rest = [(wgu_rest, wd_rest, i) for i in range(2 * DEPTH - 1)]
    _, conv_m = _trunk_tail(acts_m, zero_conv, p, w_rest, 1, WKV_CHUNK, N_META, WKV_CHUNK, want_y=False)
    yp, conv_p = _trunk_tail(acts_p, to_batch(conv_m), p, w_rest, bp, seq, seq, 512)
    ys, conv_s = _trunk_tail(acts_s, state_conv, p, w_rest, bs, t_s, t_s, 256)
    return (yp.reshape(bp, seq, D_MODEL), ys.reshape(bs, t_s, D_MODEL), wkv_p, shift_p, conv_p,
            wkv_s, shift_s, conv_s)
```

```python
import functools
import math

import jax
import jax.numpy as jnp
from jax import lax
from jax.experimental import pallas as pl
from jax.experimental.pallas import tpu as pltpu

D_MODEL = 1024
HEAD_DIM = 64
N_HEADS = D_MODEL // HEAD_DIM
D_FF = 2816
N_META = 16
DEPTH = 2
CONV_W = 3
RMS_EPS = 1e-6
GN_EPS = 64e-5
DECAY_SCALE = math.exp(-0.5)

LANES = 128
SUBLANES = 8
BF16_ROWS = 16
MXU_DIM = 256
PAIR = 2 * HEAD_DIM
WKV_CHUNK = 64
N_PAIRS = D_MODEL // PAIR
VMEM_LIMIT = 56 * 1024 * 1024

FFN_CHUNKS = ((0, 6 * MXU_DIM), (6 * MXU_DIM, D_FF))

F32 = jnp.float32
BF16 = jnp.bfloat16


def _dot(a, b):
    return jnp.dot(a, b, preferred_element_type=F32)


def _dot_nt(a, b):
    return lax.dot_general(a, b, (((1,), (1,)), ((), ())), preferred_element_type=F32)


def _dot_tn(a, b):
    return lax.dot_general(a, b, (((0,), (0,)), ((), ())), preferred_element_type=F32)


def _rms(x, g):
    ms = jnp.mean(x * x, axis=-1, keepdims=True)
    return x * lax.rsqrt(ms + RMS_EPS) * g


def _sigmoid(x):
    return 1.0 / (1.0 + jnp.exp(-x))


def _head_ones():
    r = lax.broadcasted_iota(jnp.int32, (PAIR, PAIR), 0) >> 6
    c = lax.broadcasted_iota(jnp.int32, (PAIR, PAIR), 1) >> 6
    return jnp.where(r == c, 1.0, 0.0).astype(BF16)


def _head_sum(x, ones):
    xb = x.astype(BF16)
    return jnp.concatenate([_dot(xb[:, j * PAIR:(j + 1) * PAIR], ones) for j in range(N_PAIRS)], axis=1)


def _const_spec(shape):
    nd = len(shape)
    return pl.BlockSpec(shape, lambda *_: (0,) * nd, pipeline_mode=pl.Buffered(1))


def _params(sem):
    return pltpu.CompilerParams(dimension_semantics=sem, vmem_limit_bytes=VMEM_LIMIT)


def _ffn_kernel(x_ref, g_ref, wgu_ref, wd_ref, g2_ref, *out_refs, emit_x, emit_n):
    x = x_ref[...]
    xn = _rms(x, g_ref[...]).astype(BF16)
    acc = None
    for lo, hi in FFN_CHUNKS:
        gate = _dot(xn, wgu_ref[:, lo:hi])
        up = _dot(xn, wgu_ref[:, D_FF + lo:D_FF + hi])
        act = (gate * _sigmoid(gate) * up).astype(BF16)
        part = _dot(act, wd_ref[lo:hi, :])
        acc = part if acc is None else acc + part
    out = x + 0.5 * acc
    i = 0
    if emit_x:
        out_refs[i][...] = out
        i += 1
    if emit_n:
        out_refs[i][...] = _rms(out, g2_ref[...]).astype(out_refs[i].dtype)


def _ffn(x, g, w, g2, tm, emit_x=True, emit_n=True, n_dtype=F32):
    wgu, wd, idx = w
    m = x.shape[0]
    row = pl.BlockSpec((tm, D_MODEL), lambda i: (i, 0))
    n_out = int(emit_x) + int(emit_n)
    dtypes = [F32] * int(emit_x) + [n_dtype] * int(emit_n)

    def weight_spec(rows, cols):
        return pl.BlockSpec((None, rows, cols), lambda i: (idx, 0, 0), pipeline_mode=pl.Buffered(1))

    outs = pl.pallas_call(
        functools.partial(_ffn_kernel, emit_x=emit_x, emit_n=emit_n),
        grid=(m // tm,),
        in_specs=[row, _const_spec((1, D_MODEL)), weight_spec(D_MODEL, 2 * D_FF),
                  weight_spec(D_FF, D_MODEL), _const_spec((1, D_MODEL))],
        out_specs=[row] * n_out,
        out_shape=[jax.ShapeDtypeStruct((m, D_MODEL), dt) for dt in dtypes],
        compiler_params=_params(("parallel",)),
        name="ffn",
    )(x, g.reshape(1, D_MODEL), wgu, wd, g2.reshape(1, D_MODEL))
    return tuple(outs)


N_RK_IN = 15
N_RK_OUT = 8


def _head_select(transpose=False):
    shape = (PAIR, D_MODEL) if transpose else (D_MODEL, PAIR)
    chan = lax.broadcasted_iota(jnp.int32, shape, 1 if transpose else 0) >> 6
    head = lax.broadcasted_iota(jnp.int32, shape, 0 if transpose else 1)
    return jnp.where(chan == head, 1.0, 0.0).astype(BF16)


def _rk_pre_core(h, prev, p, out_refs):
    (mu, wr, wk, wv, w0, w1, w2, a0, a1, a2, g1, g2, k_k, k_a, r_k) = [r[...] for r in p]
    xx = prev - h
    xr, xw, xk, xv, xa, xg = [(h + xx * mu[c:c + 1, :]).astype(BF16) for c in range(6)]
    r = _dot(xr, wr)
    k = _dot(xk, wk)
    v = _dot(xv, wv)
    z = w0 + _dot(jnp.tanh(_dot(xw, w1)).astype(BF16), w2)
    lw = -DECAY_SCALE * _sigmoid(z)
    a = _sigmoid(a0 + _dot(_dot(xa, a1).astype(BF16), a2))
    g = _dot(_sigmoid(_dot(xg, g1)).astype(BF16), g2)
    kk = k * k_k
    ss = _head_sum(kk * kk, _head_ones())
    kk = kk * lax.rsqrt(jnp.maximum(ss, 1e-24))
    kmod = k * (1.0 + (a - 1.0) * k_a)
    bonus = _dot((r * kmod * r_k).astype(BF16), _head_select())
    for ref, val in zip(out_refs, (r, lw, kmod, v, kk, kk * a, g, bonus)):
        ref[...] = val.astype(ref.dtype)


def _rk_pre_prompt_kernel(h_ref, s0_ref, *refs, tm):
    p, out_refs, carry_ref = refs[:N_RK_IN], refs[N_RK_IN:N_RK_IN + N_RK_OUT], refs[N_RK_IN + N_RK_OUT]

    @pl.when(pl.program_id(1) == 0)
    def _():
        carry_ref[0:1, :] = s0_ref[...]

    h = h_ref[...]
    row = lax.broadcasted_iota(jnp.int32, h.shape, 0)
    prev = jnp.where(row == 0, carry_ref[0:1, :], pltpu.roll(h, 1, axis=0))
    carry_ref[0:1, :] = h[tm - 1:tm, :]
    _rk_pre_core(h, prev, p, out_refs)


def _rk_pre_sample_kernel(h_ref, s0_ref, *refs, ns, t_len):
    p, out_refs = refs[:N_RK_IN], refs[N_RK_IN:N_RK_IN + N_RK_OUT]
    h = h_ref[...]
    rows = ns * t_len
    s0 = jnp.broadcast_to(s0_ref[...], (ns, t_len, D_MODEL)).reshape(rows, D_MODEL)
    t = lax.broadcasted_iota(jnp.int32, h.shape, 0) & (t_len - 1)
    prev = jnp.where(t == 0, s0, pltpu.roll(h, 1, axis=0))
    _rk_pre_core(h, prev, p, out_refs)


def _rk_param_specs(p):
    return [_const_spec(a.shape) for a in p]


def _rk_pre(hn, shift0, p, bsz, t_len, tm):
    m = bsz * t_len
    assert len(p) == N_RK_IN
    out_shape = ([jax.ShapeDtypeStruct((m, D_MODEL), F32 if i == 1 else BF16) for i in range(N_RK_OUT - 1)]
                 + [jax.ShapeDtypeStruct((m, PAIR), F32)])
    s0 = shift0.reshape(bsz, 1, D_MODEL)
    if t_len > tm:
        nt = t_len // tm
        row = pl.BlockSpec((tm, D_MODEL), lambda b, t: (b * nt + t, 0))
        return pl.pallas_call(
            functools.partial(_rk_pre_prompt_kernel, tm=tm),
            grid=(bsz, nt),
            in_specs=[row, pl.BlockSpec((None, 1, D_MODEL), lambda b, t: (b, 0, 0))] + _rk_param_specs(p),
            out_specs=[row] * (N_RK_OUT - 1) + [pl.BlockSpec((tm, PAIR), lambda b, t: (b * nt + t, 0))],
            out_shape=out_shape,
            scratch_shapes=[pltpu.VMEM((SUBLANES, D_MODEL), F32)],
            compiler_params=_params(("arbitrary", "arbitrary")),
            name="rk_pre_prompt",
        )(hn, s0, *p)
    ns = tm // t_len
    row = pl.BlockSpec((tm, D_MODEL), lambda i: (i, 0))
    return pl.pallas_call(
        functools.partial(_rk_pre_sample_kernel, ns=ns, t_len=t_len),
        grid=(m // tm,),
        in_specs=[row, pl.BlockSpec((ns, 1, D_MODEL), lambda i: (i, 0, 0))] + _rk_param_specs(p),
        out_specs=[row] * (N_RK_OUT - 1) + [pl.BlockSpec((tm, PAIR), lambda i: (i, 0))],
        out_shape=out_shape,
        compiler_params=_params(("parallel",)),
        name="rk_pre_sample",
    )(hn, s0, *p)


def _wkv_masks(c):
    rows = 128
    sh = c.bit_length() - 1
    row = lax.broadcasted_iota(jnp.int32, (rows, rows), 0)
    col = lax.broadcasted_iota(jnp.int32, (rows, rows), 1)
    same = (row >> sh) == (col >> sh)
    strict = same & (row > col)
    incl = same & (row >= col)
    base = (row >> 1) == (col >> 1)
    merges = []
    s = 2
    while s < c:
        b = s.bit_length() - 1
        merges.append(((row >> (b + 1)) == (col >> (b + 1))) & (((row >> b) & 1) == 1) & (((col >> b) & 1) == 0))
        s *= 2
    eye = jnp.where(row == col, 1.0, 0.0).astype(F32)
    return strict, incl, base, merges, eye


def _cumsum_rows(x, c):
    t = lax.broadcasted_iota(jnp.int32, x.shape, 0)
    s = 1
    while s < c:
        x = x + jnp.where(t >= s, pltpu.roll(x, s, axis=0), 0.0)
        s *= 2
    return x


def _wkv_units(ins, states, c, masks):
    strict, incl, base, merges, eye = masks
    nseq = len(ins[0])
    rows = 128
    assert nseq * 2 * c == rows
    lane_h = lax.broadcasted_iota(jnp.int32, (c, PAIR), 1) >> 6
    seq_rows = [slice(i * 2 * c, (i + 1) * 2 * c) for i in range(nseq)]

    def stack(xs):
        return jnp.concatenate([jnp.where(lane_h == h, x, 0.0) for x in xs for h in range(2)],
                               axis=0).astype(BF16)

    def each(f, *lists):
        return [f(*xs) for xs in zip(*lists)]

    def prep(seqs):
        per = []
        for r, lw, k, v, kk, b in seqs:
            cum = _cumsum_rows(lw, c)
            last = cum[c - 1:c, :]
            to_end = jnp.exp(last - cum)
            p_inv = jnp.exp(-cum)
            per.append(dict(p_c=jnp.exp(last), kk=kk * jnp.exp(cum - lw), r=r * jnp.exp(cum), k=k * p_inv,
                            b=b * p_inv, v=v, kd=k * to_end, nbd=-(b * to_end)))
        out = {name: stack([d[name] for d in per]) for name in ('kk', 'r', 'k', 'b', 'v', 'kd', 'nbd')}
        out['p_c'] = [d['p_c'] for d in per]
        return out

    o = [prep(x) for x in ins]
    kb = [jnp.concatenate([x['k'], x['b']], axis=0) for x in o]
    a_kk = each(lambda x, w_: _dot_nt(x['kk'], w_), o, kb)
    a_r = each(lambda x, w_: _dot_nt(x['r'], w_), o, kb)
    a_kk_k, a_kk_b = [a[:, :rows] for a in a_kk], [a[:, rows:] for a in a_kk]
    a_r_k, a_r_b = [a[:, :rows] for a in a_r], [a[:, rows:] for a in a_r]

    lm = [jnp.where(strict, a, 0.0) for a in a_kk_b]
    inv = [eye - jnp.where(base, l, 0.0) for l in lm]
    size = 2
    for m in merges:
        if size >= SUBLANES:
            starts = range(size, rows, 2 * size)

            def gather(x, starts=starts, size=size):
                return jnp.concatenate([x[r0:r0 + size] for r0 in starts], axis=0)

            def scatter(xh, n=len(starts), size=size):
                zero = jnp.zeros((size, rows), F32)
                return jnp.concatenate([p_ for j in range(n) for p_ in (zero, xh[j * size:(j + 1) * size])],
                                       axis=0)

            inv_b = [t.astype(BF16) for t in inv]
            mid = each(lambda l, t: scatter(_dot(gather(jnp.where(m, l, 0.0)).astype(BF16), t)).astype(BF16),
                       lm, inv_b)
            inv = each(lambda t, md: t - scatter(_dot(gather(t).astype(BF16), md)), inv, mid)
        else:
            inv_b = [t.astype(BF16) for t in inv]
            mid = each(lambda l, t: _dot(jnp.where(m, l, 0.0).astype(BF16), t).astype(BF16), lm, inv_b)
            inv = each(lambda t, tb, md: t - _dot(tb, md), inv, inv_b, mid)
        size *= 2

    def state_dot(x, s_list):
        outs = [_dot_nt(jnp.concatenate([x['kk'][rs], x['r'][rs]], axis=0), s.astype(BF16))
                for rs, s in zip(seq_rows, s_list)]
        h = 2 * c
        if nseq == 1:
            return outs[0][:h], outs[0][h:]
        return (jnp.concatenate([t[:h] for t in outs], axis=0), jnp.concatenate([t[h:] for t in outs], axis=0))

    sd = each(state_dot, o, states)
    kkh, rh = [t[0] for t in sd], [t[1] for t in sd]
    rhs = each(lambda h, a, x: h + _dot(jnp.where(strict, a, 0.0).astype(BF16), x['v']), kkh, a_kk_k, o)
    u_b = each(lambda t, z: _dot(t.astype(BF16), z.astype(BF16)).astype(BF16), inv, rhs)
    y_s = each(lambda h, ak, ab, x, u: h + _dot(jnp.where(incl, ak, 0.0).astype(BF16), x['v'])
               - _dot(jnp.where(incl, ab, 0.0).astype(BF16), u), rh, a_r_k, a_r_b, o, u_b)

    def fold(ys):
        return [ys[rs][:c] + ys[rs][c:] for rs in seq_rows]

    def new_state(x, u, s_list):
        return [s * pc + _dot_tn(jnp.concatenate([x['v'][rs], u[rs]], axis=0),
                                 jnp.concatenate([x['kd'][rs], x['nbd'][rs]], axis=0))
                for rs, s, pc in zip(seq_rows, s_list, x['p_c'])]

    return [fold(ys) for ys in y_s], each(new_state, o, u_b, states)


def _rows_to_blockdiag(s):
    ext = jnp.concatenate([s, jnp.zeros_like(s)], axis=1)
    row = lax.broadcasted_iota(jnp.int32, ext.shape, 0)
    return jnp.where(row < HEAD_DIM, ext, pltpu.roll(ext, HEAD_DIM, axis=1))


def _blockdiag_to_rows(s_bd):
    row = lax.broadcasted_iota(jnp.int32, s_bd.shape, 0)
    return jnp.where(row < HEAD_DIM, s_bd, pltpu.roll(s_bd, HEAD_DIM, axis=1))[:, :HEAD_DIM]


def _wkv_prompt_kernel(r_ref, lw_ref, k_ref, v_ref, kk_ref, b_ref, s0_ref, *refs, c, t_len, nb, n_cast):
    cast_in, (y_ref, sout_ref), cast_out, s_ref = (refs[:n_cast], refs[n_cast:n_cast + 2],
                                                   refs[n_cast + 2:2 * n_cast + 2], refs[2 * n_cast + 2])
    for w_in, w_out in zip(cast_in, cast_out):
        w_out[...] = w_in[...].astype(BF16)
    ci = pl.program_id(1)
    units = [(i, q) for i in range(nb) for q in range(N_PAIRS)]

    @pl.when(ci == 0)
    def _():
        for i, q in units:
            s_ref[i, q] = _rows_to_blockdiag(s0_ref[i, q])

    masks = _wkv_masks(c)
    rows = lax.broadcasted_iota(jnp.int32, (c, PAIR), 0) + ci * c
    valid = rows < t_len
    lanes = [slice(q * PAIR, (q + 1) * PAIR) for q in range(N_PAIRS)]
    ins = [[[jnp.where(valid, ref[i, :, lanes[q]].astype(F32), 0.0)
             for ref in (r_ref, lw_ref, k_ref, v_ref, kk_ref, b_ref)]] for i, q in units]
    ys, s_new = _wkv_units(ins, [[s_ref[i, q]] for i, q in units], c, masks)
    for (i, q), y, s in zip(units, ys, s_new):
        y_ref[i, :, lanes[q]] = y[0].astype(y_ref.dtype)
        s_ref[i, q] = s[0]

    @pl.when(ci == pl.num_programs(1) - 1)
    def _():
        for i, q in units:
            sout_ref[i, q] = _blockdiag_to_rows(s_ref[i, q])


def _wkv_sample_kernel(r_ref, lw_ref, k_ref, v_ref, kk_ref, b_ref, s0_ref, y_ref, sout_ref, *, c, nseq):
    masks = _wkv_masks(c)
    lanes = [slice(q * PAIR, (q + 1) * PAIR) for q in range(N_PAIRS)]
    full = [ref[...].astype(F32) for ref in (r_ref, lw_ref, k_ref, v_ref, kk_ref, b_ref)]
    ins = [[[a[i * c:(i + 1) * c, lanes[q]] for a in full] for i in range(nseq)] for q in range(N_PAIRS)]
    states = [[_rows_to_blockdiag(s0_ref[i, q]) for i in range(nseq)] for q in range(N_PAIRS)]
    ys, s_new = _wkv_units(ins, states, c, masks)
    for q in range(N_PAIRS):
        for i in range(nseq):
            y_ref[i * c:(i + 1) * c, lanes[q]] = ys[q][i]
            sout_ref[i, q] = _blockdiag_to_rows(s_new[q][i])


def _cast_slabs(w, skip, n_steps, nc):
    n_rows = w.shape[0] - skip
    slab = next(r for r in range(BF16_ROWS, n_rows + 1, BF16_ROWS)
                if n_rows % r == 0 and skip % r == 0 and n_rows // r <= n_steps)
    last = n_rows // slab - 1
    first = skip // slab
    in_spec = pl.BlockSpec((slab, w.shape[1]), lambda b, i: (first + jnp.minimum(b * nc + i, last), 0))
    out_spec = pl.BlockSpec((slab, w.shape[1]), lambda b, i: (jnp.minimum(b * nc + i, last), 0))
    return in_spec, out_spec, jax.ShapeDtypeStruct((n_rows, w.shape[1]), BF16)


def _wkv(ins, state, bsz, t_len, t_valid, cast=()):
    m = bsz * t_len
    y_shape = jax.ShapeDtypeStruct((m, D_MODEL), F32)
    s_rows = state.reshape(bsz, N_PAIRS, PAIR, HEAD_DIM)
    s_shape = jax.ShapeDtypeStruct(s_rows.shape, F32)
    if t_len >= WKV_CHUNK:
        c, nb = WKV_CHUNK, min(bsz, 2)
        nc = t_len // c
        ins3 = [a.reshape(bsz, t_len, D_MODEL) for a in ins]
        blk = pl.BlockSpec((nb, c, D_MODEL), lambda b, i: (b, i, 0))
        sblk = pl.BlockSpec((nb, N_PAIRS, PAIR, HEAD_DIM), lambda b, i: (b, 0, 0, 0))
        slabs = [_cast_slabs(w, skip, (bsz // nb) * nc, nc) for w, skip in cast]
        y, s_out, *cast_out = pl.pallas_call(
            functools.partial(_wkv_prompt_kernel, c=c, t_len=t_valid, nb=nb, n_cast=len(cast)),
            grid=(bsz // nb, nc),
            in_specs=[blk] * 6 + [sblk] + [s[0] for s in slabs],
            out_specs=[blk, sblk] + [s[1] for s in slabs],
            out_shape=[jax.ShapeDtypeStruct((bsz, t_len, D_MODEL), BF16), s_shape] + [s[2] for s in slabs],
            scratch_shapes=[pltpu.VMEM((nb, N_PAIRS, PAIR, PAIR), F32)],
            compiler_params=_params(("arbitrary", "arbitrary")),
            name="wkv_prompt",
        )(*ins3, s_rows, *[w for w, _ in cast])
        return y.reshape(m, D_MODEL), s_out.reshape(state.shape), cast_out
    assert t_valid == t_len and not cast
    c, nseq = t_len, WKV_CHUNK // t_len
    blk = pl.BlockSpec((nseq * c, D_MODEL), lambda i: (i, 0))
    sblk = pl.BlockSpec((nseq, N_PAIRS, PAIR, HEAD_DIM), lambda i: (i, 0, 0, 0))
    y, s_out = pl.pallas_call(
        functools.partial(_wkv_sample_kernel, c=c, nseq=nseq),
        grid=(bsz // nseq,),
        in_specs=[blk] * 6 + [sblk],
        out_specs=[blk, sblk],
        out_shape=[y_shape, s_shape],
        compiler_params=_params(("parallel",)),
        name="wkv_sample",
    )(*ins, s_rows)
    return y, s_out.reshape(state.shape), []


def _rk_post_kernel(x_ref, y_ref, bonus_ref, v_ref, g_ref, lnw_ref, lnb_ref, wo_ref, o_ref):
    ones = _head_ones()
    y = y_ref[...].astype(F32)
    inv_n = 1.0 / HEAD_DIM
    mean = _head_sum(y, ones) * inv_n
    yc = y - mean
    var = _head_sum(yc * yc, ones) * inv_n
    yn = yc * lax.rsqrt(var + GN_EPS) * lnw_ref[...] + lnb_ref[...]
    coef = bonus_ref[...]
    hi = coef.astype(BF16)
    lo = (coef - hi.astype(F32)).astype(BF16)
    spread = _head_select(transpose=True)
    bonus = (_dot(hi, spread) + _dot(lo, spread)) * v_ref[...].astype(F32)
    o = ((yn + bonus) * g_ref[...].astype(F32)).astype(BF16)
    o_ref[...] = x_ref[...] + _dot(o, wo_ref[...])


def _rk_post(x, y, bonus, v, g, ln_w, ln_b, w_o, tm):
    m = x.shape[0]
    row = pl.BlockSpec((tm, D_MODEL), lambda i: (i, 0))
    vec = _const_spec((1, D_MODEL))
    return pl.pallas_call(
        _rk_post_kernel,
        grid=(m // tm,),
        in_specs=[row, row, pl.BlockSpec((tm, PAIR), lambda i: (i, 0)), row, row, vec, vec,
                  _const_spec((D_MODEL, D_MODEL))],
        out_specs=row,
        out_shape=jax.ShapeDtypeStruct((m, D_MODEL), F32),
        compiler_params=_params(("parallel",)),
        name="rk_post",
    )(x, y, bonus, v, g, ln_w.reshape(1, D_MODEL), ln_b.reshape(1, D_MODEL), w_o)


def _conv_core(x, h, shifted, win_ref, cw_ref, wout_ref):
    z = _dot(h.astype(BF16), win_ref[...])
    gate_b = z[:, 0:D_MODEL]
    u = z[:, D_MODEL:2 * D_MODEL] * z[:, 2 * D_MODEL:3 * D_MODEL]
    cw = cw_ref[...]
    u1, u2 = shifted(u)
    conv = u2 * cw[0:1, :] + u1 * cw[1:2, :] + u * cw[2:3, :]
    out = x + _dot((gate_b * conv).astype(BF16), wout_ref[...])
    return out, u


def _conv_prompt_kernel(x_ref, h_ref, c0_ref, win_ref, cw_ref, wout_ref, o_ref, st_ref, carry_ref, *, tm):
    @pl.when(pl.program_id(1) == 0)
    def _():
        carry_ref[0:2, :] = c0_ref[...]

    def shifted(u):
        row = lax.broadcasted_iota(jnp.int32, u.shape, 0)
        c0 = carry_ref[0:1, :]
        c1 = carry_ref[1:2, :]
        u1 = jnp.where(row == 0, c1, pltpu.roll(u, 1, axis=0))
        u2 = jnp.where(row == 0, c0, jnp.where(row == 1, c1, pltpu.roll(u, 2, axis=0)))
        return u1, u2

    out, u = _conv_core(x_ref[...], h_ref[...], shifted, win_ref, cw_ref, wout_ref)
    o_ref[...] = out
    carry_ref[0:2, :] = u[tm - 2:tm, :]
    st_ref[...] = u[tm - SUBLANES:tm, :]


def _conv_sample_kernel(x_ref, h_ref, c0_ref, win_ref, cw_ref, wout_ref, o_ref, u_ref, *, ns, t_len):
    rows = ns * t_len
    c0 = jnp.broadcast_to(c0_ref[:, 0:1, :], (ns, t_len, D_MODEL)).reshape(rows, D_MODEL)
    c1 = jnp.broadcast_to(c0_ref[:, 1:2, :], (ns, t_len, D_MODEL)).reshape(rows, D_MODEL)

    def shifted(u):
        t = lax.broadcasted_iota(jnp.int32, u.shape, 0) & (t_len - 1)
        u1 = jnp.where(t == 0, c1, pltpu.roll(u, 1, axis=0))
        u2 = jnp.where(t == 0, c0, jnp.where(t == 1, c1, pltpu.roll(u, 2, axis=0)))
        return u1, u2

    out, u = _conv_core(x_ref[...], h_ref[...], shifted, win_ref, cw_ref, wout_ref)
    o_ref[...] = out
    u_ref[...] = u


def _conv_mix(x, hn, conv0, w_in, conv_w, w_out, bsz, t_len, t_valid, tm):
    m = bsz * t_len
    consts = [_const_spec((D_MODEL, 3 * D_MODEL)), _const_spec((CONV_W, D_MODEL)), _const_spec((D_MODEL, D_MODEL))]
    if t_len > tm:
        assert t_valid == t_len
        nt = t_len // tm
        row = pl.BlockSpec((tm, D_MODEL), lambda b, t: (b * nt + t, 0))
        out, tail = pl.pallas_call(
            functools.partial(_conv_prompt_kernel, tm=tm),
            grid=(bsz, nt),
            in_specs=[row, row, pl.BlockSpec((None, CONV_W - 1, D_MODEL), lambda b, t: (b, 0, 0))] + consts,
            out_specs=[row, pl.BlockSpec((None, SUBLANES, D_MODEL), lambda b, t: (b, 0, 0))],
            out_shape=[jax.ShapeDtypeStruct((m, D_MODEL), F32),
                       jax.ShapeDtypeStruct((bsz, SUBLANES, D_MODEL), F32)],
            scratch_shapes=[pltpu.VMEM((SUBLANES, D_MODEL), F32)],
            compiler_params=_params(("arbitrary", "arbitrary")),
            name="conv_prompt",
        )(x, hn, conv0, w_in, conv_w, w_out)
        return out, tail[:, SUBLANES - (CONV_W - 1):, :]
    ns = tm // t_len
    row = pl.BlockSpec((tm, D_MODEL), lambda i: (i, 0))
    out, u = pl.pallas_call(
        functools.partial(_conv_sample_kernel, ns=ns, t_len=t_len),
        grid=(m // tm,),
        in_specs=[row, row, pl.BlockSpec((ns, CONV_W - 1, D_MODEL), lambda i: (i, 0, 0))] + consts,
        out_specs=[row, row],
        out_shape=[jax.ShapeDtypeStruct((m, D_MODEL), F32)] * 2,
        compiler_params=_params(("parallel",)),
        name="conv_sample",
    )(x, hn, conv0, w_in, conv_w, w_out)
    return out, u.reshape(bsz, t_len, D_MODEL)[:, t_valid - (CONV_W - 1):t_valid, :]


def _trunk_head(x, wkv_in, shift_in, p, w00, bsz, t_len, t_valid, tm, cast=()):
    x, hn = _ffn(x, p['ffn_norm'][0, 0], w00, p['mix_norm'][0], tm)
    new_shift = hn.reshape(bsz, t_len, D_MODEL)[:, t_valid - 1]
    r, lw, k, v, kk, b, g, bonus = _rk_pre(hn, shift_in[0], p['rk_pre'], bsz, t_len, tm)
    y, new_wkv, cast_out = _wkv([r, lw, k, v, kk, b], wkv_in[0], bsz, t_len, t_valid, cast)
    return (x, y, bonus, v, g), new_wkv[None], new_shift[None], cast_out


def _trunk_tail(acts, conv_in, p, w_rest, bsz, t_len, t_valid, tm, want_y=True):
    x, y, bonus, v, g = acts
    norm = p['ffn_norm']
    x = _rk_post(x, y, bonus, v, g, p['rk_ln_w'][0], p['rk_ln_b'][0], p['rk_w_o'][0], tm)
    (x,) = _ffn(x, norm[0, 1], w_rest[0], norm[1, 0], tm, emit_n=False)
    x, hn = _ffn(x, norm[1, 0], w_rest[1], p['mix_norm'][1], tm, n_dtype=BF16)
    x, new_conv = _conv_mix(x, hn, conv_in[0], p['sc_w_in'][0], p['sc_conv_w'][0], p['sc_w_out'][0],
                            bsz, t_len, t_valid, tm)
    if not want_y:
        return None, new_conv[None]
    (y_out,) = _ffn(x, norm[1, 1], w_rest[2], p['final_norm'], tm, emit_x=False)
    return y_out, new_conv[None]


def kernel(x_prompt, x_sample, state_wkv, state_shift, state_conv, meta, ffn_norm, ffn_w_gu, ffn_w_down,
           mix_norm, final_norm, rk_mu, rk_w_rkv, rk_w0, rk_w1, rk_w2, rk_a0, rk_a1, rk_a2, rk_g1, rk_g2,
           rk_k_k, rk_k_a, rk_r_k, rk_ln_w, rk_ln_b, rk_w_o, sc_w_in, sc_conv_w, sc_w_out):
    assert DEPTH == 2 and rk_mu.shape[0] == 1 and sc_w_in.shape[0] == 1
    bf = lambda a: a.astype(BF16)
    vec = lambda a: a.reshape(1, D_MODEL)
    p = {
        'ffn_norm': ffn_norm, 'mix_norm': mix_norm, 'final_norm': final_norm,
        'rk_pre': [rk_mu[0], bf(rk_w_rkv[0, 0]), bf(rk_w_rkv[0, 1]), bf(rk_w_rkv[0, 2]), vec(rk_w0[0]),
                   bf(rk_w1[0]), bf(rk_w2[0]), vec(rk_a0[0]), bf(rk_a1[0]), bf(rk_a2[0]), bf(rk_g1[0]),
                   bf(rk_g2[0]), vec(rk_k_k[0]), vec(rk_k_a[0]), rk_r_k.reshape(1, D_MODEL)],
        'rk_ln_w': rk_ln_w, 'rk_ln_b': rk_ln_b, 'rk_w_o': bf(rk_w_o),
        'sc_w_in': bf(sc_w_in), 'sc_conv_w': sc_conv_w, 'sc_w_out': bf(sc_w_out),
    }
    w00 = (bf(ffn_w_gu[0, 0])[None], bf(ffn_w_down[0, 0])[None], 0)
    later = ((ffn_w_gu.reshape(2 * DEPTH * D_MODEL, 2 * D_FF), D_MODEL),
             (ffn_w_down.reshape(2 * DEPTH * D_FF, D_MODEL), D_FF))
    bs, t_s, _ = x_sample.shape
    bp, seq, _ = x_prompt.shape
    meta_rows = jnp.pad(meta.astype(x_prompt.dtype), ((0, WKV_CHUNK - N_META), (0, 0)))
    zero_wkv = jnp.zeros((1, 1, N_HEADS, HEAD_DIM, HEAD_DIM), state_wkv.dtype)
    zero_shift = jnp.zeros((1, 1, D_MODEL), state_shift.dtype)
    zero_conv = jnp.zeros((1, 1, CONV_W - 1, D_MODEL), state_conv.dtype)
    to_batch = lambda s: jnp.broadcast_to(s, (1, bp) + s.shape[2:])
    acts_m, wkv_m, shift_m, _ = _trunk_head(meta_rows, zero_wkv, zero_shift, p, w00, 1, WKV_CHUNK, N_META,
                                            WKV_CHUNK)
    acts_p, wkv_p, shift_p, (wgu_rest, wd_rest) = _trunk_head(
        x_prompt.reshape(bp * seq, D_MODEL), to_batch(wkv_m), to_batch(shift_m), p, w00, bp, seq, seq, 512,
        cast=later)
    acts_s, wkv_s, shift_s, _ = _trunk_head(x_sample.reshape(bs * t_s, D_MODEL), state_wkv, state_shift, p, w00,
                                            bs, t_s, t_s, 256)
    wgu_rest = wgu_rest.reshape(2 * DEPTH - 1, D_MODEL, 2 * D_FF)
    wd_rest = wd_rest.reshape(2 * DEPTH - 1, D_FF, D_MODEL)
    w_rest = [(wgu_rest, wd_rest, i) for i in range(2 * DEPTH - 1)]
    _, conv_m = _trunk_tail(acts_m, zero_conv, p, w_rest, 1, WKV_CHUNK, N_META, WKV_CHUNK, want_y=False)
    yp, conv_p = _trunk_tail(acts_p, to_batch(conv_m), p, w_rest, bp, seq, seq, 512)
    ys, conv_s = _trunk_tail(acts_s, state_conv, p, w_rest, bs, t_s, t_s, 256)
    return (yp.reshape(bp, seq, D_MODEL), ys.reshape(bs, t_s, D_MODEL), wkv_p, shift_p, conv_p,
            wkv_s, shift_s, conv_s)
```

```python
import functools
import math

import jax
import jax.numpy as jnp
from jax import lax
from jax.experimental import pallas as pl
from jax.experimental.pallas import tpu as pltpu

D_MODEL = 1024
HEAD_DIM = 64
N_HEADS = D_MODEL // HEAD_DIM
D_FF = 2816
N_META = 16
DEPTH = 2
CONV_W = 3
RMS_EPS = 1e-6
GN_EPS = 64e-5
DECAY_SCALE = math.exp(-0.5)

SUBLANES = 8
BF16_ROWS = 16
MXU_DIM = 256
PAIR = 2 * HEAD_DIM
HEAD_SHIFT = HEAD_DIM.bit_length() - 1
KK_FLOOR = 1e-12
WKV_CHUNK = 64
STACK_ROWS = 2 * WKV_CHUNK
N_PAIRS = D_MODEL // PAIR
VMEM_LIMIT = 56 * 1024 * 1024
ROW_TILE_LONG = 512
ROW_TILE_SHORT = 256

FFN_CHUNKS = ((0, 6 * MXU_DIM), (6 * MXU_DIM, D_FF))

F32 = jnp.float32
BF16 = jnp.bfloat16


def _dot(a, b):
    return jnp.dot(a, b, preferred_element_type=F32)


def _dot_nt(a, b):
    return lax.dot_general(a, b, (((1,), (1,)), ((), ())), preferred_element_type=F32)


def _dot_tn(a, b):
    return lax.dot_general(a, b, (((0,), (0,)), ((), ())), preferred_element_type=F32)


def _rms(x, g):
    ms = jnp.mean(x * x, axis=-1, keepdims=True)
    return x * lax.rsqrt(ms + RMS_EPS) * g


def _sigmoid(x):
    return 1.0 / (1.0 + jnp.exp(-x))


def _head_ones():
    r = lax.broadcasted_iota(jnp.int32, (PAIR, PAIR), 0) >> HEAD_SHIFT
    c = lax.broadcasted_iota(jnp.int32, (PAIR, PAIR), 1) >> HEAD_SHIFT
    return jnp.where(r == c, 1.0, 0.0).astype(BF16)


def _head_sum(x, ones):
    xb = x.astype(BF16)
    return jnp.concatenate([_dot(xb[:, j * PAIR:(j + 1) * PAIR], ones) for j in range(N_PAIRS)], axis=1)


def _const_spec(shape):
    nd = len(shape)
    return pl.BlockSpec(shape, lambda *_: (0,) * nd, pipeline_mode=pl.Buffered(1))


def _params(sem):
    return pltpu.CompilerParams(dimension_semantics=sem, vmem_limit_bytes=VMEM_LIMIT)


def _ffn_kernel(x_ref, g_ref, wgu_ref, wd_ref, g2_ref, *out_refs, emit_x, emit_n):
    x = x_ref[...]
    xn = _rms(x, g_ref[...]).astype(BF16)
    acc = None
    for lo, hi in FFN_CHUNKS:
        gate = _dot(xn, wgu_ref[:, lo:hi])
        up = _dot(xn, wgu_ref[:, D_FF + lo:D_FF + hi])
        act = (gate * _sigmoid(gate) * up).astype(BF16)
        part = _dot(act, wd_ref[lo:hi, :])
        acc = part if acc is None else acc + part
    out = x + 0.5 * acc
    i = 0
    if emit_x:
        out_refs[i][...] = out
        i += 1
    if emit_n:
        out_refs[i][...] = _rms(out, g2_ref[...]).astype(out_refs[i].dtype)


def _ffn(x, g, w, g2, tm, emit_x=True, emit_n=True, n_dtype=F32):
    wgu, wd, idx = w
    m = x.shape[0]
    row = pl.BlockSpec((tm, D_MODEL), lambda i: (i, 0))
    n_out = int(emit_x) + int(emit_n)
    dtypes = [F32] * int(emit_x) + [n_dtype] * int(emit_n)

    def weight_spec(rows, cols):
        return pl.BlockSpec((None, rows, cols), lambda i: (idx, 0, 0), pipeline_mode=pl.Buffered(1))

    outs = pl.pallas_call(
        functools.partial(_ffn_kernel, emit_x=emit_x, emit_n=emit_n),
        grid=(m // tm,),
        in_specs=[row, _const_spec((1, D_MODEL)), weight_spec(D_MODEL, 2 * D_FF),
                  weight_spec(D_FF, D_MODEL), _const_spec((1, D_MODEL))],
        out_specs=[row] * n_out,
        out_shape=[jax.ShapeDtypeStruct((m, D_MODEL), dt) for dt in dtypes],
        compiler_params=_params(("parallel",)),
        name="ffn",
    )(x, g.reshape(1, D_MODEL), wgu, wd, g2.reshape(1, D_MODEL))
    return tuple(outs)


N_RK_IN = 15
N_RK_OUT = 8


def _head_select(transpose=False):
    shape = (PAIR, D_MODEL) if transpose else (D_MODEL, PAIR)
    chan = lax.broadcasted_iota(jnp.int32, shape, 1 if transpose else 0) >> HEAD_SHIFT
    head = lax.broadcasted_iota(jnp.int32, shape, 0 if transpose else 1)
    return jnp.where(chan == head, 1.0, 0.0).astype(BF16)


def _rk_pre_core(h, prev, p, out_refs):
    (mu, wr, wk, wv, w0, w1, w2, a0, a1, a2, g1, g2, k_k, k_a, r_k) = [r[...] for r in p]
    xx = prev - h
    xr, xw, xk, xv, xa, xg = [(h + xx * mu[c:c + 1, :]).astype(BF16) for c in range(6)]
    r = _dot(xr, wr)
    k = _dot(xk, wk)
    v = _dot(xv, wv)
    z = w0 + _dot(jnp.tanh(_dot(xw, w1)).astype(BF16), w2)
    lw = -DECAY_SCALE * _sigmoid(z)
    a = _sigmoid(a0 + _dot(_dot(xa, a1).astype(BF16), a2))
    g = _dot(_sigmoid(_dot(xg, g1)).astype(BF16), g2)
    kk = k * k_k
    ss = _head_sum(kk * kk, _head_ones())
    kk = kk * lax.rsqrt(jnp.maximum(ss, KK_FLOOR * KK_FLOOR))
    kmod = k * (1.0 + (a - 1.0) * k_a)
    bonus = _dot((r * kmod * r_k).astype(BF16), _head_select())
    for ref, val in zip(out_refs, (r, lw, kmod, v, kk, kk * a, g, bonus)):
        ref[...] = val.astype(ref.dtype)


def _rk_pre_prompt_kernel(h_ref, s0_ref, *refs, tm):
    p, out_refs, carry_ref = refs[:N_RK_IN], refs[N_RK_IN:N_RK_IN + N_RK_OUT], refs[N_RK_IN + N_RK_OUT]

    @pl.when(pl.program_id(1) == 0)
    def _():
        carry_ref[0:1, :] = s0_ref[...]

    h = h_ref[...]
    row = lax.broadcasted_iota(jnp.int32, h.shape, 0)
    prev = jnp.where(row == 0, carry_ref[0:1, :], pltpu.roll(h, 1, axis=0))
    carry_ref[0:1, :] = h[tm - 1:tm, :]
    _rk_pre_core(h, prev, p, out_refs)


def _rk_pre_sample_kernel(h_ref, s0_ref, *refs, ns, t_len):
    p, out_refs = refs[:N_RK_IN], refs[N_RK_IN:N_RK_IN + N_RK_OUT]
    h = h_ref[...]
    rows = ns * t_len
    s0 = jnp.broadcast_to(s0_ref[...], (ns, t_len, D_MODEL)).reshape(rows, D_MODEL)
    t = lax.broadcasted_iota(jnp.int32, h.shape, 0) & (t_len - 1)
    prev = jnp.where(t == 0, s0, pltpu.roll(h, 1, axis=0))
    _rk_pre_core(h, prev, p, out_refs)


def _rk_param_specs(p):
    return [_const_spec(a.shape) for a in p]


def _rk_pre(hn, shift0, p, bsz, t_len, tm):
    m = bsz * t_len
    assert len(p) == N_RK_IN
    out_shape = ([jax.ShapeDtypeStruct((m, D_MODEL), F32 if i == 1 else BF16) for i in range(N_RK_OUT - 1)]
                 + [jax.ShapeDtypeStruct((m, PAIR), F32)])
    s0 = shift0.reshape(bsz, 1, D_MODEL)
    if t_len > tm:
        nt = t_len // tm
        row = pl.BlockSpec((tm, D_MODEL), lambda b, t: (b * nt + t, 0))
        return pl.pallas_call(
            functools.partial(_rk_pre_prompt_kernel, tm=tm),
            grid=(bsz, nt),
            in_specs=[row, pl.BlockSpec((None, 1, D_MODEL), lambda b, t: (b, 0, 0))] + _rk_param_specs(p),
            out_specs=[row] * (N_RK_OUT - 1) + [pl.BlockSpec((tm, PAIR), lambda b, t: (b * nt + t, 0))],
            out_shape=out_shape,
            scratch_shapes=[pltpu.VMEM((SUBLANES, D_MODEL), F32)],
            compiler_params=_params(("arbitrary", "arbitrary")),
            name="rk_pre_prompt",
        )(hn, s0, *p)
    ns = tm // t_len
    row = pl.BlockSpec((tm, D_MODEL), lambda i: (i, 0))
    return pl.pallas_call(
        functools.partial(_rk_pre_sample_kernel, ns=ns, t_len=t_len),
        grid=(m // tm,),
        in_specs=[row, pl.BlockSpec((ns, 1, D_MODEL), lambda i: (i, 0, 0))] + _rk_param_specs(p),
        out_specs=[row] * (N_RK_OUT - 1) + [pl.BlockSpec((tm, PAIR), lambda i: (i, 0))],
        out_shape=out_shape,
        compiler_params=_params(("parallel",)),
        name="rk_pre_sample",
    )(hn, s0, *p)


def _wkv_masks(c):
    rows = STACK_ROWS
    sh = c.bit_length() - 1
    row = lax.broadcasted_iota(jnp.int32, (rows, rows), 0)
    col = lax.broadcasted_iota(jnp.int32, (rows, rows), 1)
    same = (row >> sh) == (col >> sh)
    strict = same & (row > col)
    incl = same & (row >= col)
    base = (row >> 1) == (col >> 1)
    merges = []
    s = 2
    while s < c:
        b = s.bit_length() - 1
        merges.append(((row >> (b + 1)) == (col >> (b + 1))) & (((row >> b) & 1) == 1) & (((col >> b) & 1) == 0))
        s *= 2
    eye = jnp.where(row == col, 1.0, 0.0).astype(F32)
    return strict, incl, base, merges, eye


def _cumsum_rows(x, c):
    t = lax.broadcasted_iota(jnp.int32, x.shape, 0)
    s = 1
    while s < c:
        x = x + jnp.where(t >= s, pltpu.roll(x, s, axis=0), 0.0)
        s *= 2
    return x


def _wkv_units(ins, states, c, masks):
    strict, incl, base, merges, eye = masks
    nseq = len(ins[0])
    rows = STACK_ROWS
    assert nseq * 2 * c == rows
    lane_h = lax.broadcasted_iota(jnp.int32, (c, PAIR), 1) >> HEAD_SHIFT
    seq_rows = [slice(i * 2 * c, (i + 1) * 2 * c) for i in range(nseq)]

    def stack(xs):
        return jnp.concatenate([jnp.where(lane_h == h, x, 0.0) for x in xs for h in range(2)],
                               axis=0).astype(BF16)

    def each(f, *lists):
        return [f(*xs) for xs in zip(*lists)]

    def prep(seqs):
        per = []
        for r, lw, k, v, kk, b in seqs:
            cum = _cumsum_rows(lw, c)
            last = cum[c - 1:c, :]
            to_end = jnp.exp(last - cum)
            p_inv = jnp.exp(-cum)
            per.append(dict(p_c=jnp.exp(last), kk=kk * jnp.exp(cum - lw), r=r * jnp.exp(cum), k=k * p_inv,
                            b=b * p_inv, v=v, kd=k * to_end, nbd=-(b * to_end)))
        out = {name: stack([d[name] for d in per]) for name in ('kk', 'r', 'k', 'b', 'v', 'kd', 'nbd')}
        out['p_c'] = [d['p_c'] for d in per]
        return out

    o = [prep(x) for x in ins]
    kb = [jnp.concatenate([x['k'], x['b']], axis=0) for x in o]
    a_kk = each(lambda x, w_: _dot_nt(x['kk'], w_), o, kb)
    a_r = each(lambda x, w_: _dot_nt(x['r'], w_), o, kb)
    a_kk_k, a_kk_b = [a[:, :rows] for a in a_kk], [a[:, rows:] for a in a_kk]
    a_r_k, a_r_b = [a[:, :rows] for a in a_r], [a[:, rows:] for a in a_r]

    lm = [jnp.where(strict, a, 0.0) for a in a_kk_b]
    inv = [eye - jnp.where(base, l, 0.0) for l in lm]
    size = 2
    for m in merges:
        if size >= SUBLANES:
            starts = range(size, rows, 2 * size)

            def gather(x, starts=starts, size=size):
                return jnp.concatenate([x[r0:r0 + size] for r0 in starts], axis=0)

            def scatter(xh, n=len(starts), size=size):
                zero = jnp.zeros((size, rows), F32)
                return jnp.concatenate([p_ for j in range(n) for p_ in (zero, xh[j * size:(j + 1) * size])],
                                       axis=0)

            inv_b = [t.astype(BF16) for t in inv]
            mid = each(lambda l, t: scatter(_dot(gather(jnp.where(m, l, 0.0)).astype(BF16), t)).astype(BF16),
                       lm, inv_b)
            inv = each(lambda t, md: t - scatter(_dot(gather(t).astype(BF16), md)), inv, mid)
        else:
            inv_b = [t.astype(BF16) for t in inv]
            mid = each(lambda l, t: _dot(jnp.where(m, l, 0.0).astype(BF16), t).astype(BF16), lm, inv_b)
            inv = each(lambda t, tb, md: t - _dot(tb, md), inv, inv_b, mid)
        size *= 2

    def state_dot(x, s_list):
        outs = [_dot_nt(jnp.concatenate([x['kk'][rs], x['r'][rs]], axis=0), s.astype(BF16))
                for rs, s in zip(seq_rows, s_list)]
        h = 2 * c
        if nseq == 1:
            return outs[0][:h], outs[0][h:]
        return (jnp.concatenate([t[:h] for t in outs], axis=0), jnp.concatenate([t[h:] for t in outs], axis=0))

    sd = each(state_dot, o, states)
    kkh, rh = [t[0] for t in sd], [t[1] for t in sd]
    rhs = each(lambda h, a, x: h + _dot(jnp.where(strict, a, 0.0).astype(BF16), x['v']), kkh, a_kk_k, o)
    u_b = each(lambda t, z: _dot(t.astype(BF16), z.astype(BF16)).astype(BF16), inv, rhs)
    y_s = each(lambda h, ak, ab, x, u: h + _dot(jnp.where(incl, ak, 0.0).astype(BF16), x['v'])
               - _dot(jnp.where(incl, ab, 0.0).astype(BF16), u), rh, a_r_k, a_r_b, o, u_b)

    def fold(ys):
        return [ys[rs][:c] + ys[rs][c:] for rs in seq_rows]

    def new_state(x, u, s_list):
        return [s * pc + _dot_tn(jnp.concatenate([x['v'][rs], u[rs]], axis=0),
                                 jnp.concatenate([x['kd'][rs], x['nbd'][rs]], axis=0))
                for rs, s, pc in zip(seq_rows, s_list, x['p_c'])]

    return [fold(ys) for ys in y_s], each(new_state, o, u_b, states)


def _rows_to_blockdiag(s):
    ext = jnp.concatenate([s, jnp.zeros_like(s)], axis=1)
    row = lax.broadcasted_iota(jnp.int32, ext.shape, 0)
    return jnp.where(row < HEAD_DIM, ext, pltpu.roll(ext, HEAD_DIM, axis=1))


def _blockdiag_to_rows(s_bd):
    row = lax.broadcasted_iota(jnp.int32, s_bd.shape, 0)
    return jnp.where(row < HEAD_DIM, s_bd, pltpu.roll(s_bd, HEAD_DIM, axis=1))[:, :HEAD_DIM]


def _wkv_prompt_kernel(r_ref, lw_ref, k_ref, v_ref, kk_ref, b_ref, s0_ref, *refs, c, t_len, nb, n_cast):
    cast_in, (y_ref, sout_ref), cast_out, s_ref = (refs[:n_cast], refs[n_cast:n_cast + 2],
                                                   refs[n_cast + 2:2 * n_cast + 2], refs[2 * n_cast + 2])
    for w_in, w_out in zip(cast_in, cast_out):
        w_out[...] = w_in[...].astype(BF16)
    ci = pl.program_id(1)
    units = [(i, q) for i in range(nb) for q in range(N_PAIRS)]

    @pl.when(ci == 0)
    def _():
        for i, q in units:
            s_ref[i, q] = _rows_to_blockdiag(s0_ref[i, q])

    masks = _wkv_masks(c)
    rows = lax.broadcasted_iota(jnp.int32, (c, PAIR), 0) + ci * c
    valid = rows < t_len
    lanes = [slice(q * PAIR, (q + 1) * PAIR) for q in range(N_PAIRS)]
    ins = [[[jnp.where(valid, ref[i, :, lanes[q]].astype(F32), 0.0)
             for ref in (r_ref, lw_ref, k_ref, v_ref, kk_ref, b_ref)]] for i, q in units]
    ys, s_new = _wkv_units(ins, [[s_ref[i, q]] for i, q in units], c, masks)
    for (i, q), y, s in zip(units, ys, s_new):
        y_ref[i, :, lanes[q]] = y[0].astype(y_ref.dtype)
        s_ref[i, q] = s[0]

    @pl.when(ci == pl.num_programs(1) - 1)
    def _():
        for i, q in units:
            sout_ref[i, q] = _blockdiag_to_rows(s_ref[i, q])


def _wkv_sample_kernel(r_ref, lw_ref, k_ref, v_ref, kk_ref, b_ref, s0_ref, y_ref, sout_ref, *, c, nseq):
    masks = _wkv_masks(c)
    lanes = [slice(q * PAIR, (q + 1) * PAIR) for q in range(N_PAIRS)]
    full = [ref[...].astype(F32) for ref in (r_ref, lw_ref, k_ref, v_ref, kk_ref, b_ref)]
    ins = [[[a[i * c:(i + 1) * c, lanes[q]] for a in full] for i in range(nseq)] for q in range(N_PAIRS)]
    states = [[_rows_to_blockdiag(s0_ref[i, q]) for i in range(nseq)] for q in range(N_PAIRS)]
    ys, s_new = _wkv_units(ins, states, c, masks)
    for q in range(N_PAIRS):
        for i in range(nseq):
            y_ref[i * c:(i + 1) * c, lanes[q]] = ys[q][i]
            sout_ref[i, q] = _blockdiag_to_rows(s_new[q][i])


def _cast_slabs(w, skip, n_steps, nc):
    n_rows = w.shape[0] - skip
    slab = next(r for r in range(BF16_ROWS, n_rows + 1, BF16_ROWS)
                if n_rows % r == 0 and skip % r == 0 and n_rows // r <= n_steps)
    last = n_rows // slab - 1
    first = skip // slab
    in_spec = pl.BlockSpec((slab, w.shape[1]), lambda b, i: (first + jnp.minimum(b * nc + i, last), 0))
    out_spec = pl.BlockSpec((slab, w.shape[1]), lambda b, i: (jnp.minimum(b * nc + i, last), 0))
    return in_spec, out_spec, jax.ShapeDtypeStruct((n_rows, w.shape[1]), BF16)


def _wkv(ins, state, bsz, t_len, t_valid, cast=()):
    m = bsz * t_len
    y_shape = jax.ShapeDtypeStruct((m, D_MODEL), F32)
    s_rows = state.reshape(bsz, N_PAIRS, PAIR, HEAD_DIM)
    s_shape = jax.ShapeDtypeStruct(s_rows.shape, F32)
    if t_len >= WKV_CHUNK:
        c, nb = WKV_CHUNK, min(bsz, 2)
        nc = t_len // c
        ins3 = [a.reshape(bsz, t_len, D_MODEL) for a in ins]
        blk = pl.BlockSpec((nb, c, D_MODEL), lambda b, i: (b, i, 0))
        sblk = pl.BlockSpec((nb, N_PAIRS, PAIR, HEAD_DIM), lambda b, i: (b, 0, 0, 0))
        slabs = [_cast_slabs(w, skip, (bsz // nb) * nc, nc) for w, skip in cast]
        y, s_out, *cast_out = pl.pallas_call(
            functools.partial(_wkv_prompt_kernel, c=c, t_len=t_valid, nb=nb, n_cast=len(cast)),
            grid=(bsz // nb, nc),
            in_specs=[blk] * 6 + [sblk] + [s[0] for s in slabs],
            out_specs=[blk, sblk] + [s[1] for s in slabs],
            out_shape=[jax.ShapeDtypeStruct((bsz, t_len, D_MODEL), BF16), s_shape] + [s[2] for s in slabs],
            scratch_shapes=[pltpu.VMEM((nb, N_PAIRS, PAIR, PAIR), F32)],
            compiler_params=_params(("arbitrary", "arbitrary")),
            name="wkv_prompt",
        )(*ins3, s_rows, *[w for w, _ in cast])
        return y.reshape(m, D_MODEL), s_out.reshape(state.shape), cast_out
    assert t_valid == t_len and not cast
    c, nseq = t_len, WKV_CHUNK // t_len
    blk = pl.BlockSpec((nseq * c, D_MODEL), lambda i: (i, 0))
    sblk = pl.BlockSpec((nseq, N_PAIRS, PAIR, HEAD_DIM), lambda i: (i, 0, 0, 0))
    y, s_out = pl.pallas_call(
        functools.partial(_wkv_sample_kernel, c=c, nseq=nseq),
        grid=(bsz // nseq,),
        in_specs=[blk] * 6 + [sblk],
        out_specs=[blk, sblk],
        out_shape=[y_shape, s_shape],
        compiler_params=_params(("parallel",)),
        name="wkv_sample",
    )(*ins, s_rows)
    return y, s_out.reshape(state.shape), []


def _rk_post_kernel(x_ref, y_ref, bonus_ref, v_ref, g_ref, lnw_ref, lnb_ref, wo_ref, o_ref):
    ones = _head_ones()
    y = y_ref[...].astype(F32)
    inv_n = 1.0 / HEAD_DIM
    mean = _head_sum(y, ones) * inv_n
    yc = y - mean
    var = _head_sum(yc * yc, ones) * inv_n
    yn = yc * lax.rsqrt(var + GN_EPS) * lnw_ref[...] + lnb_ref[...]
    coef = bonus_ref[...]
    hi = coef.astype(BF16)
    lo = (coef - hi.astype(F32)).astype(BF16)
    spread = _head_select(transpose=True)
    bonus = (_dot(hi, spread) + _dot(lo, spread)) * v_ref[...].astype(F32)
    o = ((yn + bonus) * g_ref[...].astype(F32)).astype(BF16)
    o_ref[...] = x_ref[...] + _dot(o, wo_ref[...])


def _rk_post(x, y, bonus, v, g, ln_w, ln_b, w_o, tm):
    m = x.shape[0]
    row = pl.BlockSpec((tm, D_MODEL), lambda i: (i, 0))
    vec = _const_spec((1, D_MODEL))
    return pl.pallas_call(
        _rk_post_kernel,
        grid=(m // tm,),
        in_specs=[row, row, pl.BlockSpec((tm, PAIR), lambda i: (i, 0)), row, row, vec, vec,
                  _const_spec((D_MODEL, D_MODEL))],
        out_specs=row,
        out_shape=jax.ShapeDtypeStruct((m, D_MODEL), F32),
        compiler_params=_params(("parallel",)),
        name="rk_post",
    )(x, y, bonus, v, g, ln_w.reshape(1, D_MODEL), ln_b.reshape(1, D_MODEL), w_o)


def _conv_core(x, h, shifted, win_ref, cw_ref, wout_ref):
    z = _dot(h.astype(BF16), win_ref[...])
    gate_b = z[:, 0:D_MODEL]
    u = z[:, D_MODEL:2 * D_MODEL] * z[:, 2 * D_MODEL:3 * D_MODEL]
    cw = cw_ref[...]
    u1, u2 = shifted(u)
    conv = u2 * cw[0:1, :] + u1 * cw[1:2, :] + u * cw[2:3, :]
    out = x + _dot((gate_b * conv).astype(BF16), wout_ref[...])
    return out, u


def _conv_prompt_kernel(x_ref, h_ref, c0_ref, win_ref, cw_ref, wout_ref, o_ref, st_ref, carry_ref, *, tm):
    @pl.when(pl.program_id(1) == 0)
    def _():
        carry_ref[0:2, :] = c0_ref[...]

    def shifted(u):
        row = lax.broadcasted_iota(jnp.int32, u.shape, 0)
        c0 = carry_ref[0:1, :]
        c1 = carry_ref[1:2, :]
        u1 = jnp.where(row == 0, c1, pltpu.roll(u, 1, axis=0))
        u2 = jnp.where(row == 0, c0, jnp.where(row == 1, c1, pltpu.roll(u, 2, axis=0)))
        return u1, u2

    out, u = _conv_core(x_ref[...], h_ref[...], shifted, win_ref, cw_ref, wout_ref)
    o_ref[...] = out
    carry_ref[0:2, :] = u[tm - 2:tm, :]
    st_ref[...] = u[tm - SUBLANES:tm, :]


def _conv_sample_kernel(x_ref, h_ref, c0_ref, win_ref, cw_ref, wout_ref, o_ref, u_ref, *, ns, t_len):
    rows = ns * t_len
    c0 = jnp.broadcast_to(c0_ref[:, 0:1, :], (ns, t_len, D_MODEL)).reshape(rows, D_MODEL)
    c1 = jnp.broadcast_to(c0_ref[:, 1:2, :], (ns, t_len, D_MODEL)).reshape(rows, D_MODEL)

    def shifted(u):
        t = lax.broadcasted_iota(jnp.int32, u.shape, 0) & (t_len - 1)
        u1 = jnp.where(t == 0, c1, pltpu.roll(u, 1, axis=0))
        u2 = jnp.where(t == 0, c0, jnp.where(t == 1, c1, pltpu.roll(u, 2, axis=0)))
        return u1, u2

    out, u = _conv_core(x_ref[...], h_ref[...], shifted, win_ref, cw_ref, wout_ref)
    o_ref[...] = out
    u_ref[...] = u


def _conv_mix(x, hn, conv0, w_in, conv_w, w_out, bsz, t_len, t_valid, tm):
    m = bsz * t_len
    consts = [_const_spec((D_MODEL, 3 * D_MODEL)), _const_spec((CONV_W, D_MODEL)), _const_spec((D_MODEL, D_MODEL))]
    if t_len > tm:
        assert t_valid == t_len
        nt = t_len // tm
        row = pl.BlockSpec((tm, D_MODEL), lambda b, t: (b * nt + t, 0))
        out, tail = pl.pallas_call(
            functools.partial(_conv_prompt_kernel, tm=tm),
            grid=(bsz, nt),
            in_specs=[row, row, pl.BlockSpec((None, CONV_W - 1, D_MODEL), lambda b, t: (b, 0, 0))] + consts,
            out_specs=[row, pl.BlockSpec((None, SUBLANES, D_MODEL), lambda b, t: (b, 0, 0))],
            out_shape=[jax.ShapeDtypeStruct((m, D_MODEL), F32),
                       jax.ShapeDtypeStruct((bsz, SUBLANES, D_MODEL), F32)],
            scratch_shapes=[pltpu.VMEM((SUBLANES, D_MODEL), F32)],
            compiler_params=_params(("arbitrary", "arbitrary")),
            name="conv_prompt",
        )(x, hn, conv0, w_in, conv_w, w_out)
        return out, tail[:, SUBLANES - (CONV_W - 1):, :]
    ns = tm // t_len
    row = pl.BlockSpec((tm, D_MODEL), lambda i: (i, 0))
    out, u = pl.pallas_call(
        functools.partial(_conv_sample_kernel, ns=ns, t_len=t_len),
        grid=(m // tm,),
        in_specs=[row, row, pl.BlockSpec((ns, CONV_W - 1, D_MODEL), lambda i: (i, 0, 0))] + consts,
        out_specs=[row, row],
        out_shape=[jax.ShapeDtypeStruct((m, D_MODEL), F32)] * 2,
        compiler_params=_params(("parallel",)),
        name="conv_sample",
    )(x, hn, conv0, w_in, conv_w, w_out)
    return out, u.reshape(bsz, t_len, D_MODEL)[:, t_valid - (CONV_W - 1):t_valid, :]


def _trunk_head(x, wkv_in, shift_in, p, w00, bsz, t_len, t_valid, tm, cast=()):
    x, hn = _ffn(x, p['ffn_norm'][0, 0], w00, p['mix_norm'][0], tm)
    new_shift = hn.reshape(bsz, t_len, D_MODEL)[:, t_valid - 1]
    r, lw, k, v, kk, b, g, bonus = _rk_pre(hn, shift_in[0], p['rk_pre'], bsz, t_len, tm)
    y, new_wkv, cast_out = _wkv([r, lw, k, v, kk, b], wkv_in[0], bsz, t_len, t_valid, cast)
    return (x, y, bonus, v, g), new_wkv[None], new_shift[None], cast_out


def _trunk_tail(acts, conv_in, p, w_rest, bsz, t_len, t_valid, tm, want_y=True):
    x, y, bonus, v, g = acts
    norm = p['ffn_norm']
    tm_wide = 2 * tm if t_len >= 2 * tm else tm
    x = _rk_post(x, y, bonus, v, g, p['rk_ln_w'][0], p['rk_ln_b'][0], p['rk_w_o'][0], tm_wide)
    (x,) = _ffn(x, norm[0, 1], w_rest[0], norm[1, 0], tm, emit_n=False)
    x, hn = _ffn(x, norm[1, 0], w_rest[1], p['mix_norm'][1], tm, n_dtype=BF16)
    x, new_conv = _conv_mix(x, hn, conv_in[0], p['sc_w_in'][0], p['sc_conv_w'][0], p['sc_w_out'][0],
                            bsz, t_len, t_valid, tm_wide)
    if not want_y:
        return None, new_conv[None]
    (y_out,) = _ffn(x, norm[1, 1], w_rest[2], p['final_norm'], tm, emit_x=False)
    return y_out, new_conv[None]


def kernel(x_prompt, x_sample, state_wkv, state_shift, state_conv, meta, ffn_norm, ffn_w_gu, ffn_w_down,
           mix_norm, final_norm, rk_mu, rk_w_rkv, rk_w0, rk_w1, rk_w2, rk_a0, rk_a1, rk_a2, rk_g1, rk_g2,
           rk_k_k, rk_k_a, rk_r_k, rk_ln_w, rk_ln_b, rk_w_o, sc_w_in, sc_conv_w, sc_w_out):
    assert DEPTH == 2 and rk_mu.shape[0] == 1 and sc_w_in.shape[0] == 1
    bf = lambda a: a.astype(BF16)
    vec = lambda a: a.reshape(1, D_MODEL)
    p = {
        'ffn_norm': ffn_norm, 'mix_norm': mix_norm, 'final_norm': final_norm,
        'rk_pre': [rk_mu[0], bf(rk_w_rkv[0, 0]), bf(rk_w_rkv[0, 1]), bf(rk_w_rkv[0, 2]), vec(rk_w0[0]),
                   bf(rk_w1[0]), bf(rk_w2[0]), vec(rk_a0[0]), bf(rk_a1[0]), bf(rk_a2[0]), bf(rk_g1[0]),
                   bf(rk_g2[0]), vec(rk_k_k[0]), vec(rk_k_a[0]), rk_r_k.reshape(1, D_MODEL)],
        'rk_ln_w': rk_ln_w, 'rk_ln_b': rk_ln_b, 'rk_w_o': bf(rk_w_o),
        'sc_w_in': bf(sc_w_in), 'sc_conv_w': sc_conv_w, 'sc_w_out': bf(sc_w_out),
    }
    w00 = (bf(ffn_w_gu[0, 0])[None], bf(ffn_w_down[0, 0])[None], 0)
    later = ((ffn_w_gu.reshape(2 * DEPTH * D_MODEL, 2 * D_FF), D_MODEL),
             (ffn_w_down.reshape(2 * DEPTH * D_FF, D_MODEL), D_FF))
    bs, t_s, _ = x_sample.shape
    bp, seq, _ = x_prompt.shape
    meta_rows = jnp.pad(meta.astype(x_prompt.dtype), ((0, WKV_CHUNK - N_META), (0, 0)))
    zero_wkv = jnp.zeros((1, 1, N_HEADS, HEAD_DIM, HEAD_DIM), state_wkv.dtype)
    zero_shift = jnp.zeros((1, 1, D_MODEL), state_shift.dtype)
    zero_conv = jnp.zeros((1, 1, CONV_W - 1, D_MODEL), state_conv.dtype)
    to_batch = lambda s: jnp.broadcast_to(s, (1, bp) + s.shape[2:])
    acts_m, wkv_m, shift_m, _ = _trunk_head(meta_rows, zero_wkv, zero_shift, p, w00, 1, WKV_CHUNK, N_META,
                                            WKV_CHUNK)
    acts_p, wkv_p, shift_p, (wgu_rest, wd_rest) = _trunk_head(
        x_prompt.reshape(bp * seq, D_MODEL), to_batch(wkv_m), to_batch(shift_m), p, w00, bp, seq, seq,
        ROW_TILE_LONG, cast=later)
    acts_s, wkv_s, shift_s, _ = _trunk_head(x_sample.reshape(bs * t_s, D_MODEL), state_wkv, state_shift, p, w00,
                                            bs, t_s, t_s, ROW_TILE_SHORT)
    wgu_rest = wgu_rest.reshape(2 * DEPTH - 1, D_MODEL, 2 * D_FF)
    wd_rest = wd_rest.reshape(2 * DEPTH - 1, D_FF, D_MODEL)
    w_rest = [(wgu_rest, wd_rest, i) for i in range(2 * DEPTH - 1)]
    _, conv_m = _trunk_tail(acts_m, zero_conv, p, w_rest, 1, WKV_CHUNK, N_META, WKV_CHUNK, want_y=False)
    yp, conv_p = _trunk_tail(acts_p, to_batch(conv_m), p, w_rest, bp, seq, seq, ROW_TILE_LONG)
    ys, conv_s = _trunk_tail(acts_s, state_conv, p, w_rest, bs, t_s, t_s, ROW_TILE_SHORT)
    return (yp.reshape(bp, seq, D_MODEL), ys.reshape(bs, t_s, D_MODEL), wkv_p, shift_p, conv_p,
            wkv_s, shift_s, conv_s)
```

```python
import functools
import math

import jax
import jax.numpy as jnp
from jax import lax
from jax.experimental import pallas as pl
from jax.experimental.pallas import tpu as pltpu

D_MODEL = 1024
HEAD_DIM = 64
N_HEADS = D_MODEL // HEAD_DIM
D_FF = 2816
N_META = 16
DEPTH = 2
CONV_W = 3
RMS_EPS = 1e-6
GN_EPS = 64e-5
DECAY_SCALE = math.exp(-0.5)

SUBLANES = 8
BF16_ROWS = 16
MXU_DIM = 256
PAIR = 2 * HEAD_DIM
HEAD_SHIFT = HEAD_DIM.bit_length() - 1
KK_FLOOR = 1e-12
WKV_CHUNK = 64
STACK_ROWS = 2 * WKV_CHUNK
WKV_BATCH = 4
N_PAIRS = D_MODEL // PAIR
VMEM_LIMIT = 56 * 1024 * 1024
ROW_TILE_LONG = 512
ROW_TILE_SHORT = 512

FFN_CHUNKS = ((0, 6 * MXU_DIM), (6 * MXU_DIM, D_FF))

F32 = jnp.float32
BF16 = jnp.bfloat16


def _dot(a, b):
    return jnp.dot(a, b, preferred_element_type=F32)


def _dot_nt(a, b):
    return lax.dot_general(a, b, (((1,), (1,)), ((), ())), preferred_element_type=F32)


def _dot_tn(a, b):
    return lax.dot_general(a, b, (((0,), (0,)), ((), ())), preferred_element_type=F32)


def _rms(x, g):
    ms = jnp.mean(x * x, axis=-1, keepdims=True)
    return x * lax.rsqrt(ms + RMS_EPS) * g


def _sigmoid(x):
    return 1.0 / (1.0 + jnp.exp(-x))


def _head_ones():
    r = lax.broadcasted_iota(jnp.int32, (PAIR, PAIR), 0) >> HEAD_SHIFT
    c = lax.broadcasted_iota(jnp.int32, (PAIR, PAIR), 1) >> HEAD_SHIFT
    return jnp.where(r == c, 1.0, 0.0).astype(BF16)


def _head_sum(x, ones):
    xb = x.astype(BF16)
    return jnp.concatenate([_dot(xb[:, j * PAIR:(j + 1) * PAIR], ones) for j in range(N_PAIRS)], axis=1)


def _const_spec(shape):
    nd = len(shape)
    return pl.BlockSpec(shape, lambda *_: (0,) * nd, pipeline_mode=pl.Buffered(1))


def _params(sem):
    return pltpu.CompilerParams(dimension_semantics=sem, vmem_limit_bytes=VMEM_LIMIT)


def _ffn_kernel(x_ref, g_ref, wgu_ref, wd_ref, g2_ref, *out_refs, emit_x, emit_n):
    x = x_ref[...]
    xn = _rms(x, g_ref[...]).astype(BF16)
    acc = None
    for lo, hi in FFN_CHUNKS:
        gate = _dot(xn, wgu_ref[:, lo:hi])
        up = _dot(xn, wgu_ref[:, D_FF + lo:D_FF + hi])
        act = (gate * _sigmoid(gate) * up).astype(BF16)
        part = _dot(act, wd_ref[lo:hi, :])
        acc = part if acc is None else acc + part
    out = x + 0.5 * acc
    i = 0
    if emit_x:
        out_refs[i][...] = out
        i += 1
    if emit_n:
        out_refs[i][...] = _rms(out, g2_ref[...]).astype(out_refs[i].dtype)


def _ffn(x, g, w, g2, tm, emit_x=True, emit_n=True, n_dtype=F32):
    wgu, wd, idx = w
    m = x.shape[0]
    row = pl.BlockSpec((tm, D_MODEL), lambda i: (i, 0))
    n_out = int(emit_x) + int(emit_n)
    dtypes = [F32] * int(emit_x) + [n_dtype] * int(emit_n)

    def weight_spec(rows, cols):
        return pl.BlockSpec((None, rows, cols), lambda i: (idx, 0, 0), pipeline_mode=pl.Buffered(1))

    outs = pl.pallas_call(
        functools.partial(_ffn_kernel, emit_x=emit_x, emit_n=emit_n),
        grid=(m // tm,),
        in_specs=[row, _const_spec((1, D_MODEL)), weight_spec(D_MODEL, 2 * D_FF),
                  weight_spec(D_FF, D_MODEL), _const_spec((1, D_MODEL))],
        out_specs=[row] * n_out,
        out_shape=[jax.ShapeDtypeStruct((m, D_MODEL), dt) for dt in dtypes],
        compiler_params=_params(("parallel",)),
        name="ffn",
    )(x, g.reshape(1, D_MODEL), wgu, wd, g2.reshape(1, D_MODEL))
    return tuple(outs)


N_RK_IN = 15
N_RK_OUT = 8


def _head_select(transpose=False):
    shape = (PAIR, D_MODEL) if transpose else (D_MODEL, PAIR)
    chan = lax.broadcasted_iota(jnp.int32, shape, 1 if transpose else 0) >> HEAD_SHIFT
    head = lax.broadcasted_iota(jnp.int32, shape, 0 if transpose else 1)
    return jnp.where(chan == head, 1.0, 0.0).astype(BF16)


def _rk_pre_core(h, prev, p, out_refs):
    (mu, wr, wk, wv, w0, w1, w2, a0, a1, a2, g1, g2, k_k, k_a, r_k) = [r[...] for r in p]
    xx = prev - h
    xr, xw, xk, xv, xa, xg = [(h + xx * mu[c:c + 1, :]).astype(BF16) for c in range(6)]
    r = _dot(xr, wr)
    k = _dot(xk, wk)
    v = _dot(xv, wv)
    z = w0 + _dot(jnp.tanh(_dot(xw, w1)).astype(BF16), w2)
    lw = -DECAY_SCALE * _sigmoid(z)
    a = _sigmoid(a0 + _dot(_dot(xa, a1).astype(BF16), a2))
    g = _dot(_sigmoid(_dot(xg, g1)).astype(BF16), g2)
    kk = k * k_k
    ss = _head_sum(kk * kk, _head_ones())
    kk = kk * lax.rsqrt(jnp.maximum(ss, KK_FLOOR * KK_FLOOR))
    kmod = k * (1.0 + (a - 1.0) * k_a)
    bonus = _dot((r * kmod * r_k).astype(BF16), _head_select())
    for ref, val in zip(out_refs, (r, lw, kmod, v, kk, kk * a, g, bonus)):
        ref[...] = val.astype(ref.dtype)


def _rk_pre_prompt_kernel(h_ref, s0_ref, *refs, tm):
    p, out_refs, carry_ref = refs[:N_RK_IN], refs[N_RK_IN:N_RK_IN + N_RK_OUT], refs[N_RK_IN + N_RK_OUT]

    @pl.when(pl.program_id(1) == 0)
    def _():
        carry_ref[0:1, :] = s0_ref[...]

    h = h_ref[...]
    row = lax.broadcasted_iota(jnp.int32, h.shape, 0)
    prev = jnp.where(row == 0, carry_ref[0:1, :], pltpu.roll(h, 1, axis=0))
    carry_ref[0:1, :] = h[tm - 1:tm, :]
    _rk_pre_core(h, prev, p, out_refs)


def _rk_pre_sample_kernel(h_ref, s0_ref, *refs, ns, t_len):
    p, out_refs = refs[:N_RK_IN], refs[N_RK_IN:N_RK_IN + N_RK_OUT]
    h = h_ref[...]
    rows = ns * t_len
    s0 = jnp.broadcast_to(s0_ref[...], (ns, t_len, D_MODEL)).reshape(rows, D_MODEL)
    t = lax.broadcasted_iota(jnp.int32, h.shape, 0) & (t_len - 1)
    prev = jnp.where(t == 0, s0, pltpu.roll(h, 1, axis=0))
    _rk_pre_core(h, prev, p, out_refs)


def _rk_param_specs(p):
    return [_const_spec(a.shape) for a in p]


def _rk_pre(hn, shift0, p, bsz, t_len, tm):
    m = bsz * t_len
    assert len(p) == N_RK_IN
    out_shape = ([jax.ShapeDtypeStruct((m, D_MODEL), F32 if i == 1 else BF16) for i in range(N_RK_OUT - 1)]
                 + [jax.ShapeDtypeStruct((m, PAIR), F32)])
    s0 = shift0.reshape(bsz, 1, D_MODEL)
    if t_len > tm:
        nt = t_len // tm
        row = pl.BlockSpec((tm, D_MODEL), lambda b, t: (b * nt + t, 0))
        return pl.pallas_call(
            functools.partial(_rk_pre_prompt_kernel, tm=tm),
            grid=(bsz, nt),
            in_specs=[row, pl.BlockSpec((None, 1, D_MODEL), lambda b, t: (b, 0, 0))] + _rk_param_specs(p),
            out_specs=[row] * (N_RK_OUT - 1) + [pl.BlockSpec((tm, PAIR), lambda b, t: (b * nt + t, 0))],
            out_shape=out_shape,
            scratch_shapes=[pltpu.VMEM((SUBLANES, D_MODEL), F32)],
            compiler_params=_params(("arbitrary", "arbitrary")),
            name="rk_pre_prompt",
        )(hn, s0, *p)
    ns = tm // t_len
    row = pl.BlockSpec((tm, D_MODEL), lambda i: (i, 0))
    return pl.pallas_call(
        functools.partial(_rk_pre_sample_kernel, ns=ns, t_len=t_len),
        grid=(m // tm,),
        in_specs=[row, pl.BlockSpec((ns, 1, D_MODEL), lambda i: (i, 0, 0))] + _rk_param_specs(p),
        out_specs=[row] * (N_RK_OUT - 1) + [pl.BlockSpec((tm, PAIR), lambda i: (i, 0))],
        out_shape=out_shape,
        compiler_params=_params(("parallel",)),
        name="rk_pre_sample",
    )(hn, s0, *p)


def _wkv_masks(c):
    rows = STACK_ROWS
    sh = c.bit_length() - 1
    row = lax.broadcasted_iota(jnp.int32, (rows, rows), 0)
    col = lax.broadcasted_iota(jnp.int32, (rows, rows), 1)
    same = (row >> sh) == (col >> sh)
    strict = same & (row > col)
    incl = same & (row >= col)
    base = (row >> 1) == (col >> 1)
    merges = []
    s = 2
    while s < c:
        b = s.bit_length() - 1
        merges.append(((row >> (b + 1)) == (col >> (b + 1))) & (((row >> b) & 1) == 1) & (((col >> b) & 1) == 0))
        s *= 2
    eye = jnp.where(row == col, 1.0, 0.0).astype(F32)
    return strict, incl, base, merges, eye


def _cumsum_rows(x, c):
    t = lax.broadcasted_iota(jnp.int32, x.shape, 0)
    s = 1
    while s < c:
        x = x + jnp.where(t >= s, pltpu.roll(x, s, axis=0), 0.0)
        s *= 2
    return x


def _wkv_units(ins, states, c, masks):
    strict, incl, base, merges, eye = masks
    nseq = len(ins[0])
    rows = STACK_ROWS
    assert nseq * 2 * c == rows
    lane_h = lax.broadcasted_iota(jnp.int32, (c, PAIR), 1) >> HEAD_SHIFT
    seq_rows = [slice(i * 2 * c, (i + 1) * 2 * c) for i in range(nseq)]

    def stack(xs):
        return jnp.concatenate([jnp.where(lane_h == h, x, 0.0) for x in xs for h in range(2)],
                               axis=0).astype(BF16)

    def each(f, *lists):
        return [f(*xs) for xs in zip(*lists)]

    def prep(seqs):
        per = []
        for r, lw, k, v, kk, b in seqs:
            cum = _cumsum_rows(lw, c)
            last = cum[c - 1:c, :]
            to_end = jnp.exp(last - cum)
            p_inv = jnp.exp(-cum)
            per.append(dict(p_c=jnp.exp(last), kk=kk * jnp.exp(cum - lw), r=r * jnp.exp(cum), k=k * p_inv,
                            b=b * p_inv, v=v, kd=k * to_end, nbd=-(b * to_end)))
        out = {name: stack([d[name] for d in per]) for name in ('kk', 'r', 'k', 'b', 'v', 'kd', 'nbd')}
        out['p_c'] = [d['p_c'] for d in per]
        return out

    o = [prep(x) for x in ins]
    kb = [jnp.concatenate([x['k'], x['b']], axis=0) for x in o]
    a_kk = each(lambda x, w_: _dot_nt(x['kk'], w_), o, kb)
    a_r = each(lambda x, w_: _dot_nt(x['r'], w_), o, kb)
    a_kk_k, a_kk_b = [a[:, :rows] for a in a_kk], [a[:, rows:] for a in a_kk]
    a_r_k, a_r_b = [a[:, :rows] for a in a_r], [a[:, rows:] for a in a_r]

    lm = [jnp.where(strict, a, 0.0) for a in a_kk_b]
    inv = [eye - jnp.where(base, l, 0.0) for l in lm]
    size = 2
    for m in merges:
        if size >= SUBLANES:
            starts = range(size, rows, 2 * size)

            def gather(x, starts=starts, size=size):
                return jnp.concatenate([x[r0:r0 + size] for r0 in starts], axis=0)

            def scatter(xh, n=len(starts), size=size):
                zero = jnp.zeros((size, rows), F32)
                return jnp.concatenate([p_ for j in range(n) for p_ in (zero, xh[j * size:(j + 1) * size])],
                                       axis=0)

            inv_b = [t.astype(BF16) for t in inv]
            mid = each(lambda l, t: scatter(_dot(gather(jnp.where(m, l, 0.0)).astype(BF16), t)).astype(BF16),
                       lm, inv_b)
            inv = each(lambda t, md: t - scatter(_dot(gather(t).astype(BF16), md)), inv, mid)
        else:
            inv_b = [t.astype(BF16) for t in inv]
            mid = each(lambda l, t: _dot(jnp.where(m, l, 0.0).astype(BF16), t).astype(BF16), lm, inv_b)
            inv = each(lambda t, tb, md: t - _dot(tb, md), inv, inv_b, mid)
        size *= 2

    def state_dot(x, s_list):
        outs = [_dot_nt(jnp.concatenate([x['kk'][rs], x['r'][rs]], axis=0), s.astype(BF16))
                for rs, s in zip(seq_rows, s_list)]
        h = 2 * c
        if nseq == 1:
            return outs[0][:h], outs[0][h:]
        return (jnp.concatenate([t[:h] for t in outs], axis=0), jnp.concatenate([t[h:] for t in outs], axis=0))

    sd = each(state_dot, o, states)
    kkh, rh = [t[0] for t in sd], [t[1] for t in sd]
    rhs = each(lambda h, a, x: h + _dot(jnp.where(strict, a, 0.0).astype(BF16), x['v']), kkh, a_kk_k, o)
    u_b = each(lambda t, z: _dot(t.astype(BF16), z.astype(BF16)).astype(BF16), inv, rhs)
    y_s = each(lambda h, ak, ab, x, u: h + _dot(jnp.where(incl, ak, 0.0).astype(BF16), x['v'])
               - _dot(jnp.where(incl, ab, 0.0).astype(BF16), u), rh, a_r_k, a_r_b, o, u_b)

    def fold(ys):
        return [ys[rs][:c] + ys[rs][c:] for rs in seq_rows]

    def new_state(x, u, s_list):
        return [s * pc + _dot_tn(jnp.concatenate([x['v'][rs], u[rs]], axis=0),
                                 jnp.concatenate([x['kd'][rs], x['nbd'][rs]], axis=0))
                for rs, s, pc in zip(seq_rows, s_list, x['p_c'])]

    return [fold(ys) for ys in y_s], each(new_state, o, u_b, states)


def _rows_to_blockdiag(s):
    ext = jnp.concatenate([s, jnp.zeros_like(s)], axis=1)
    row = lax.broadcasted_iota(jnp.int32, ext.shape, 0)
    return jnp.where(row < HEAD_DIM, ext, pltpu.roll(ext, HEAD_DIM, axis=1))


def _blockdiag_to_rows(s_bd):
    row = lax.broadcasted_iota(jnp.int32, s_bd.shape, 0)
    return jnp.where(row < HEAD_DIM, s_bd, pltpu.roll(s_bd, HEAD_DIM, axis=1))[:, :HEAD_DIM]


def _wkv_prompt_kernel(r_ref, lw_ref, k_ref, v_ref, kk_ref, b_ref, s0_ref, *refs, c, t_len, nb, n_cast):
    cast_in, (y_ref, sout_ref), cast_out, s_ref = (refs[:n_cast], refs[n_cast:n_cast + 2],
                                                   refs[n_cast + 2:2 * n_cast + 2], refs[2 * n_cast + 2])
    for w_in, w_out in zip(cast_in, cast_out):
        w_out[...] = w_in[...].astype(BF16)
    ci = pl.program_id(1)
    units = [(i, q) for i in range(nb) for q in range(N_PAIRS)]

    @pl.when(ci == 0)
    def _():
        for i, q in units:
            s_ref[i, q] = _rows_to_blockdiag(s0_ref[i, q])

    masks = _wkv_masks(c)
    lanes = [slice(q * PAIR, (q + 1) * PAIR) for q in range(N_PAIRS)]

    def load(ref, i, q):
        x = ref[i, :, lanes[q]].astype(F32)
        if t_len % c:
            valid = lax.broadcasted_iota(jnp.int32, (c, PAIR), 0) + ci * c < t_len
            x = jnp.where(valid, x, 0.0)
        return x

    ins = [[[load(ref, i, q) for ref in (r_ref, lw_ref, k_ref, v_ref, kk_ref, b_ref)]] for i, q in units]
    ys, s_new = _wkv_units(ins, [[s_ref[i, q]] for i, q in units], c, masks)
    for (i, q), y, s in zip(units, ys, s_new):
        y_ref[i, :, lanes[q]] = y[0].astype(y_ref.dtype)
        s_ref[i, q] = s[0]

    @pl.when(ci == pl.num_programs(1) - 1)
    def _():
        for i, q in units:
            sout_ref[i, q] = _blockdiag_to_rows(s_ref[i, q])


def _wkv_sample_kernel(r_ref, lw_ref, k_ref, v_ref, kk_ref, b_ref, s0_ref, y_ref, sout_ref, *, c, nseq):
    masks = _wkv_masks(c)
    lanes = [slice(q * PAIR, (q + 1) * PAIR) for q in range(N_PAIRS)]
    full = [ref[...].astype(F32) for ref in (r_ref, lw_ref, k_ref, v_ref, kk_ref, b_ref)]
    ins = [[[a[i * c:(i + 1) * c, lanes[q]] for a in full] for i in range(nseq)] for q in range(N_PAIRS)]
    states = [[_rows_to_blockdiag(s0_ref[i, q]) for i in range(nseq)] for q in range(N_PAIRS)]
    ys, s_new = _wkv_units(ins, states, c, masks)
    for q in range(N_PAIRS):
        for i in range(nseq):
            y_ref[i * c:(i + 1) * c, lanes[q]] = ys[q][i]
            sout_ref[i, q] = _blockdiag_to_rows(s_new[q][i])


def _cast_slabs(w, skip, n_steps, nc):
    n_rows = w.shape[0] - skip
    slab = next(r for r in range(BF16_ROWS, n_rows + 1, BF16_ROWS)
                if n_rows % r == 0 and skip % r == 0 and n_rows // r <= n_steps)
    last = n_rows // slab - 1
    first = skip // slab
    in_spec = pl.BlockSpec((slab, w.shape[1]), lambda b, i: (first + jnp.minimum(b * nc + i, last), 0))
    out_spec = pl.BlockSpec((slab, w.shape[1]), lambda b, i: (jnp.minimum(b * nc + i, last), 0))
    return in_spec, out_spec, jax.ShapeDtypeStruct((n_rows, w.shape[1]), BF16)


def _wkv(ins, state, bsz, t_len, t_valid, cast=()):
    m = bsz * t_len
    y_shape = jax.ShapeDtypeStruct((m, D_MODEL), F32)
    s_rows = state.reshape(bsz, N_PAIRS, PAIR, HEAD_DIM)
    s_shape = jax.ShapeDtypeStruct(s_rows.shape, F32)
    if t_len >= WKV_CHUNK:
        c, nb = WKV_CHUNK, min(bsz, WKV_BATCH)
        nc = t_len // c
        ins3 = [a.reshape(bsz, t_len, D_MODEL) for a in ins]
        blk = pl.BlockSpec((nb, c, D_MODEL), lambda b, i: (b, i, 0))
        sblk = pl.BlockSpec((nb, N_PAIRS, PAIR, HEAD_DIM), lambda b, i: (b, 0, 0, 0))
        slabs = [_cast_slabs(w, skip, (bsz // nb) * nc, nc) for w, skip in cast]
        y, s_out, *cast_out = pl.pallas_call(
            functools.partial(_wkv_prompt_kernel, c=c, t_len=t_valid, nb=nb, n_cast=len(cast)),
            grid=(bsz // nb, nc),
            in_specs=[blk] * 6 + [sblk] + [s[0] for s in slabs],
            out_specs=[blk, sblk] + [s[1] for s in slabs],
            out_shape=[jax.ShapeDtypeStruct((bsz, t_len, D_MODEL), BF16), s_shape] + [s[2] for s in slabs],
            scratch_shapes=[pltpu.VMEM((nb, N_PAIRS, PAIR, PAIR), F32)],
            compiler_params=_params(("arbitrary", "arbitrary")),
            name="wkv_prompt",
        )(*ins3, s_rows, *[w for w, _ in cast])
        return y.reshape(m, D_MODEL), s_out.reshape(state.shape), cast_out
    assert t_valid == t_len and not cast
    c, nseq = t_len, WKV_CHUNK // t_len
    blk = pl.BlockSpec((nseq * c, D_MODEL), lambda i: (i, 0))
    sblk = pl.BlockSpec((nseq, N_PAIRS, PAIR, HEAD_DIM), lambda i: (i, 0, 0, 0))
    y, s_out = pl.pallas_call(
        functools.partial(_wkv_sample_kernel, c=c, nseq=nseq),
        grid=(bsz // nseq,),
        in_specs=[blk] * 6 + [sblk],
        out_specs=[blk, sblk],
        out_shape=[y_shape, s_shape],
        compiler_params=_params(("parallel",)),
        name="wkv_sample",
    )(*ins, s_rows)
    return y, s_out.reshape(state.shape), []


def _rk_post_kernel(x_ref, y_ref, bonus_ref, v_ref, g_ref, lnw_ref, lnb_ref, wo_ref, o_ref):
    ones = _head_ones()
    y = y_ref[...].astype(F32)
    inv_n = 1.0 / HEAD_DIM
    mean = _head_sum(y, ones) * inv_n
    yc = y - mean
    var = _head_sum(yc * yc, ones) * inv_n
    yn = yc * lax.rsqrt(var + GN_EPS) * lnw_ref[...] + lnb_ref[...]
    coef = bonus_ref[...]
    hi = coef.astype(BF16)
    lo = (coef - hi.astype(F32)).astype(BF16)
    spread = _head_select(transpose=True)
    bonus = (_dot(hi, spread) + _dot(lo, spread)) * v_ref[...].astype(F32)
    o = ((yn + bonus) * g_ref[...].astype(F32)).astype(BF16)
    o_ref[...] = x_ref[...] + _dot(o, wo_ref[...])


def _rk_post(x, y, bonus, v, g, ln_w, ln_b, w_o, tm):
    m = x.shape[0]
    row = pl.BlockSpec((tm, D_MODEL), lambda i: (i, 0))
    vec = _const_spec((1, D_MODEL))
    return pl.pallas_call(
        _rk_post_kernel,
        grid=(m // tm,),
        in_specs=[row, row, pl.BlockSpec((tm, PAIR), lambda i: (i, 0)), row, row, vec, vec,
                  _const_spec((D_MODEL, D_MODEL))],
        out_specs=row,
        out_shape=jax.ShapeDtypeStruct((m, D_MODEL), F32),
        compiler_params=_params(("parallel",)),
        name="rk_post",
    )(x, y, bonus, v, g, ln_w.reshape(1, D_MODEL), ln_b.reshape(1, D_MODEL), w_o)


def _conv_core(x, h, shifted, win_ref, cw_ref, wout_ref):
    z = _dot(h.astype(BF16), win_ref[...])
    gate_b = z[:, 0:D_MODEL]
    u = z[:, D_MODEL:2 * D_MODEL] * z[:, 2 * D_MODEL:3 * D_MODEL]
    cw = cw_ref[...]
    u1, u2 = shifted(u)
    conv = u2 * cw[0:1, :] + u1 * cw[1:2, :] + u * cw[2:3, :]
    out = x + _dot((gate_b * conv).astype(BF16), wout_ref[...])
    return out, u


def _conv_prompt_kernel(x_ref, h_ref, c0_ref, win_ref, cw_ref, wout_ref, o_ref, st_ref, carry_ref, *, tm):
    @pl.when(pl.program_id(1) == 0)
    def _():
        carry_ref[0:2, :] = c0_ref[...]

    def shifted(u):
        row = lax.broadcasted_iota(jnp.int32, u.shape, 0)
        c0 = carry_ref[0:1, :]
        c1 = carry_ref[1:2, :]
        u1 = jnp.where(row == 0, c1, pltpu.roll(u, 1, axis=0))
        u2 = jnp.where(row == 0, c0, jnp.where(row == 1, c1, pltpu.roll(u, 2, axis=0)))
        return u1, u2

    out, u = _conv_core(x_ref[...], h_ref[...], shifted, win_ref, cw_ref, wout_ref)
    o_ref[...] = out
    carry_ref[0:2, :] = u[tm - 2:tm, :]
    st_ref[...] = u[tm - SUBLANES:tm, :]


def _conv_sample_kernel(x_ref, h_ref, c0_ref, win_ref, cw_ref, wout_ref, o_ref, u_ref, *, ns, t_len):
    rows = ns * t_len
    c0 = jnp.broadcast_to(c0_ref[:, 0:1, :], (ns, t_len, D_MODEL)).reshape(rows, D_MODEL)
    c1 = jnp.broadcast_to(c0_ref[:, 1:2, :], (ns, t_len, D_MODEL)).reshape(rows, D_MODEL)

    def shifted(u):
        t = lax.broadcasted_iota(jnp.int32, u.shape, 0) & (t_len - 1)
        u1 = jnp.where(t == 0, c1, pltpu.roll(u, 1, axis=0))
        u2 = jnp.where(t == 0, c0, jnp.where(t == 1, c1, pltpu.roll(u, 2, axis=0)))
        return u1, u2

    out, u = _conv_core(x_ref[...], h_ref[...], shifted, win_ref, cw_ref, wout_ref)
    o_ref[...] = out
    u_ref[...] = u


def _conv_mix(x, hn, conv0, w_in, conv_w, w_out, bsz, t_len, t_valid, tm):
    m = bsz * t_len
    consts = [_const_spec((D_MODEL, 3 * D_MODEL)), _const_spec((CONV_W, D_MODEL)), _const_spec((D_MODEL, D_MODEL))]
    if t_len > tm:
        assert t_valid == t_len
        nt = t_len // tm
        row = pl.BlockSpec((tm, D_MODEL), lambda b, t: (b * nt + t, 0))
        out, tail = pl.pallas_call(
            functools.partial(_conv_prompt_kernel, tm=tm),
            grid=(bsz, nt),
            in_specs=[row, row, pl.BlockSpec((None, CONV_W - 1, D_MODEL), lambda b, t: (b, 0, 0))] + consts,
            out_specs=[row, pl.BlockSpec((None, SUBLANES, D_MODEL), lambda b, t: (b, 0, 0))],
            out_shape=[jax.ShapeDtypeStruct((m, D_MODEL), F32),
                       jax.ShapeDtypeStruct((bsz, SUBLANES, D_MODEL), F32)],
            scratch_shapes=[pltpu.VMEM((SUBLANES, D_MODEL), F32)],
            compiler_params=_params(("arbitrary", "arbitrary")),
            name="conv_prompt",
        )(x, hn, conv0, w_in, conv_w, w_out)
        return out, tail[:, SUBLANES - (CONV_W - 1):, :]
    ns = tm // t_len
    row = pl.BlockSpec((tm, D_MODEL), lambda i: (i, 0))
    out, u = pl.pallas_call(
        functools.partial(_conv_sample_kernel, ns=ns, t_len=t_len),
        grid=(m // tm,),
        in_specs=[row, row, pl.BlockSpec((ns, CONV_W - 1, D_MODEL), lambda i: (i, 0, 0))] + consts,
        out_specs=[row, row],
        out_shape=[jax.ShapeDtypeStruct((m, D_MODEL), F32)] * 2,
        compiler_params=_params(("parallel",)),
        name="conv_sample",
    )(x, hn, conv0, w_in, conv_w, w_out)
    return out, u.reshape(bsz, t_len, D_MODEL)[:, t_valid - (CONV_W - 1):t_valid, :]


def _trunk_head(x, wkv_in, shift_in, p, w00, bsz, t_len, t_valid, tm, cast=()):
    x, hn = _ffn(x, p['ffn_norm'][0, 0], w00, p['mix_norm'][0], tm)
    new_shift = hn.reshape(bsz, t_len, D_MODEL)[:, t_valid - 1]
    r, lw, k, v, kk, b, g, bonus = _rk_pre(hn, shift_in[0], p['rk_pre'], bsz, t_len, tm)
    y, new_wkv, cast_out = _wkv([r, lw, k, v, kk, b], wkv_in[0], bsz, t_len, t_valid, cast)
    return (x, y, bonus, v, g), new_wkv[None], new_shift[None], cast_out


def _trunk_tail(acts, conv_in, p, w_rest, bsz, t_len, t_valid, tm, want_y=True):
    x, y, bonus, v, g = acts
    norm = p['ffn_norm']
    tm_wide = 2 * tm if t_len >= 2 * tm else tm
    x = _rk_post(x, y, bonus, v, g, p['rk_ln_w'][0], p['rk_ln_b'][0], p['rk_w_o'][0], tm_wide)
    (x,) = _ffn(x, norm[0, 1], w_rest[0], norm[1, 0], tm, emit_n=False)
    x, hn = _ffn(x, norm[1, 0], w_rest[1], p['mix_norm'][1], tm, n_dtype=BF16)
    x, new_conv = _conv_mix(x, hn, conv_in[0], p['sc_w_in'][0], p['sc_conv_w'][0], p['sc_w_out'][0],
                            bsz, t_len, t_valid, tm_wide)
    if not want_y:
        return None, new_conv[None]
    (y_out,) = _ffn(x, norm[1, 1], w_rest[2], p['final_norm'], tm, emit_x=False)
    return y_out, new_conv[None]


def kernel(x_prompt, x_sample, state_wkv, state_shift, state_conv, meta, ffn_norm, ffn_w_gu, ffn_w_down,
           mix_norm, final_norm, rk_mu, rk_w_rkv, rk_w0, rk_w1, rk_w2, rk_a0, rk_a1, rk_a2, rk_g1, rk_g2,
           rk_k_k, rk_k_a, rk_r_k, rk_ln_w, rk_ln_b, rk_w_o, sc_w_in, sc_conv_w, sc_w_out):
    assert DEPTH == 2 and rk_mu.shape[0] == 1 and sc_w_in.shape[0] == 1
    bf = lambda a: a.astype(BF16)
    vec = lambda a: a.reshape(1, D_MODEL)
    p = {
        'ffn_norm': ffn_norm, 'mix_norm': mix_norm, 'final_norm': final_norm,
        'rk_pre': [rk_mu[0], bf(rk_w_rkv[0, 0]), bf(rk_w_rkv[0, 1]), bf(rk_w_rkv[0, 2]), vec(rk_w0[0]),
                   bf(rk_w1[0]), bf(rk_w2[0]), vec(rk_a0[0]), bf(rk_a1[0]), bf(rk_a2[0]), bf(rk_g1[0]),
                   bf(rk_g2[0]), vec(rk_k_k[0]), vec(rk_k_a[0]), rk_r_k.reshape(1, D_MODEL)],
        'rk_ln_w': rk_ln_w, 'rk_ln_b': rk_ln_b, 'rk_w_o': bf(rk_w_o),
        'sc_w_in': bf(sc_w_in), 'sc_conv_w': sc_conv_w, 'sc_w_out': bf(sc_w_out),
    }
    w00 = (bf(ffn_w_gu[0, 0])[None], bf(ffn_w_down[0, 0])[None], 0)
    later = ((ffn_w_gu.reshape(2 * DEPTH * D_MODEL, 2 * D_FF), D_MODEL),
             (ffn_w_down.reshape(2 * DEPTH * D_FF, D_MODEL), D_FF))
    bs, t_s, _ = x_sample.shape
    bp, seq, _ = x_prompt.shape
    meta_rows = jnp.pad(meta.astype(x_prompt.dtype), ((0, WKV_CHUNK - N_META), (0, 0)))
    zero_wkv = jnp.zeros((1, 1, N_HEADS, HEAD_DIM, HEAD_DIM), state_wkv.dtype)
    zero_shift = jnp.zeros((1, 1, D_MODEL), state_shift.dtype)
    zero_conv = jnp.zeros((1, 1, CONV_W - 1, D_MODEL), state_conv.dtype)
    to_batch = lambda s: jnp.broadcast_to(s, (1, bp) + s.shape[2:])
    acts_m, wkv_m, shift_m, _ = _trunk_head(meta_rows, zero_wkv, zero_shift, p, w00, 1, WKV_CHUNK, N_META,
                                            WKV_CHUNK)
    acts_p, wkv_p, shift_p, (wgu_rest, wd_rest) = _trunk_head(
        x_prompt.reshape(bp * seq, D_MODEL), to_batch(wkv_m), to_batch(shift_m), p, w00, bp, seq, seq,
        ROW_TILE_LONG, cast=later)
    acts_s, wkv_s, shift_s, _ = _trunk_head(x_sample.reshape(bs * t_s, D_MODEL), state_wkv, state_shift, p, w00,
                                            bs, t_s, t_s, ROW_TILE_SHORT)
    wgu_rest = wgu_rest.reshape(2 * DEPTH - 1, D_MODEL, 2 * D_FF)
    wd_rest = wd_rest.reshape(2 * DEPTH - 1, D_FF, D_MODEL)
    w_rest = [(wgu_rest, wd_rest, i) for i in range(2 * DEPTH - 1)]
    _, conv_m = _trunk_tail(acts_m, zero_conv, p, w_rest, 1, WKV_CHUNK, N_META, WKV_CHUNK, want_y=False)
    yp, conv_p = _trunk_tail(acts_p, to_batch(conv_m), p, w_rest, bp, seq, seq, ROW_TILE_LONG)
    ys, conv_s = _trunk_tail(acts_s, state_conv, p, w_rest, bs, t_s, t_s, ROW_TILE_SHORT)
    return (yp.reshape(bp, seq, D_MODEL), ys.reshape(bs, t_s, D_MODEL), wkv_p, shift_p, conv_p,
            wkv_s, shift_s, conv_s)
```

```python
import functools
import math

import jax
import jax.numpy as jnp
from jax import lax
from jax.experimental import pallas as pl
from jax.experimental.pallas import tpu as pltpu

D_MODEL = 1024
HEAD_DIM = 64
N_HEADS = D_MODEL // HEAD_DIM
D_FF = 2816
N_META = 16
DEPTH = 2
CONV_W = 3
RMS_EPS = 1e-6
GN_EPS = 64e-5
DECAY_SCALE = math.exp(-0.5)

SUBLANES = 8
BF16_ROWS = 16
MXU_DIM = 256
PAIR = 2 * HEAD_DIM
HEAD_SHIFT = HEAD_DIM.bit_length() - 1
KK_FLOOR = 1e-12
WKV_CHUNK = 64
STACK_ROWS = 2 * WKV_CHUNK
WKV_BATCH = 4
N_PAIRS = D_MODEL // PAIR
VMEM_LIMIT = 56 * 1024 * 1024
ROW_TILE = 512

FFN_CHUNKS = ((0, 6 * MXU_DIM), (6 * MXU_DIM, D_FF))

F32 = jnp.float32
BF16 = jnp.bfloat16


def _dot(a, b):
    return jnp.dot(a, b, preferred_element_type=F32)


def _dot_nt(a, b):
    return lax.dot_general(a, b, (((1,), (1,)), ((), ())), preferred_element_type=F32)


def _dot_tn(a, b):
    return lax.dot_general(a, b, (((0,), (0,)), ((), ())), preferred_element_type=F32)


def _rms(x, g):
    ms = jnp.mean(x * x, axis=-1, keepdims=True)
    return x * lax.rsqrt(ms + RMS_EPS) * g


def _sigmoid(x):
    return 1.0 / (1.0 + jnp.exp(-x))


def _head_ones():
    r = lax.broadcasted_iota(jnp.int32, (PAIR, PAIR), 0) >> HEAD_SHIFT
    c = lax.broadcasted_iota(jnp.int32, (PAIR, PAIR), 1) >> HEAD_SHIFT
    return jnp.where(r == c, 1.0, 0.0).astype(BF16)


def _head_sum(x, ones):
    xb = x.astype(BF16)
    return jnp.concatenate([_dot(xb[:, j * PAIR:(j + 1) * PAIR], ones) for j in range(N_PAIRS)], axis=1)


def _const_spec(shape):
    nd = len(shape)
    return pl.BlockSpec(shape, lambda *_: (0,) * nd, pipeline_mode=pl.Buffered(1))


def _params(sem):
    return pltpu.CompilerParams(dimension_semantics=sem, vmem_limit_bytes=VMEM_LIMIT)


def _ffn_kernel(g_ref, wgu_ref, wd_ref, g2_ref, *refs, bounds, emit_x, emit_n):
    n_out = int(emit_x) + int(emit_n)
    x_refs, out_refs = refs[:len(bounds)], refs[len(bounds):]

    def tile(x_ref, outs):
        x = x_ref[...]
        xn = _rms(x, g_ref[...]).astype(BF16)
        acc = None
        for lo, hi in FFN_CHUNKS:
            gate = _dot(xn, wgu_ref[:, lo:hi])
            up = _dot(xn, wgu_ref[:, D_FF + lo:D_FF + hi])
            act = (gate * _sigmoid(gate) * up).astype(BF16)
            part = _dot(act, wd_ref[lo:hi, :])
            acc = part if acc is None else acc + part
        out = x + 0.5 * acc
        if emit_x:
            outs[0][...] = out
        if emit_n:
            outs[-1][...] = _rms(out, g2_ref[...]).astype(outs[-1].dtype)

    step = pl.program_id(0)
    for s, (lo, hi) in enumerate(bounds):
        args = (x_refs[s], out_refs[s * n_out:(s + 1) * n_out])
        if len(bounds) == 1:
            tile(*args)
        else:
            pl.when((step >= lo) & (step < hi))(functools.partial(tile, *args))


def _ffn(xs, g, w, g2, tm, emit_x=True, emit_n=True, n_dtype=F32):
    wgu, wd, idx = w
    n_tiles = [x.shape[0] // tm for x in xs]
    bounds, lo = [], 0
    for nt in n_tiles:
        bounds.append((lo, lo + nt))
        lo += nt
    dtypes = [F32] * int(emit_x) + [n_dtype] * int(emit_n)

    def weight_spec(rows, cols):
        return pl.BlockSpec((None, rows, cols), lambda i: (idx, 0, 0), pipeline_mode=pl.Buffered(1))

    def row_spec(first, nt):
        return pl.BlockSpec((tm, D_MODEL), lambda i: (jnp.clip(i - first, 0, nt - 1), 0))

    rows = [row_spec(b[0], nt) for b, nt in zip(bounds, n_tiles)]
    vec = _const_spec((1, D_MODEL))
    outs = pl.pallas_call(
        functools.partial(_ffn_kernel, bounds=tuple(bounds), emit_x=emit_x, emit_n=emit_n),
        grid=(lo,),
        in_specs=[vec, weight_spec(D_MODEL, 2 * D_FF), weight_spec(D_FF, D_MODEL), vec] + rows,
        out_specs=[r for r in rows for _ in dtypes],
        out_shape=[jax.ShapeDtypeStruct(x.shape, dt) for x in xs for dt in dtypes],
        compiler_params=_params(("arbitrary",)),
        name="ffn",
    )(g.reshape(1, D_MODEL), wgu, wd, g2.reshape(1, D_MODEL), *xs)
    return [tuple(outs[s * len(dtypes):(s + 1) * len(dtypes)]) for s in range(len(xs))]


N_RK_IN = 15
N_RK_OUT = 8


def _head_select(transpose=False):
    shape = (PAIR, D_MODEL) if transpose else (D_MODEL, PAIR)
    chan = lax.broadcasted_iota(jnp.int32, shape, 1 if transpose else 0) >> HEAD_SHIFT
    head = lax.broadcasted_iota(jnp.int32, shape, 0 if transpose else 1)
    return jnp.where(chan == head, 1.0, 0.0).astype(BF16)


def _rk_pre_core(h, prev, p, out_refs):
    (mu, wr, wk, wv, w0, w1, w2, a0, a1, a2, g1, g2, k_k, k_a, r_k) = [r[...] for r in p]
    xx = prev - h
    xr, xw, xk, xv, xa, xg = [(h + xx * mu[c:c + 1, :]).astype(BF16) for c in range(6)]
    r = _dot(xr, wr)
    k = _dot(xk, wk)
    v = _dot(xv, wv)
    z = w0 + _dot(jnp.tanh(_dot(xw, w1)).astype(BF16), w2)
    lw = -DECAY_SCALE * _sigmoid(z)
    a = _sigmoid(a0 + _dot(_dot(xa, a1).astype(BF16), a2))
    g = _dot(_sigmoid(_dot(xg, g1)).astype(BF16), g2)
    kk = k * k_k
    ss = _head_sum(kk * kk, _head_ones())
    kk = kk * lax.rsqrt(jnp.maximum(ss, KK_FLOOR * KK_FLOOR))
    kmod = k * (1.0 + (a - 1.0) * k_a)
    bonus = _dot((r * kmod * r_k).astype(BF16), _head_select())
    for ref, val in zip(out_refs, (r, lw, kmod, v, kk, kk * a, g, bonus)):
        ref[...] = val.astype(ref.dtype)


def _rk_pre_prompt_kernel(h_ref, s0_ref, *refs, tm):
    p, out_refs, carry_ref = refs[:N_RK_IN], refs[N_RK_IN:N_RK_IN + N_RK_OUT], refs[N_RK_IN + N_RK_OUT]

    @pl.when(pl.program_id(1) == 0)
    def _():
        carry_ref[0:1, :] = s0_ref[...]

    h = h_ref[...]
    row = lax.broadcasted_iota(jnp.int32, h.shape, 0)
    prev = jnp.where(row == 0, carry_ref[0:1, :], pltpu.roll(h, 1, axis=0))
    carry_ref[0:1, :] = h[tm - 1:tm, :]
    _rk_pre_core(h, prev, p, out_refs)


def _rk_pre_sample_kernel(h_ref, s0_ref, *refs, ns, t_len):
    p, out_refs = refs[:N_RK_IN], refs[N_RK_IN:N_RK_IN + N_RK_OUT]
    h = h_ref[...]
    rows = ns * t_len
    s0 = jnp.broadcast_to(s0_ref[...], (ns, t_len, D_MODEL)).reshape(rows, D_MODEL)
    t = lax.broadcasted_iota(jnp.int32, h.shape, 0) & (t_len - 1)
    prev = jnp.where(t == 0, s0, pltpu.roll(h, 1, axis=0))
    _rk_pre_core(h, prev, p, out_refs)


def _rk_param_specs(p):
    return [_const_spec(a.shape) for a in p]


def _rk_pre(hn, shift0, p, bsz, t_len, tm):
    m = bsz * t_len
    assert len(p) == N_RK_IN
    out_shape = ([jax.ShapeDtypeStruct((m, D_MODEL), F32 if i == 1 else BF16) for i in range(N_RK_OUT - 1)]
                 + [jax.ShapeDtypeStruct((m, PAIR), F32)])
    s0 = shift0.reshape(bsz, 1, D_MODEL)
    if t_len > tm:
        nt = t_len // tm
        row = pl.BlockSpec((tm, D_MODEL), lambda b, t: (b * nt + t, 0))
        return pl.pallas_call(
            functools.partial(_rk_pre_prompt_kernel, tm=tm),
            grid=(bsz, nt),
            in_specs=[row, pl.BlockSpec((None, 1, D_MODEL), lambda b, t: (b, 0, 0))] + _rk_param_specs(p),
            out_specs=[row] * (N_RK_OUT - 1) + [pl.BlockSpec((tm, PAIR), lambda b, t: (b * nt + t, 0))],
            out_shape=out_shape,
            scratch_shapes=[pltpu.VMEM((SUBLANES, D_MODEL), F32)],
            compiler_params=_params(("arbitrary", "arbitrary")),
            name="rk_pre_prompt",
        )(hn, s0, *p)
    ns = tm // t_len
    row = pl.BlockSpec((tm, D_MODEL), lambda i: (i, 0))
    return pl.pallas_call(
        functools.partial(_rk_pre_sample_kernel, ns=ns, t_len=t_len),
        grid=(m // tm,),
        in_specs=[row, pl.BlockSpec((ns, 1, D_MODEL), lambda i: (i, 0, 0))] + _rk_param_specs(p),
        out_specs=[row] * (N_RK_OUT - 1) + [pl.BlockSpec((tm, PAIR), lambda i: (i, 0))],
        out_shape=out_shape,
        compiler_params=_params(("parallel",)),
        name="rk_pre_sample",
    )(hn, s0, *p)


def _wkv_masks(c):
    rows = STACK_ROWS
    sh = c.bit_length() - 1
    row = lax.broadcasted_iota(jnp.int32, (rows, rows), 0)
    col = lax.broadcasted_iota(jnp.int32, (rows, rows), 1)
    same = (row >> sh) == (col >> sh)
    strict = same & (row > col)
    incl = same & (row >= col)
    base = (row >> 1) == (col >> 1)
    merges = []
    s = 2
    while s < c:
        b = s.bit_length() - 1
        merges.append(((row >> (b + 1)) == (col >> (b + 1))) & (((row >> b) & 1) == 1) & (((col >> b) & 1) == 0))
        s *= 2
    eye = jnp.where(row == col, 1.0, 0.0).astype(F32)
    return strict, incl, base, merges, eye


def _cumsum_rows(x, c):
    t = lax.broadcasted_iota(jnp.int32, x.shape, 0)
    s = 1
    while s < c:
        x = x + jnp.where(t >= s, pltpu.roll(x, s, axis=0), 0.0)
        s *= 2
    return x


def _wkv_units(ins, states, c, masks):
    strict, incl, base, merges, eye = masks
    nseq = len(ins[0])
    rows = STACK_ROWS
    assert nseq * 2 * c == rows
    lane_h = lax.broadcasted_iota(jnp.int32, (c, PAIR), 1) >> HEAD_SHIFT
    seq_rows = [slice(i * 2 * c, (i + 1) * 2 * c) for i in range(nseq)]

    def stack(xs):
        return jnp.concatenate([jnp.where(lane_h == h, x, 0.0) for x in xs for h in range(2)],
                               axis=0).astype(BF16)

    def each(f, *lists):
        return [f(*xs) for xs in zip(*lists)]

    def prep(seqs):
        per = []
        for r, lw, k, v, kk, b in seqs:
            cum = _cumsum_rows(lw, c)
            last = cum[c - 1:c, :]
            to_end = jnp.exp(last - cum)
            p_inv = jnp.exp(-cum)
            per.append(dict(p_c=jnp.exp(last), kk=kk * jnp.exp(cum - lw), r=r * jnp.exp(cum), k=k * p_inv,
                            b=b * p_inv, v=v, kd=k * to_end, nbd=-(b * to_end)))
        out = {name: stack([d[name] for d in per]) for name in ('kk', 'r', 'k', 'b', 'v', 'kd', 'nbd')}
        out['p_c'] = [d['p_c'] for d in per]
        return out

    o = [prep(x) for x in ins]
    kb = [jnp.concatenate([x['k'], x['b']], axis=0) for x in o]
    a_kk = each(lambda x, w_: _dot_nt(x['kk'], w_), o, kb)
    a_r = each(lambda x, w_: _dot_nt(x['r'], w_), o, kb)
    a_kk_k, a_kk_b = [a[:, :rows] for a in a_kk], [a[:, rows:] for a in a_kk]
    a_r_k, a_r_b = [a[:, :rows] for a in a_r], [a[:, rows:] for a in a_r]

    lm = [jnp.where(strict, a, 0.0) for a in a_kk_b]
    inv = [eye - jnp.where(base, l, 0.0) for l in lm]
    size = 2
    for m in merges:
        if size >= SUBLANES:
            starts = range(size, rows, 2 * size)

            def gather(x, starts=starts, size=size):
                return jnp.concatenate([x[r0:r0 + size] for r0 in starts], axis=0)

            def scatter(xh, n=len(starts), size=size):
                zero = jnp.zeros((size, rows), F32)
                return jnp.concatenate([p_ for j in range(n) for p_ in (zero, xh[j * size:(j + 1) * size])],
                                       axis=0)

            inv_b = [t.astype(BF16) for t in inv]
            mid = each(lambda l, t: scatter(_dot(gather(jnp.where(m, l, 0.0)).astype(BF16), t)).astype(BF16),
                       lm, inv_b)
            inv = each(lambda t, md: t - scatter(_dot(gather(t).astype(BF16), md)), inv, mid)
        else:
            inv_b = [t.astype(BF16) for t in inv]
            mid = each(lambda l, t: _dot(jnp.where(m, l, 0.0).astype(BF16), t).astype(BF16), lm, inv_b)
            inv = each(lambda t, tb, md: t - _dot(tb, md), inv, inv_b, mid)
        size *= 2

    def state_dot(x, s_list):
        outs = [_dot_nt(jnp.concatenate([x['kk'][rs], x['r'][rs]], axis=0), s.astype(BF16))
                for rs, s in zip(seq_rows, s_list)]
        h = 2 * c
        if nseq == 1:
            return outs[0][:h], outs[0][h:]
        return (jnp.concatenate([t[:h] for t in outs], axis=0), jnp.concatenate([t[h:] for t in outs], axis=0))

    sd = each(state_dot, o, states)
    kkh, rh = [t[0] for t in sd], [t[1] for t in sd]
    rhs = each(lambda h, a, x: h + _dot(jnp.where(strict, a, 0.0).astype(BF16), x['v']), kkh, a_kk_k, o)
    u_b = each(lambda t, z: _dot(t.astype(BF16), z.astype(BF16)).astype(BF16), inv, rhs)
    y_s = each(lambda h, ak, ab, x, u: h + _dot(jnp.where(incl, ak, 0.0).astype(BF16), x['v'])
               - _dot(jnp.where(incl, ab, 0.0).astype(BF16), u), rh, a_r_k, a_r_b, o, u_b)

    def fold(ys):
        return [ys[rs][:c] + ys[rs][c:] for rs in seq_rows]

    def new_state(x, u, s_list):
        return [s * pc + _dot_tn(jnp.concatenate([x['v'][rs], u[rs]], axis=0),
                                 jnp.concatenate([x['kd'][rs], x['nbd'][rs]], axis=0))
                for rs, s, pc in zip(seq_rows, s_list, x['p_c'])]

    return [fold(ys) for ys in y_s], each(new_state, o, u_b, states)


def _rows_to_blockdiag(s):
    ext = jnp.concatenate([s, jnp.zeros_like(s)], axis=1)
    row = lax.broadcasted_iota(jnp.int32, ext.shape, 0)
    return jnp.where(row < HEAD_DIM, ext, pltpu.roll(ext, HEAD_DIM, axis=1))


def _blockdiag_to_rows(s_bd):
    row = lax.broadcasted_iota(jnp.int32, s_bd.shape, 0)
    return jnp.where(row < HEAD_DIM, s_bd, pltpu.roll(s_bd, HEAD_DIM, axis=1))[:, :HEAD_DIM]


def _wkv_prompt_kernel(r_ref, lw_ref, k_ref, v_ref, kk_ref, b_ref, s0_ref, *refs, c, t_len, nb, n_cast):
    cast_in, (y_ref, sout_ref), cast_out, s_ref = (refs[:n_cast], refs[n_cast:n_cast + 2],
                                                   refs[n_cast + 2:2 * n_cast + 2], refs[2 * n_cast + 2])
    for w_in, w_out in zip(cast_in, cast_out):
        w_out[...] = w_in[...].astype(BF16)
    ci = pl.program_id(1)
    units = [(i, q) for i in range(nb) for q in range(N_PAIRS)]

    @pl.when(ci == 0)
    def _():
        for i, q in units:
            s_ref[i, q] = _rows_to_blockdiag(s0_ref[i, q])

    masks = _wkv_masks(c)
    lanes = [slice(q * PAIR, (q + 1) * PAIR) for q in range(N_PAIRS)]

    def load(ref, i, q):
        x = ref[i, :, lanes[q]].astype(F32)
        if t_len % c:
            valid = lax.broadcasted_iota(jnp.int32, (c, PAIR), 0) + ci * c < t_len
            x = jnp.where(valid, x, 0.0)
        return x

    ins = [[[load(ref, i, q) for ref in (r_ref, lw_ref, k_ref, v_ref, kk_ref, b_ref)]] for i, q in units]
    ys, s_new = _wkv_units(ins, [[s_ref[i, q]] for i, q in units], c, masks)
    for (i, q), y, s in zip(units, ys, s_new):
        y_ref[i, :, lanes[q]] = y[0].astype(y_ref.dtype)
        s_ref[i, q] = s[0]

    @pl.when(ci == pl.num_programs(1) - 1)
    def _():
        for i, q in units:
            sout_ref[i, q] = _blockdiag_to_rows(s_ref[i, q])


def _wkv_sample_kernel(r_ref, lw_ref, k_ref, v_ref, kk_ref, b_ref, s0_ref, y_ref, sout_ref, *, c, nseq):
    masks = _wkv_masks(c)
    lanes = [slice(q * PAIR, (q + 1) * PAIR) for q in range(N_PAIRS)]
    full = [ref[...].astype(F32) for ref in (r_ref, lw_ref, k_ref, v_ref, kk_ref, b_ref)]
    ins = [[[a[i * c:(i + 1) * c, lanes[q]] for a in full] for i in range(nseq)] for q in range(N_PAIRS)]
    states = [[_rows_to_blockdiag(s0_ref[i, q]) for i in range(nseq)] for q in range(N_PAIRS)]
    ys, s_new = _wkv_units(ins, states, c, masks)
    for q in range(N_PAIRS):
        for i in range(nseq):
            y_ref[i * c:(i + 1) * c, lanes[q]] = ys[q][i]
            sout_ref[i, q] = _blockdiag_to_rows(s_new[q][i])


def _cast_slabs(w, skip, n_steps, nc):
    n_rows = w.shape[0] - skip
    slab = next(r for r in range(BF16_ROWS, n_rows + 1, BF16_ROWS)
                if n_rows % r == 0 and skip % r == 0 and n_rows // r <= n_steps)
    last = n_rows // slab - 1
    first = skip // slab
    in_spec = pl.BlockSpec((slab, w.shape[1]), lambda b, i: (first + jnp.minimum(b * nc + i, last), 0))
    out_spec = pl.BlockSpec((slab, w.shape[1]), lambda b, i: (jnp.minimum(b * nc + i, last), 0))
    return in_spec, out_spec, jax.ShapeDtypeStruct((n_rows, w.shape[1]), BF16)


def _wkv(ins, state, bsz, t_len, t_valid, cast=()):
    m = bsz * t_len
    y_shape = jax.ShapeDtypeStruct((m, D_MODEL), F32)
    s_rows = state.reshape(bsz, N_PAIRS, PAIR, HEAD_DIM)
    s_shape = jax.ShapeDtypeStruct(s_rows.shape, F32)
    if t_len >= WKV_CHUNK:
        c, nb = WKV_CHUNK, min(bsz, WKV_BATCH)
        nc = t_len // c
        ins3 = [a.reshape(bsz, t_len, D_MODEL) for a in ins]
        blk = pl.BlockSpec((nb, c, D_MODEL), lambda b, i: (b, i, 0))
        sblk = pl.BlockSpec((nb, N_PAIRS, PAIR, HEAD_DIM), lambda b, i: (b, 0, 0, 0))
        slabs = [_cast_slabs(w, skip, (bsz // nb) * nc, nc) for w, skip in cast]
        y, s_out, *cast_out = pl.pallas_call(
            functools.partial(_wkv_prompt_kernel, c=c, t_len=t_valid, nb=nb, n_cast=len(cast)),
            grid=(bsz // nb, nc),
            in_specs=[blk] * 6 + [sblk] + [s[0] for s in slabs],
            out_specs=[blk, sblk] + [s[1] for s in slabs],
            out_shape=[jax.ShapeDtypeStruct((bsz, t_len, D_MODEL), BF16), s_shape] + [s[2] for s in slabs],
            scratch_shapes=[pltpu.VMEM((nb, N_PAIRS, PAIR, PAIR), F32)],
            compiler_params=_params(("arbitrary", "arbitrary")),
            name="wkv_prompt",
        )(*ins3, s_rows, *[w for w, _ in cast])
        return y.reshape(m, D_MODEL), s_out.reshape(state.shape), cast_out
    assert t_valid == t_len and not cast
    c, nseq = t_len, WKV_CHUNK // t_len
    blk = pl.BlockSpec((nseq * c, D_MODEL), lambda i: (i, 0))
    sblk = pl.BlockSpec((nseq, N_PAIRS, PAIR, HEAD_DIM), lambda i: (i, 0, 0, 0))
    y, s_out = pl.pallas_call(
        functools.partial(_wkv_sample_kernel, c=c, nseq=nseq),
        grid=(bsz // nseq,),
        in_specs=[blk] * 6 + [sblk],
        out_specs=[blk, sblk],
        out_shape=[y_shape, s_shape],
        compiler_params=_params(("parallel",)),
        name="wkv_sample",
    )(*ins, s_rows)
    return y, s_out.reshape(state.shape), []


def _rk_post_kernel(x_ref, y_ref, bonus_ref, v_ref, g_ref, lnw_ref, lnb_ref, wo_ref, o_ref):
    ones = _head_ones()
    y = y_ref[...].astype(F32)
    inv_n = 1.0 / HEAD_DIM
    mean = _head_sum(y, ones) * inv_n
    yc = y - mean
    var = _head_sum(yc * yc, ones) * inv_n
    yn = yc * lax.rsqrt(var + GN_EPS) * lnw_ref[...] + lnb_ref[...]
    coef = bonus_ref[...]
    hi = coef.astype(BF16)
    lo = (coef - hi.astype(F32)).astype(BF16)
    spread = _head_select(transpose=True)
    bonus = (_dot(hi, spread) + _dot(lo, spread)) * v_ref[...].astype(F32)
    o = ((yn + bonus) * g_ref[...].astype(F32)).astype(BF16)
    o_ref[...] = x_ref[...] + _dot(o, wo_ref[...])


def _rk_post(x, y, bonus, v, g, ln_w, ln_b, w_o, tm):
    m = x.shape[0]
    row = pl.BlockSpec((tm, D_MODEL), lambda i: (i, 0))
    vec = _const_spec((1, D_MODEL))
    return pl.pallas_call(
        _rk_post_kernel,
        grid=(m // tm,),
        in_specs=[row, row, pl.BlockSpec((tm, PAIR), lambda i: (i, 0)), row, row, vec, vec,
                  _const_spec((D_MODEL, D_MODEL))],
        out_specs=row,
        out_shape=jax.ShapeDtypeStruct((m, D_MODEL), F32),
        compiler_params=_params(("parallel",)),
        name="rk_post",
    )(x, y, bonus, v, g, ln_w.reshape(1, D_MODEL), ln_b.reshape(1, D_MODEL), w_o)


def _conv_core(x, h, shifted, win_ref, cw_ref, wout_ref):
    z = _dot(h.astype(BF16), win_ref[...])
    gate_b = z[:, 0:D_MODEL]
    u = z[:, D_MODEL:2 * D_MODEL] * z[:, 2 * D_MODEL:3 * D_MODEL]
    cw = cw_ref[...]
    u1, u2 = shifted(u)
    conv = u2 * cw[0:1, :] + u1 * cw[1:2, :] + u * cw[2:3, :]
    out = x + _dot((gate_b * conv).astype(BF16), wout_ref[...])
    return out, u


def _conv_prompt_kernel(x_ref, h_ref, c0_ref, win_ref, cw_ref, wout_ref, o_ref, st_ref, carry_ref, *, tm):
    @pl.when(pl.program_id(1) == 0)
    def _():
        carry_ref[0:2, :] = c0_ref[...]

    def shifted(u):
        row = lax.broadcasted_iota(jnp.int32, u.shape, 0)
        c0 = carry_ref[0:1, :]
        c1 = carry_ref[1:2, :]
        u1 = jnp.where(row == 0, c1, pltpu.roll(u, 1, axis=0))
        u2 = jnp.where(row == 0, c0, jnp.where(row == 1, c1, pltpu.roll(u, 2, axis=0)))
        return u1, u2

    out, u = _conv_core(x_ref[...], h_ref[...], shifted, win_ref, cw_ref, wout_ref)
    o_ref[...] = out
    carry_ref[0:2, :] = u[tm - 2:tm, :]
    st_ref[...] = u[tm - SUBLANES:tm, :]


def _conv_sample_kernel(x_ref, h_ref, c0_ref, win_ref, cw_ref, wout_ref, o_ref, u_ref, *, ns, t_len):
    rows = ns * t_len
    c0 = jnp.broadcast_to(c0_ref[:, 0:1, :], (ns, t_len, D_MODEL)).reshape(rows, D_MODEL)
    c1 = jnp.broadcast_to(c0_ref[:, 1:2, :], (ns, t_len, D_MODEL)).reshape(rows, D_MODEL)

    def shifted(u):
        t = lax.broadcasted_iota(jnp.int32, u.shape, 0) & (t_len - 1)
        u1 = jnp.where(t == 0, c1, pltpu.roll(u, 1, axis=0))
        u2 = jnp.where(t == 0, c0, jnp.where(t == 1, c1, pltpu.roll(u, 2, axis=0)))
        return u1, u2

    out, u = _conv_core(x_ref[...], h_ref[...], shifted, win_ref, cw_ref, wout_ref)
    o_ref[...] = out
    u_ref[...] = u


def _conv_mix(x, hn, conv0, w_in, conv_w, w_out, bsz, t_len, t_valid, tm):
    m = bsz * t_len
    consts = [_const_spec((D_MODEL, 3 * D_MODEL)), _const_spec((CONV_W, D_MODEL)), _const_spec((D_MODEL, D_MODEL))]
    if t_len > tm:
        assert t_valid == t_len
        nt = t_len // tm
        row = pl.BlockSpec((tm, D_MODEL), lambda b, t: (b * nt + t, 0))
        out, tail = pl.pallas_call(
            functools.partial(_conv_prompt_kernel, tm=tm),
            grid=(bsz, nt),
            in_specs=[row, row, pl.BlockSpec((None, CONV_W - 1, D_MODEL), lambda b, t: (b, 0, 0))] + consts,
            out_specs=[row, pl.BlockSpec((None, SUBLANES, D_MODEL), lambda b, t: (b, 0, 0))],
            out_shape=[jax.ShapeDtypeStruct((m, D_MODEL), F32),
                       jax.ShapeDtypeStruct((bsz, SUBLANES, D_MODEL), F32)],
            scratch_shapes=[pltpu.VMEM((SUBLANES, D_MODEL), F32)],
            compiler_params=_params(("arbitrary", "arbitrary")),
            name="conv_prompt",
        )(x, hn, conv0, w_in, conv_w, w_out)
        return out, tail[:, SUBLANES - (CONV_W - 1):, :]
    ns = tm // t_len
    row = pl.BlockSpec((tm, D_MODEL), lambda i: (i, 0))
    out, u = pl.pallas_call(
        functools.partial(_conv_sample_kernel, ns=ns, t_len=t_len),
        grid=(m // tm,),
        in_specs=[row, row, pl.BlockSpec((ns, CONV_W - 1, D_MODEL), lambda i: (i, 0, 0))] + consts,
        out_specs=[row, row],
        out_shape=[jax.ShapeDtypeStruct((m, D_MODEL), F32)] * 2,
        compiler_params=_params(("parallel",)),
        name="conv_sample",
    )(x, hn, conv0, w_in, conv_w, w_out)
    return out, u.reshape(bsz, t_len, D_MODEL)[:, t_valid - (CONV_W - 1):t_valid, :]


def _trunk_head(streams, p, w00, tm, cast=()):
    heads, cast_out = [], []
    ffn_out = _ffn([s['x'] for s in streams], p['ffn_norm'][0, 0], w00, p['mix_norm'][0], tm)
    for i, (s, (x, hn)) in enumerate(zip(streams, ffn_out)):
        bsz, t_len, t_valid = s['bsz'], s['t_len'], s['t_valid']
        new_shift = hn.reshape(bsz, t_len, D_MODEL)[:, t_valid - 1]
        r, lw, k, v, kk, b, g, bonus = _rk_pre(hn, s['shift'][0], p['rk_pre'], bsz, t_len, tm)
        y, new_wkv, casts = _wkv([r, lw, k, v, kk, b], s['wkv'][0], bsz, t_len, t_valid, cast if i == 0 else ())
        cast_out += casts
        heads.append(dict(s, acts=(x, y, bonus, v, g), new_wkv=new_wkv[None], new_shift=new_shift[None]))
    return heads, cast_out


def _trunk_tail(heads, conv_in, p, w_rest, tm, want_y=True):
    norm = p['ffn_norm']
    xs, tiles = [], []
    for s in heads:
        tiles.append(2 * tm if s['t_len'] >= 2 * tm else tm)
        x, y, bonus, v, g = s['acts']
        xs.append(_rk_post(x, y, bonus, v, g, p['rk_ln_w'][0], p['rk_ln_b'][0], p['rk_w_o'][0], tiles[-1]))
    xs = [o[0] for o in _ffn(xs, norm[0, 1], w_rest[0], norm[1, 0], tm, emit_n=False)]
    mixed = _ffn(xs, norm[1, 0], w_rest[1], p['mix_norm'][1], tm, n_dtype=BF16)
    xs, convs = [], []
    for s, (x, hn), conv0, tile in zip(heads, mixed, conv_in, tiles):
        x, new_conv = _conv_mix(x, hn, conv0[0], p['sc_w_in'][0], p['sc_conv_w'][0], p['sc_w_out'][0],
                                s['bsz'], s['t_len'], s['t_valid'], tile)
        xs.append(x)
        convs.append(new_conv[None])
    ys = [o[0] for o in _ffn(xs, norm[1, 1], w_rest[2], p['final_norm'], tm, emit_x=False)] if want_y \
        else [None] * len(heads)
    return list(zip(ys, convs))


def kernel(x_prompt, x_sample, state_wkv, state_shift, state_conv, meta, ffn_norm, ffn_w_gu, ffn_w_down,
           mix_norm, final_norm, rk_mu, rk_w_rkv, rk_w0, rk_w1, rk_w2, rk_a0, rk_a1, rk_a2, rk_g1, rk_g2,
           rk_k_k, rk_k_a, rk_r_k, rk_ln_w, rk_ln_b, rk_w_o, sc_w_in, sc_conv_w, sc_w_out):
    assert DEPTH == 2 and rk_mu.shape[0] == 1 and sc_w_in.shape[0] == 1
    bf = lambda a: a.astype(BF16)
    vec = lambda a: a.reshape(1, D_MODEL)
    p = {
        'ffn_norm': ffn_norm, 'mix_norm': mix_norm, 'final_norm': final_norm,
        'rk_pre': [rk_mu[0], bf(rk_w_rkv[0, 0]), bf(rk_w_rkv[0, 1]), bf(rk_w_rkv[0, 2]), vec(rk_w0[0]),
                   bf(rk_w1[0]), bf(rk_w2[0]), vec(rk_a0[0]), bf(rk_a1[0]), bf(rk_a2[0]), bf(rk_g1[0]),
                   bf(rk_g2[0]), vec(rk_k_k[0]), vec(rk_k_a[0]), rk_r_k.reshape(1, D_MODEL)],
        'rk_ln_w': rk_ln_w, 'rk_ln_b': rk_ln_b, 'rk_w_o': bf(rk_w_o),
        'sc_w_in': bf(sc_w_in), 'sc_conv_w': sc_conv_w, 'sc_w_out': bf(sc_w_out),
    }
    w00 = (bf(ffn_w_gu[0, 0])[None], bf(ffn_w_down[0, 0])[None], 0)
    later = ((ffn_w_gu.reshape(2 * DEPTH * D_MODEL, 2 * D_FF), D_MODEL),
             (ffn_w_down.reshape(2 * DEPTH * D_FF, D_MODEL), D_FF))
    bs, t_s, _ = x_sample.shape
    bp, seq, _ = x_prompt.shape
    meta_rows = jnp.pad(meta.astype(x_prompt.dtype), ((0, WKV_CHUNK - N_META), (0, 0)))
    zero_wkv = jnp.zeros((1, 1, N_HEADS, HEAD_DIM, HEAD_DIM), state_wkv.dtype)
    zero_shift = jnp.zeros((1, 1, D_MODEL), state_shift.dtype)
    zero_conv = jnp.zeros((1, 1, CONV_W - 1, D_MODEL), state_conv.dtype)
    to_batch = lambda s: jnp.broadcast_to(s, (1, bp) + s.shape[2:])
    meta_stream = dict(x=meta_rows, bsz=1, t_len=WKV_CHUNK, t_valid=N_META, wkv=zero_wkv, shift=zero_shift)
    (head_m,), _ = _trunk_head([meta_stream], p, w00, WKV_CHUNK)
    main = [dict(x=x_prompt.reshape(bp * seq, D_MODEL), bsz=bp, t_len=seq, t_valid=seq,
                 wkv=to_batch(head_m['new_wkv']), shift=to_batch(head_m['new_shift'])),
            dict(x=x_sample.reshape(bs * t_s, D_MODEL), bsz=bs, t_len=t_s, t_valid=t_s,
                 wkv=state_wkv, shift=state_shift)]
    (head_p, head_s), (wgu_rest, wd_rest) = _trunk_head(main, p, w00, ROW_TILE, cast=later)
    wgu_rest = wgu_rest.reshape(2 * DEPTH - 1, D_MODEL, 2 * D_FF)
    wd_rest = wd_rest.reshape(2 * DEPTH - 1, D_FF, D_MODEL)
    w_rest = [(wgu_rest, wd_rest, i) for i in range(2 * DEPTH - 1)]
    ((_, conv_m),) = _trunk_tail([head_m], [zero_conv], p, w_rest, WKV_CHUNK, want_y=False)
    (yp, conv_p), (ys, conv_s) = _trunk_tail([head_p, head_s], [to_batch(conv_m), state_conv], p, w_rest,
                                             ROW_TILE)
    return (yp.reshape(bp, seq, D_MODEL), ys.reshape(bs, t_s, D_MODEL), head_p['new_wkv'], head_p['new_shift'],
            conv_p, head_s['new_wkv'], head_s['new_shift'], conv_s)
```

```python
import functools
import math

import jax
import jax.numpy as jnp
from jax import lax
from jax.experimental import pallas as pl
from jax.experimental.pallas import tpu as pltpu

D_MODEL = 1024
HEAD_DIM = 64
N_HEADS = D_MODEL // HEAD_DIM
D_FF = 2816
N_META = 16
DEPTH = 2
CONV_W = 3
RMS_EPS = 1e-6
GN_EPS = 64e-5
DECAY_SCALE = math.exp(-0.5)

SUBLANES = 8
BF16_ROWS = 16
MXU_DIM = 256
PAIR = 2 * HEAD_DIM
HEAD_SHIFT = HEAD_DIM.bit_length() - 1
KK_FLOOR = 1e-12
WKV_CHUNK = 64
STACK_ROWS = 2 * WKV_CHUNK
WKV_BATCH = 4
WKV_SHORT_GROUPS = 2
N_PAIRS = D_MODEL // PAIR
VMEM_LIMIT = 56 * 1024 * 1024
ROW_TILE = 512

FFN_CHUNKS = ((0, 6 * MXU_DIM), (6 * MXU_DIM, D_FF))

F32 = jnp.float32
BF16 = jnp.bfloat16


def _dot(a, b):
    return jnp.dot(a, b, preferred_element_type=F32)


def _dot_nt(a, b):
    return lax.dot_general(a, b, (((1,), (1,)), ((), ())), preferred_element_type=F32)


def _dot_tn(a, b):
    return lax.dot_general(a, b, (((0,), (0,)), ((), ())), preferred_element_type=F32)


def _rms(x, g):
    ms = jnp.mean(x * x, axis=-1, keepdims=True)
    return x * lax.rsqrt(ms + RMS_EPS) * g


def _sigmoid(x):
    return 1.0 / (1.0 + jnp.exp(-x))


def _head_ones():
    r = lax.broadcasted_iota(jnp.int32, (PAIR, PAIR), 0) >> HEAD_SHIFT
    c = lax.broadcasted_iota(jnp.int32, (PAIR, PAIR), 1) >> HEAD_SHIFT
    return jnp.where(r == c, 1.0, 0.0).astype(BF16)


def _head_sum(x, ones):
    xb = x.astype(BF16)
    return jnp.concatenate([_dot(xb[:, j * PAIR:(j + 1) * PAIR], ones) for j in range(N_PAIRS)], axis=1)


def _const_spec(shape):
    nd = len(shape)
    return pl.BlockSpec(shape, lambda *_: (0,) * nd, pipeline_mode=pl.Buffered(1))


def _params(sem):
    return pltpu.CompilerParams(dimension_semantics=sem, vmem_limit_bytes=VMEM_LIMIT)


def _ffn_kernel(g_ref, wgu_ref, wd_ref, g2_ref, *refs, bounds, emit_x, emit_n):
    n_out = int(emit_x) + int(emit_n)
    x_refs, out_refs = refs[:len(bounds)], refs[len(bounds):]

    def tile(x_ref, outs):
        x = x_ref[...]
        xn = _rms(x, g_ref[...]).astype(BF16)
        acc = None
        for lo, hi in FFN_CHUNKS:
            gate = _dot(xn, wgu_ref[:, lo:hi])
            up = _dot(xn, wgu_ref[:, D_FF + lo:D_FF + hi])
            act = (gate * _sigmoid(gate) * up).astype(BF16)
            part = _dot(act, wd_ref[lo:hi, :])
            acc = part if acc is None else acc + part
        out = x + 0.5 * acc
        if emit_x:
            outs[0][...] = out
        if emit_n:
            outs[-1][...] = _rms(out, g2_ref[...]).astype(outs[-1].dtype)

    step = pl.program_id(0)
    for s, (lo, hi) in enumerate(bounds):
        args = (x_refs[s], out_refs[s * n_out:(s + 1) * n_out])
        if len(bounds) == 1:
            tile(*args)
        else:
            pl.when((step >= lo) & (step < hi))(functools.partial(tile, *args))


def _ffn(xs, g, w, g2, tm, emit_x=True, emit_n=True, n_dtype=F32):
    wgu, wd, idx = w
    n_tiles = [x.shape[0] // tm for x in xs]
    bounds, lo = [], 0
    for nt in n_tiles:
        bounds.append((lo, lo + nt))
        lo += nt
    dtypes = [F32] * int(emit_x) + [n_dtype] * int(emit_n)

    def weight_spec(rows, cols):
        return pl.BlockSpec((None, rows, cols), lambda i: (idx, 0, 0), pipeline_mode=pl.Buffered(1))

    def row_spec(first, nt):
        return pl.BlockSpec((tm, D_MODEL), lambda i: (jnp.clip(i - first, 0, nt - 1), 0))

    rows = [row_spec(b[0], nt) for b, nt in zip(bounds, n_tiles)]
    vec = _const_spec((1, D_MODEL))
    outs = pl.pallas_call(
        functools.partial(_ffn_kernel, bounds=tuple(bounds), emit_x=emit_x, emit_n=emit_n),
        grid=(lo,),
        in_specs=[vec, weight_spec(D_MODEL, 2 * D_FF), weight_spec(D_FF, D_MODEL), vec] + rows,
        out_specs=[r for r in rows for _ in dtypes],
        out_shape=[jax.ShapeDtypeStruct(x.shape, dt) for x in xs for dt in dtypes],
        compiler_params=_params(("arbitrary",)),
        name="ffn",
    )(g.reshape(1, D_MODEL), wgu, wd, g2.reshape(1, D_MODEL), *xs)
    return [tuple(outs[s * len(dtypes):(s + 1) * len(dtypes)]) for s in range(len(xs))]


N_RK_IN = 15
N_RK_OUT = 8


def _head_select(transpose=False):
    shape = (PAIR, D_MODEL) if transpose else (D_MODEL, PAIR)
    chan = lax.broadcasted_iota(jnp.int32, shape, 1 if transpose else 0) >> HEAD_SHIFT
    head = lax.broadcasted_iota(jnp.int32, shape, 0 if transpose else 1)
    return jnp.where(chan == head, 1.0, 0.0).astype(BF16)


def _rk_pre_core(h, prev, p, out_refs):
    (mu, wr, wk, wv, w0, w1, w2, a0, a1, a2, g1, g2, k_k, k_a, r_k) = [r[...] for r in p]
    xx = prev - h
    xr, xw, xk, xv, xa, xg = [(h + xx * mu[c:c + 1, :]).astype(BF16) for c in range(6)]
    r = _dot(xr, wr)
    k = _dot(xk, wk)
    v = _dot(xv, wv)
    z = w0 + _dot(jnp.tanh(_dot(xw, w1)).astype(BF16), w2)
    lw = -DECAY_SCALE * _sigmoid(z)
    a = _sigmoid(a0 + _dot(_dot(xa, a1).astype(BF16), a2))
    g = _dot(_sigmoid(_dot(xg, g1)).astype(BF16), g2)
    kk = k * k_k
    ss = _head_sum(kk * kk, _head_ones())
    kk = kk * lax.rsqrt(jnp.maximum(ss, KK_FLOOR * KK_FLOOR))
    kmod = k * (1.0 + (a - 1.0) * k_a)
    bonus = _dot((r * kmod * r_k).astype(BF16), _head_select())
    for ref, val in zip(out_refs, (r, lw, kmod, v, kk, kk * a, g, bonus)):
        ref[...] = val.astype(ref.dtype)


def _rk_pre_prompt_kernel(h_ref, s0_ref, *refs, tm):
    p, out_refs, carry_ref = refs[:N_RK_IN], refs[N_RK_IN:N_RK_IN + N_RK_OUT], refs[N_RK_IN + N_RK_OUT]

    @pl.when(pl.program_id(1) == 0)
    def _():
        carry_ref[0:1, :] = s0_ref[...]

    h = h_ref[...]
    row = lax.broadcasted_iota(jnp.int32, h.shape, 0)
    prev = jnp.where(row == 0, carry_ref[0:1, :], pltpu.roll(h, 1, axis=0))
    carry_ref[0:1, :] = h[tm - 1:tm, :]
    _rk_pre_core(h, prev, p, out_refs)


def _rk_pre_sample_kernel(h_ref, s0_ref, *refs, ns, t_len):
    p, out_refs = refs[:N_RK_IN], refs[N_RK_IN:N_RK_IN + N_RK_OUT]
    h = h_ref[...]
    rows = ns * t_len
    s0 = jnp.broadcast_to(s0_ref[...], (ns, t_len, D_MODEL)).reshape(rows, D_MODEL)
    t = lax.broadcasted_iota(jnp.int32, h.shape, 0) & (t_len - 1)
    prev = jnp.where(t == 0, s0, pltpu.roll(h, 1, axis=0))
    _rk_pre_core(h, prev, p, out_refs)


def _rk_param_specs(p):
    return [_const_spec(a.shape) for a in p]


def _rk_pre(hn, shift0, p, bsz, t_len, tm):
    m = bsz * t_len
    assert len(p) == N_RK_IN
    out_shape = ([jax.ShapeDtypeStruct((m, D_MODEL), F32 if i == 1 else BF16) for i in range(N_RK_OUT - 1)]
                 + [jax.ShapeDtypeStruct((m, PAIR), F32)])
    s0 = shift0.reshape(bsz, 1, D_MODEL)
    if t_len > tm:
        nt = t_len // tm
        row = pl.BlockSpec((tm, D_MODEL), lambda b, t: (b * nt + t, 0))
        return pl.pallas_call(
            functools.partial(_rk_pre_prompt_kernel, tm=tm),
            grid=(bsz, nt),
            in_specs=[row, pl.BlockSpec((None, 1, D_MODEL), lambda b, t: (b, 0, 0))] + _rk_param_specs(p),
            out_specs=[row] * (N_RK_OUT - 1) + [pl.BlockSpec((tm, PAIR), lambda b, t: (b * nt + t, 0))],
            out_shape=out_shape,
            scratch_shapes=[pltpu.VMEM((SUBLANES, D_MODEL), F32)],
            compiler_params=_params(("arbitrary", "arbitrary")),
            name="rk_pre_prompt",
        )(hn, s0, *p)
    ns = tm // t_len
    row = pl.BlockSpec((tm, D_MODEL), lambda i: (i, 0))
    return pl.pallas_call(
        functools.partial(_rk_pre_sample_kernel, ns=ns, t_len=t_len),
        grid=(m // tm,),
        in_specs=[row, pl.BlockSpec((ns, 1, D_MODEL), lambda i: (i, 0, 0))] + _rk_param_specs(p),
        out_specs=[row] * (N_RK_OUT - 1) + [pl.BlockSpec((tm, PAIR), lambda i: (i, 0))],
        out_shape=out_shape,
        compiler_params=_params(("parallel",)),
        name="rk_pre_sample",
    )(hn, s0, *p)


def _wkv_masks(c):
    rows = STACK_ROWS
    sh = c.bit_length() - 1
    row = lax.broadcasted_iota(jnp.int32, (rows, rows), 0)
    col = lax.broadcasted_iota(jnp.int32, (rows, rows), 1)
    same = (row >> sh) == (col >> sh)
    strict = same & (row > col)
    incl = same & (row >= col)
    base = (row >> 1) == (col >> 1)
    merges = []
    s = 2
    while s < c:
        b = s.bit_length() - 1
        merges.append(((row >> (b + 1)) == (col >> (b + 1))) & (((row >> b) & 1) == 1) & (((col >> b) & 1) == 0))
        s *= 2
    eye = jnp.where(row == col, 1.0, 0.0).astype(F32)
    return strict, incl, base, merges, eye


def _cumsum_rows(x, c):
    t = lax.broadcasted_iota(jnp.int32, x.shape, 0)
    s = 1
    while s < c:
        x = x + jnp.where(t >= s, pltpu.roll(x, s, axis=0), 0.0)
        s *= 2
    return x


def _wkv_units(ins, states, c, masks):
    strict, incl, base, merges, eye = masks
    nseq = len(ins[0])
    rows = STACK_ROWS
    assert nseq * 2 * c == rows
    lane_h = lax.broadcasted_iota(jnp.int32, (c, PAIR), 1) >> HEAD_SHIFT
    seq_rows = [slice(i * 2 * c, (i + 1) * 2 * c) for i in range(nseq)]

    def stack(xs):
        return jnp.concatenate([jnp.where(lane_h == h, x, 0.0) for x in xs for h in range(2)],
                               axis=0).astype(BF16)

    def each(f, *lists):
        return [f(*xs) for xs in zip(*lists)]

    def prep(seqs):
        per = []
        for r, lw, k, v, kk, b in seqs:
            cum = _cumsum_rows(lw, c)
            last = cum[c - 1:c, :]
            to_end = jnp.exp(last - cum)
            p_inv = jnp.exp(-cum)
            per.append(dict(p_c=jnp.exp(last), kk=kk * jnp.exp(cum - lw), r=r * jnp.exp(cum), k=k * p_inv,
                            b=b * p_inv, v=v, kd=k * to_end, nbd=-(b * to_end)))
        out = {name: stack([d[name] for d in per]) for name in ('kk', 'r', 'k', 'b', 'v', 'kd', 'nbd')}
        out['p_c'] = [d['p_c'] for d in per]
        return out

    o = [prep(x) for x in ins]
    kb = [jnp.concatenate([x['k'], x['b']], axis=0) for x in o]
    a_kk = each(lambda x, w_: _dot_nt(x['kk'], w_), o, kb)
    a_r = each(lambda x, w_: _dot_nt(x['r'], w_), o, kb)
    a_kk_k, a_kk_b = [a[:, :rows] for a in a_kk], [a[:, rows:] for a in a_kk]
    a_r_k, a_r_b = [a[:, :rows] for a in a_r], [a[:, rows:] for a in a_r]

    lm = [jnp.where(strict, a, 0.0) for a in a_kk_b]
    inv = [eye - jnp.where(base, l, 0.0) for l in lm]
    size = 2
    for m in merges:
        if size >= SUBLANES:
            starts = range(size, rows, 2 * size)

            def gather(x, starts=starts, size=size):
                return jnp.concatenate([x[r0:r0 + size] for r0 in starts], axis=0)

            def scatter(xh, n=len(starts), size=size):
                zero = jnp.zeros((size, rows), F32)
                return jnp.concatenate([p_ for j in range(n) for p_ in (zero, xh[j * size:(j + 1) * size])],
                                       axis=0)

            inv_b = [t.astype(BF16) for t in inv]
            mid = each(lambda l, t: scatter(_dot(gather(jnp.where(m, l, 0.0)).astype(BF16), t)).astype(BF16),
                       lm, inv_b)
            inv = each(lambda t, md: t - scatter(_dot(gather(t).astype(BF16), md)), inv, mid)
        else:
            inv_b = [t.astype(BF16) for t in inv]
            mid = each(lambda l, t: _dot(jnp.where(m, l, 0.0).astype(BF16), t).astype(BF16), lm, inv_b)
            inv = each(lambda t, tb, md: t - _dot(tb, md), inv, inv_b, mid)
        size *= 2

    def state_dot(x, s_list):
        outs = [_dot_nt(jnp.concatenate([x['kk'][rs], x['r'][rs]], axis=0), s.astype(BF16))
                for rs, s in zip(seq_rows, s_list)]
        h = 2 * c
        if nseq == 1:
            return outs[0][:h], outs[0][h:]
        return (jnp.concatenate([t[:h] for t in outs], axis=0), jnp.concatenate([t[h:] for t in outs], axis=0))

    sd = each(state_dot, o, states)
    kkh, rh = [t[0] for t in sd], [t[1] for t in sd]
    rhs = each(lambda h, a, x: h + _dot(jnp.where(strict, a, 0.0).astype(BF16), x['v']), kkh, a_kk_k, o)
    u_b = each(lambda t, z: _dot(t.astype(BF16), z.astype(BF16)).astype(BF16), inv, rhs)
    y_s = each(lambda h, ak, ab, x, u: h + _dot(jnp.where(incl, ak, 0.0).astype(BF16), x['v'])
               - _dot(jnp.where(incl, ab, 0.0).astype(BF16), u), rh, a_r_k, a_r_b, o, u_b)

    def fold(ys):
        return [ys[rs][:c] + ys[rs][c:] for rs in seq_rows]

    def new_state(x, u, s_list):
        return [s * pc + _dot_tn(jnp.concatenate([x['v'][rs], u[rs]], axis=0),
                                 jnp.concatenate([x['kd'][rs], x['nbd'][rs]], axis=0))
                for rs, s, pc in zip(seq_rows, s_list, x['p_c'])]

    return [fold(ys) for ys in y_s], each(new_state, o, u_b, states)


def _rows_to_blockdiag(s):
    ext = jnp.concatenate([s, jnp.zeros_like(s)], axis=1)
    row = lax.broadcasted_iota(jnp.int32, ext.shape, 0)
    return jnp.where(row < HEAD_DIM, ext, pltpu.roll(ext, HEAD_DIM, axis=1))


def _blockdiag_to_rows(s_bd):
    row = lax.broadcasted_iota(jnp.int32, s_bd.shape, 0)
    return jnp.where(row < HEAD_DIM, s_bd, pltpu.roll(s_bd, HEAD_DIM, axis=1))[:, :HEAD_DIM]


def _wkv_prompt_kernel(r_ref, lw_ref, k_ref, v_ref, kk_ref, b_ref, s0_ref, *refs, c, t_len, nb, n_cast):
    cast_in, (y_ref, sout_ref), cast_out, s_ref = (refs[:n_cast], refs[n_cast:n_cast + 2],
                                                   refs[n_cast + 2:2 * n_cast + 2], refs[2 * n_cast + 2])
    for w_in, w_out in zip(cast_in, cast_out):
        w_out[...] = w_in[...].astype(BF16)
    ci = pl.program_id(1)
    units = [(i, q) for i in range(nb) for q in range(N_PAIRS)]

    @pl.when(ci == 0)
    def _():
        for i, q in units:
            s_ref[i, q] = _rows_to_blockdiag(s0_ref[i, q])

    masks = _wkv_masks(c)
    lanes = [slice(q * PAIR, (q + 1) * PAIR) for q in range(N_PAIRS)]

    def load(ref, i, q):
        x = ref[i, :, lanes[q]].astype(F32)
        if t_len % c:
            valid = lax.broadcasted_iota(jnp.int32, (c, PAIR), 0) + ci * c < t_len
            x = jnp.where(valid, x, 0.0)
        return x

    ins = [[[load(ref, i, q) for ref in (r_ref, lw_ref, k_ref, v_ref, kk_ref, b_ref)]] for i, q in units]
    ys, s_new = _wkv_units(ins, [[s_ref[i, q]] for i, q in units], c, masks)
    for (i, q), y, s in zip(units, ys, s_new):
        y_ref[i, :, lanes[q]] = y[0].astype(y_ref.dtype)
        s_ref[i, q] = s[0]

    @pl.when(ci == pl.num_programs(1) - 1)
    def _():
        for i, q in units:
            sout_ref[i, q] = _blockdiag_to_rows(s_ref[i, q])


def _wkv_sample_kernel(r_ref, lw_ref, k_ref, v_ref, kk_ref, b_ref, s0_ref, y_ref, sout_ref, *, c, nseq, groups):
    masks = _wkv_masks(c)
    lanes = [slice(q * PAIR, (q + 1) * PAIR) for q in range(N_PAIRS)]
    full = [ref[...].astype(F32) for ref in (r_ref, lw_ref, k_ref, v_ref, kk_ref, b_ref)]
    units = [(g, q) for g in range(groups) for q in range(N_PAIRS)]
    seqs = lambda g: range(g * nseq, (g + 1) * nseq)
    ins = [[[a[i * c:(i + 1) * c, lanes[q]] for a in full] for i in seqs(g)] for g, q in units]
    states = [[_rows_to_blockdiag(s0_ref[i, q]) for i in seqs(g)] for g, q in units]
    ys, s_new = _wkv_units(ins, states, c, masks)
    for (g, q), y, s in zip(units, ys, s_new):
        for j, i in enumerate(seqs(g)):
            y_ref[i * c:(i + 1) * c, lanes[q]] = y[j]
            sout_ref[i, q] = _blockdiag_to_rows(s[j])


def _cast_slabs(w, skip, n_steps, nc):
    n_rows = w.shape[0] - skip
    slab = next(r for r in range(BF16_ROWS, n_rows + 1, BF16_ROWS)
                if n_rows % r == 0 and skip % r == 0 and n_rows // r <= n_steps)
    last = n_rows // slab - 1
    first = skip // slab
    in_spec = pl.BlockSpec((slab, w.shape[1]), lambda b, i: (first + jnp.minimum(b * nc + i, last), 0))
    out_spec = pl.BlockSpec((slab, w.shape[1]), lambda b, i: (jnp.minimum(b * nc + i, last), 0))
    return in_spec, out_spec, jax.ShapeDtypeStruct((n_rows, w.shape[1]), BF16)


def _wkv(ins, state, bsz, t_len, t_valid, cast=()):
    m = bsz * t_len
    y_shape = jax.ShapeDtypeStruct((m, D_MODEL), F32)
    s_rows = state.reshape(bsz, N_PAIRS, PAIR, HEAD_DIM)
    s_shape = jax.ShapeDtypeStruct(s_rows.shape, F32)
    if t_len >= WKV_CHUNK:
        c, nb = WKV_CHUNK, min(bsz, WKV_BATCH)
        nc = t_len // c
        ins3 = [a.reshape(bsz, t_len, D_MODEL) for a in ins]
        blk = pl.BlockSpec((nb, c, D_MODEL), lambda b, i: (b, i, 0))
        sblk = pl.BlockSpec((nb, N_PAIRS, PAIR, HEAD_DIM), lambda b, i: (b, 0, 0, 0))
        slabs = [_cast_slabs(w, skip, (bsz // nb) * nc, nc) for w, skip in cast]
        y, s_out, *cast_out = pl.pallas_call(
            functools.partial(_wkv_prompt_kernel, c=c, t_len=t_valid, nb=nb, n_cast=len(cast)),
            grid=(bsz // nb, nc),
            in_specs=[blk] * 6 + [sblk] + [s[0] for s in slabs],
            out_specs=[blk, sblk] + [s[1] for s in slabs],
            out_shape=[jax.ShapeDtypeStruct((bsz, t_len, D_MODEL), BF16), s_shape] + [s[2] for s in slabs],
            scratch_shapes=[pltpu.VMEM((nb, N_PAIRS, PAIR, PAIR), F32)],
            compiler_params=_params(("arbitrary", "arbitrary")),
            name="wkv_prompt",
        )(*ins3, s_rows, *[w for w, _ in cast])
        return y.reshape(m, D_MODEL), s_out.reshape(state.shape), cast_out
    assert t_valid == t_len and not cast
    c, nseq = t_len, WKV_CHUNK // t_len
    groups = WKV_SHORT_GROUPS if bsz % (WKV_SHORT_GROUPS * nseq) == 0 else 1
    blk = pl.BlockSpec((groups * nseq * c, D_MODEL), lambda i: (i, 0))
    sblk = pl.BlockSpec((groups * nseq, N_PAIRS, PAIR, HEAD_DIM), lambda i: (i, 0, 0, 0))
    y, s_out = pl.pallas_call(
        functools.partial(_wkv_sample_kernel, c=c, nseq=nseq, groups=groups),
        grid=(bsz // (groups * nseq),),
        in_specs=[blk] * 6 + [sblk],
        out_specs=[blk, sblk],
        out_shape=[y_shape, s_shape],
        compiler_params=_params(("parallel",)),
        name="wkv_sample",
    )(*ins, s_rows)
    return y, s_out.reshape(state.shape), []


def _rk_post_kernel(x_ref, y_ref, bonus_ref, v_ref, g_ref, lnw_ref, lnb_ref, wo_ref, o_ref):
    ones = _head_ones()
    y = y_ref[...].astype(F32)
    inv_n = 1.0 / HEAD_DIM
    mean = _head_sum(y, ones) * inv_n
    yc = y - mean
    var = _head_sum(yc * yc, ones) * inv_n
    yn = yc * lax.rsqrt(var + GN_EPS) * lnw_ref[...] + lnb_ref[...]
    coef = bonus_ref[...]
    hi = coef.astype(BF16)
    lo = (coef - hi.astype(F32)).astype(BF16)
    spread = _head_select(transpose=True)
    bonus = (_dot(hi, spread) + _dot(lo, spread)) * v_ref[...].astype(F32)
    o = ((yn + bonus) * g_ref[...].astype(F32)).astype(BF16)
    o_ref[...] = x_ref[...] + _dot(o, wo_ref[...])


def _rk_post(x, y, bonus, v, g, ln_w, ln_b, w_o, tm):
    m = x.shape[0]
    row = pl.BlockSpec((tm, D_MODEL), lambda i: (i, 0))
    vec = _const_spec((1, D_MODEL))
    return pl.pallas_call(
        _rk_post_kernel,
        grid=(m // tm,),
        in_specs=[row, row, pl.BlockSpec((tm, PAIR), lambda i: (i, 0)), row, row, vec, vec,
                  _const_spec((D_MODEL, D_MODEL))],
        out_specs=row,
        out_shape=jax.ShapeDtypeStruct((m, D_MODEL), F32),
        compiler_params=_params(("parallel",)),
        name="rk_post",
    )(x, y, bonus, v, g, ln_w.reshape(1, D_MODEL), ln_b.reshape(1, D_MODEL), w_o)


def _conv_core(x, h, shifted, win_ref, cw_ref, wout_ref):
    z = _dot(h.astype(BF16), win_ref[...])
    gate_b = z[:, 0:D_MODEL]
    u = z[:, D_MODEL:2 * D_MODEL] * z[:, 2 * D_MODEL:3 * D_MODEL]
    cw = cw_ref[...]
    u1, u2 = shifted(u)
    conv = u2 * cw[0:1, :] + u1 * cw[1:2, :] + u * cw[2:3, :]
    out = x + _dot((gate_b * conv).astype(BF16), wout_ref[...])
    return out, u


def _conv_prompt_kernel(x_ref, h_ref, c0_ref, win_ref, cw_ref, wout_ref, o_ref, st_ref, carry_ref, *, tm):
    @pl.when(pl.program_id(1) == 0)
    def _():
        carry_ref[0:2, :] = c0_ref[...]

    def shifted(u):
        row = lax.broadcasted_iota(jnp.int32, u.shape, 0)
        c0 = carry_ref[0:1, :]
        c1 = carry_ref[1:2, :]
        u1 = jnp.where(row == 0, c1, pltpu.roll(u, 1, axis=0))
        u2 = jnp.where(row == 0, c0, jnp.where(row == 1, c1, pltpu.roll(u, 2, axis=0)))
        return u1, u2

    out, u = _conv_core(x_ref[...], h_ref[...], shifted, win_ref, cw_ref, wout_ref)
    o_ref[...] = out
    carry_ref[0:2, :] = u[tm - 2:tm, :]
    st_ref[...] = u[tm - SUBLANES:tm, :]


def _conv_sample_kernel(x_ref, h_ref, c0_ref, win_ref, cw_ref, wout_ref, o_ref, u_ref, *, ns, t_len):
    rows = ns * t_len
    c0 = jnp.broadcast_to(c0_ref[:, 0:1, :], (ns, t_len, D_MODEL)).reshape(rows, D_MODEL)
    c1 = jnp.broadcast_to(c0_ref[:, 1:2, :], (ns, t_len, D_MODEL)).reshape(rows, D_MODEL)

    def shifted(u):
        t = lax.broadcasted_iota(jnp.int32, u.shape, 0) & (t_len - 1)
        u1 = jnp.where(t == 0, c1, pltpu.roll(u, 1, axis=0))
        u2 = jnp.where(t == 0, c0, jnp.where(t == 1, c1, pltpu.roll(u, 2, axis=0)))
        return u1, u2

    out, u = _conv_core(x_ref[...], h_ref[...], shifted, win_ref, cw_ref, wout_ref)
    o_ref[...] = out
    u_ref[...] = u


def _conv_mix(x, hn, conv0, w_in, conv_w, w_out, bsz, t_len, t_valid, tm):
    m = bsz * t_len
    consts = [_const_spec((D_MODEL, 3 * D_MODEL)), _const_spec((CONV_W, D_MODEL)), _const_spec((D_MODEL, D_MODEL))]
    if t_len > tm:
        assert t_valid == t_len
        nt = t_len // tm
        row = pl.BlockSpec((tm, D_MODEL), lambda b, t: (b * nt + t, 0))
        out, tail = pl.pallas_call(
            functools.partial(_conv_prompt_kernel, tm=tm),
            grid=(bsz, nt),
            in_specs=[row, row, pl.BlockSpec((None, CONV_W - 1, D_MODEL), lambda b, t: (b, 0, 0))] + consts,
            out_specs=[row, pl.BlockSpec((None, SUBLANES, D_MODEL), lambda b, t: (b, 0, 0))],
            out_shape=[jax.ShapeDtypeStruct((m, D_MODEL), F32),
                       jax.ShapeDtypeStruct((bsz, SUBLANES, D_MODEL), F32)],
            scratch_shapes=[pltpu.VMEM((SUBLANES, D_MODEL), F32)],
            compiler_params=_params(("arbitrary", "arbitrary")),
            name="conv_prompt",
        )(x, hn, conv0, w_in, conv_w, w_out)
        return out, tail[:, SUBLANES - (CONV_W - 1):, :]
    ns = tm // t_len
    row = pl.BlockSpec((tm, D_MODEL), lambda i: (i, 0))
    out, u = pl.pallas_call(
        functools.partial(_conv_sample_kernel, ns=ns, t_len=t_len),
        grid=(m // tm,),
        in_specs=[row, row, pl.BlockSpec((ns, CONV_W - 1, D_MODEL), lambda i: (i, 0, 0))] + consts,
        out_specs=[row, row],
        out_shape=[jax.ShapeDtypeStruct((m, D_MODEL), F32)] * 2,
        compiler_params=_params(("parallel",)),
        name="conv_sample",
    )(x, hn, conv0, w_in, conv_w, w_out)
    return out, u.reshape(bsz, t_len, D_MODEL)[:, t_valid - (CONV_W - 1):t_valid, :]


def _trunk_head(streams, p, w00, tm, cast=()):
    heads, cast_out = [], []
    ffn_out = _ffn([s['x'] for s in streams], p['ffn_norm'][0, 0], w00, p['mix_norm'][0], tm)
    for i, (s, (x, hn)) in enumerate(zip(streams, ffn_out)):
        bsz, t_len, t_valid = s['bsz'], s['t_len'], s['t_valid']
        new_shift = hn.reshape(bsz, t_len, D_MODEL)[:, t_valid - 1]
        r, lw, k, v, kk, b, g, bonus = _rk_pre(hn, s['shift'][0], p['rk_pre'], bsz, t_len, tm)
        y, new_wkv, casts = _wkv([r, lw, k, v, kk, b], s['wkv'][0], bsz, t_len, t_valid, cast if i == 0 else ())
        cast_out += casts
        heads.append(dict(s, acts=(x, y, bonus, v, g), new_wkv=new_wkv[None], new_shift=new_shift[None]))
    return heads, cast_out


def _trunk_tail(heads, conv_in, p, w_rest, tm, want_y=True):
    norm = p['ffn_norm']
    xs, tiles = [], []
    for s in heads:
        tiles.append(2 * tm if s['t_len'] >= 2 * tm else tm)
        x, y, bonus, v, g = s['acts']
        xs.append(_rk_post(x, y, bonus, v, g, p['rk_ln_w'][0], p['rk_ln_b'][0], p['rk_w_o'][0], tiles[-1]))
    xs = [o[0] for o in _ffn(xs, norm[0, 1], w_rest[0], norm[1, 0], tm, emit_n=False)]
    mixed = _ffn(xs, norm[1, 0], w_rest[1], p['mix_norm'][1], tm, n_dtype=BF16)
    xs, convs = [], []
    for s, (x, hn), conv0, tile in zip(heads, mixed, conv_in, tiles):
        x, new_conv = _conv_mix(x, hn, conv0[0], p['sc_w_in'][0], p['sc_conv_w'][0], p['sc_w_out'][0],
                                s['bsz'], s['t_len'], s['t_valid'], tile)
        xs.append(x)
        convs.append(new_conv[None])
    ys = [o[0] for o in _ffn(xs, norm[1, 1], w_rest[2], p['final_norm'], tm, emit_x=False)] if want_y \
        else [None] * len(heads)
    return list(zip(ys, convs))


def kernel(x_prompt, x_sample, state_wkv, state_shift, state_conv, meta, ffn_norm, ffn_w_gu, ffn_w_down,
           mix_norm, final_norm, rk_mu, rk_w_rkv, rk_w0, rk_w1, rk_w2, rk_a0, rk_a1, rk_a2, rk_g1, rk_g2,
           rk_k_k, rk_k_a, rk_r_k, rk_ln_w, rk_ln_b, rk_w_o, sc_w_in, sc_conv_w, sc_w_out):
    assert DEPTH == 2 and rk_mu.shape[0] == 1 and sc_w_in.shape[0] == 1
    bf = lambda a: a.astype(BF16)
    vec = lambda a: a.reshape(1, D_MODEL)
    p = {
        'ffn_norm': ffn_norm, 'mix_norm': mix_norm, 'final_norm': final_norm,
        'rk_pre': [rk_mu[0], bf(rk_w_rkv[0, 0]), bf(rk_w_rkv[0, 1]), bf(rk_w_rkv[0, 2]), vec(rk_w0[0]),
                   bf(rk_w1[0]), bf(rk_w2[0]), vec(rk_a0[0]), bf(rk_a1[0]), bf(rk_a2[0]), bf(rk_g1[0]),
                   bf(rk_g2[0]), vec(rk_k_k[0]), vec(rk_k_a[0]), rk_r_k.reshape(1, D_MODEL)],
        'rk_ln_w': rk_ln_w, 'rk_ln_b': rk_ln_b, 'rk_w_o': bf(rk_w_o),
        'sc_w_in': bf(sc_w_in), 'sc_conv_w': sc_conv_w, 'sc_w_out': bf(sc_w_out),
    }
    w00 = (bf(ffn_w_gu[0, 0])[None], bf(ffn_w_down[0, 0])[None], 0)
    later = ((ffn_w_gu.reshape(2 * DEPTH * D_MODEL, 2 * D_FF), D_MODEL),
             (ffn_w_down.reshape(2 * DEPTH * D_FF, D_MODEL), D_FF))
    bs, t_s, _ = x_sample.shape
    bp, seq, _ = x_prompt.shape
    meta_rows = jnp.pad(meta.astype(x_prompt.dtype), ((0, WKV_CHUNK - N_META), (0, 0)))
    zero_wkv = jnp.zeros((1, 1, N_HEADS, HEAD_DIM, HEAD_DIM), state_wkv.dtype)
    zero_shift = jnp.zeros((1, 1, D_MODEL), state_shift.dtype)
    zero_conv = jnp.zeros((1, 1, CONV_W - 1, D_MODEL), state_conv.dtype)
    to_batch = lambda s: jnp.broadcast_to(s, (1, bp) + s.shape[2:])
    meta_stream = dict(x=meta_rows, bsz=1, t_len=WKV_CHUNK, t_valid=N_META, wkv=zero_wkv, shift=zero_shift)
    (head_m,), _ = _trunk_head([meta_stream], p, w00, WKV_CHUNK)
    main = [dict(x=x_prompt.reshape(bp * seq, D_MODEL), bsz=bp, t_len=seq, t_valid=seq,
                 wkv=to_batch(head_m['new_wkv']), shift=to_batch(head_m['new_shift'])),
            dict(x=x_sample.reshape(bs * t_s, D_MODEL), bsz=bs, t_len=t_s, t_valid=t_s,
                 wkv=state_wkv, shift=state_shift)]
    (head_p, head_s), (wgu_rest, wd_rest) = _trunk_head(main, p, w00, ROW_TILE, cast=later)
    wgu_rest = wgu_rest.reshape(2 * DEPTH - 1, D_MODEL, 2 * D_FF)
    wd_rest = wd_rest.reshape(2 * DEPTH - 1, D_FF, D_MODEL)
    w_rest = [(wgu_rest, wd_rest, i) for i in range(2 * DEPTH - 1)]
    ((_, conv_m),) = _trunk_tail([head_m], [zero_conv], p, w_rest, WKV_CHUNK, want_y=False)
    (yp, conv_p), (ys, conv_s) = _trunk_tail([head_p, head_s], [to_batch(conv_m), state_conv], p, w_rest,
                                             ROW_TILE)
    return (yp.reshape(bp, seq, D_MODEL), ys.reshape(bs, t_s, D_MODEL), head_p['new_wkv'], head_p['new_shift'],
            conv_p, head_s['new_wkv'], head_s['new_shift'], conv_s)
```

```python
import functools
import math

import jax
import jax.numpy as jnp
from jax import lax
from jax.experimental import pallas as pl
from jax.experimental.pallas import tpu as pltpu

D_MODEL = 1024
HEAD_DIM = 64
N_HEADS = D_MODEL // HEAD_DIM
D_FF = 2816
N_META = 16
DEPTH = 2
CONV_W = 3
RMS_EPS = 1e-6
GN_EPS = 64e-5
DECAY_SCALE = math.exp(-0.5)

SUBLANES = 8
BF16_ROWS = 16
MXU_DIM = 256
PAIR = 2 * HEAD_DIM
HEAD_SHIFT = HEAD_DIM.bit_length() - 1
KK_FLOOR = 1e-12
WKV_CHUNK = 64
STACK_ROWS = 2 * WKV_CHUNK
WKV_BATCH = 4
WKV_SHORT_GROUPS = 2
N_PAIRS = D_MODEL // PAIR
VMEM_LIMIT = 56 * 1024 * 1024
ROW_TILE = 512

FFN_CHUNKS = ((0, 6 * MXU_DIM), (6 * MXU_DIM, D_FF))

F32 = jnp.float32
BF16 = jnp.bfloat16


def _dot(a, b):
    return jnp.dot(a, b, preferred_element_type=F32)


def _dot_nt(a, b):
    return lax.dot_general(a, b, (((1,), (1,)), ((), ())), preferred_element_type=F32)


def _dot_tn(a, b):
    return lax.dot_general(a, b, (((0,), (0,)), ((), ())), preferred_element_type=F32)


def _rms(x, g):
    ms = jnp.mean(x * x, axis=-1, keepdims=True)
    return x * lax.rsqrt(ms + RMS_EPS) * g


def _sigmoid(x):
    return 1.0 / (1.0 + jnp.exp(-x))


def _head_ones():
    r = lax.broadcasted_iota(jnp.int32, (PAIR, PAIR), 0) >> HEAD_SHIFT
    c = lax.broadcasted_iota(jnp.int32, (PAIR, PAIR), 1) >> HEAD_SHIFT
    return jnp.where(r == c, 1.0, 0.0).astype(BF16)


def _head_sum(x, ones):
    xb = x.astype(BF16)
    return jnp.concatenate([_dot(xb[:, j * PAIR:(j + 1) * PAIR], ones) for j in range(N_PAIRS)], axis=1)


def _const_spec(shape):
    nd = len(shape)
    return pl.BlockSpec(shape, lambda *_: (0,) * nd, pipeline_mode=pl.Buffered(1))


def _params(sem):
    return pltpu.CompilerParams(dimension_semantics=sem, vmem_limit_bytes=VMEM_LIMIT)


def _ffn_kernel(g_ref, wgu_ref, wd_ref, g2_ref, *refs, bounds, emit_x, emit_n):
    n_out = int(emit_x) + int(emit_n)
    x_refs, out_refs = refs[:len(bounds)], refs[len(bounds):]

    def tile(x_ref, outs):
        x = x_ref[...]
        xn = _rms(x, g_ref[...]).astype(BF16)
        acc = None
        for lo, hi in FFN_CHUNKS:
            gate = _dot(xn, wgu_ref[:, lo:hi])
            up = _dot(xn, wgu_ref[:, D_FF + lo:D_FF + hi])
            act = (gate * _sigmoid(gate) * up).astype(BF16)
            part = _dot(act, wd_ref[lo:hi, :])
            acc = part if acc is None else acc + part
        out = x + 0.5 * acc
        if emit_x:
            outs[0][...] = out
        if emit_n:
            outs[-1][...] = _rms(out, g2_ref[...]).astype(outs[-1].dtype)

    step = pl.program_id(0)
    for s, (lo, hi) in enumerate(bounds):
        args = (x_refs[s], out_refs[s * n_out:(s + 1) * n_out])
        if len(bounds) == 1:
            tile(*args)
        else:
            pl.when((step >= lo) & (step < hi))(functools.partial(tile, *args))


def _ffn(xs, g, w, g2, tm, emit_x=True, emit_n=True, n_dtype=F32):
    wgu, wd, idx = w
    tiles = [min(tm, x.shape[0]) for x in xs]
    n_tiles = [x.shape[0] // t for x, t in zip(xs, tiles)]
    bounds, lo = [], 0
    for nt in n_tiles:
        bounds.append((lo, lo + nt))
        lo += nt
    dtypes = [F32] * int(emit_x) + [n_dtype] * int(emit_n)

    def weight_spec(rows, cols):
        return pl.BlockSpec((None, rows, cols), lambda i: (idx, 0, 0), pipeline_mode=pl.Buffered(1))

    def row_spec(first, nt, tile):
        return pl.BlockSpec((tile, D_MODEL), lambda i: (jnp.clip(i - first, 0, nt - 1), 0))

    rows = [row_spec(b[0], nt, t) for b, nt, t in zip(bounds, n_tiles, tiles)]
    vec = _const_spec((1, D_MODEL))
    outs = pl.pallas_call(
        functools.partial(_ffn_kernel, bounds=tuple(bounds), emit_x=emit_x, emit_n=emit_n),
        grid=(lo,),
        in_specs=[vec, weight_spec(D_MODEL, 2 * D_FF), weight_spec(D_FF, D_MODEL), vec] + rows,
        out_specs=[r for r in rows for _ in dtypes],
        out_shape=[jax.ShapeDtypeStruct(x.shape, dt) for x in xs for dt in dtypes],
        compiler_params=_params(("arbitrary",)),
        name="ffn",
    )(g.reshape(1, D_MODEL), wgu, wd, g2.reshape(1, D_MODEL), *xs)
    return [tuple(outs[s * len(dtypes):(s + 1) * len(dtypes)]) for s in range(len(xs))]


N_RK_IN = 15
N_RK_OUT = 8


def _head_select(transpose=False):
    shape = (PAIR, D_MODEL) if transpose else (D_MODEL, PAIR)
    chan = lax.broadcasted_iota(jnp.int32, shape, 1 if transpose else 0) >> HEAD_SHIFT
    head = lax.broadcasted_iota(jnp.int32, shape, 0 if transpose else 1)
    return jnp.where(chan == head, 1.0, 0.0).astype(BF16)


def _rk_pre_core(h, prev, p, out_refs):
    (mu, wr, wk, wv, w0, w1, w2, a0, a1, a2, g1, g2, k_k, k_a, r_k) = [r[...] for r in p]
    xx = prev - h
    xr, xw, xk, xv, xa, xg = [(h + xx * mu[c:c + 1, :]).astype(BF16) for c in range(6)]
    r = _dot(xr, wr)
    k = _dot(xk, wk)
    v = _dot(xv, wv)
    z = w0 + _dot(jnp.tanh(_dot(xw, w1)).astype(BF16), w2)
    lw = -DECAY_SCALE * _sigmoid(z)
    a = _sigmoid(a0 + _dot(_dot(xa, a1).astype(BF16), a2))
    g = _dot(_sigmoid(_dot(xg, g1)).astype(BF16), g2)
    kk = k * k_k
    ss = _head_sum(kk * kk, _head_ones())
    kk = kk * lax.rsqrt(jnp.maximum(ss, KK_FLOOR * KK_FLOOR))
    kmod = k * (1.0 + (a - 1.0) * k_a)
    bonus = _dot((r * kmod * r_k).astype(BF16), _head_select())
    for ref, val in zip(out_refs, (r, lw, kmod, v, kk, kk * a, g, bonus)):
        ref[...] = val.astype(ref.dtype)


def _rk_pre_prompt_kernel(h_ref, s0_ref, *refs, tm):
    p, out_refs, carry_ref = refs[:N_RK_IN], refs[N_RK_IN:N_RK_IN + N_RK_OUT], refs[N_RK_IN + N_RK_OUT]

    @pl.when(pl.program_id(1) == 0)
    def _():
        carry_ref[0:1, :] = s0_ref[...]

    h = h_ref[...]
    row = lax.broadcasted_iota(jnp.int32, h.shape, 0)
    prev = jnp.where(row == 0, carry_ref[0:1, :], pltpu.roll(h, 1, axis=0))
    carry_ref[0:1, :] = h[tm - 1:tm, :]
    _rk_pre_core(h, prev, p, out_refs)


def _rk_pre_sample_kernel(h_ref, s0_ref, *refs, ns, t_len):
    p, out_refs = refs[:N_RK_IN], refs[N_RK_IN:N_RK_IN + N_RK_OUT]
    h = h_ref[...]
    rows = ns * t_len
    s0 = jnp.broadcast_to(s0_ref[...], (ns, t_len, D_MODEL)).reshape(rows, D_MODEL)
    t = lax.broadcasted_iota(jnp.int32, h.shape, 0) & (t_len - 1)
    prev = jnp.where(t == 0, s0, pltpu.roll(h, 1, axis=0))
    _rk_pre_core(h, prev, p, out_refs)


def _rk_param_specs(p):
    return [_const_spec(a.shape) for a in p]


def _rk_pre(hn, shift0, p, bsz, t_len, tm):
    m = bsz * t_len
    assert len(p) == N_RK_IN
    out_shape = ([jax.ShapeDtypeStruct((m, D_MODEL), F32 if i == 1 else BF16) for i in range(N_RK_OUT - 1)]
                 + [jax.ShapeDtypeStruct((m, PAIR), F32)])
    s0 = shift0.reshape(bsz, 1, D_MODEL)
    if t_len > tm:
        nt = t_len // tm
        row = pl.BlockSpec((tm, D_MODEL), lambda b, t: (b * nt + t, 0))
        return pl.pallas_call(
            functools.partial(_rk_pre_prompt_kernel, tm=tm),
            grid=(bsz, nt),
            in_specs=[row, pl.BlockSpec((None, 1, D_MODEL), lambda b, t: (b, 0, 0))] + _rk_param_specs(p),
            out_specs=[row] * (N_RK_OUT - 1) + [pl.BlockSpec((tm, PAIR), lambda b, t: (b * nt + t, 0))],
            out_shape=out_shape,
            scratch_shapes=[pltpu.VMEM((SUBLANES, D_MODEL), F32)],
            compiler_params=_params(("arbitrary", "arbitrary")),
            name="rk_pre_prompt",
        )(hn, s0, *p)
    ns = tm // t_len
    row = pl.BlockSpec((tm, D_MODEL), lambda i: (i, 0))
    return pl.pallas_call(
        functools.partial(_rk_pre_sample_kernel, ns=ns, t_len=t_len),
        grid=(m // tm,),
        in_specs=[row, pl.BlockSpec((ns, 1, D_MODEL), lambda i: (i, 0, 0))] + _rk_param_specs(p),
        out_specs=[row] * (N_RK_OUT - 1) + [pl.BlockSpec((tm, PAIR), lambda i: (i, 0))],
        out_shape=out_shape,
        compiler_params=_params(("parallel",)),
        name="rk_pre_sample",
    )(hn, s0, *p)


def _wkv_masks(c):
    rows = STACK_ROWS
    sh = c.bit_length() - 1
    row = lax.broadcasted_iota(jnp.int32, (rows, rows), 0)
    col = lax.broadcasted_iota(jnp.int32, (rows, rows), 1)
    same = (row >> sh) == (col >> sh)
    strict = same & (row > col)
    incl = same & (row >= col)
    base = (row >> 1) == (col >> 1)
    merges = []
    s = 2
    while s < c:
        b = s.bit_length() - 1
        merges.append(((row >> (b + 1)) == (col >> (b + 1))) & (((row >> b) & 1) == 1) & (((col >> b) & 1) == 0))
        s *= 2
    eye = jnp.where(row == col, 1.0, 0.0).astype(F32)
    return strict, incl, base, merges, eye


def _cumsum_rows(x, c):
    t = lax.broadcasted_iota(jnp.int32, x.shape, 0)
    s = 1
    while s < c:
        x = x + jnp.where(t >= s, pltpu.roll(x, s, axis=0), 0.0)
        s *= 2
    return x


def _wkv_units(ins, states, c, masks):
    strict, incl, base, merges, eye = masks
    nseq = len(ins[0])
    rows = STACK_ROWS
    assert nseq * 2 * c == rows
    lane_h = lax.broadcasted_iota(jnp.int32, (c, PAIR), 1) >> HEAD_SHIFT
    seq_rows = [slice(i * 2 * c, (i + 1) * 2 * c) for i in range(nseq)]

    def stack(xs):
        return jnp.concatenate([jnp.where(lane_h == h, x, 0.0) for x in xs for h in range(2)],
                               axis=0).astype(BF16)

    def each(f, *lists):
        return [f(*xs) for xs in zip(*lists)]

    def prep(seqs):
        per = []
        for r, lw, k, v, kk, b in seqs:
            cum = _cumsum_rows(lw, c)
            last = cum[c - 1:c, :]
            to_end = jnp.exp(last - cum)
            p_inv = jnp.exp(-cum)
            per.append(dict(p_c=jnp.exp(last), kk=kk * jnp.exp(cum - lw), r=r * jnp.exp(cum), k=k * p_inv,
                            b=b * p_inv, v=v, kd=k * to_end, nbd=-(b * to_end)))
        out = {name: stack([d[name] for d in per]) for name in ('kk', 'r', 'k', 'b', 'v', 'kd', 'nbd')}
        out['p_c'] = [d['p_c'] for d in per]
        return out

    o = [prep(x) for x in ins]
    kb = [jnp.concatenate([x['k'], x['b']], axis=0) for x in o]
    a_kk = each(lambda x, w_: _dot_nt(x['kk'], w_), o, kb)
    a_r = each(lambda x, w_: _dot_nt(x['r'], w_), o, kb)
    a_kk_k, a_kk_b = [a[:, :rows] for a in a_kk], [a[:, rows:] for a in a_kk]
    a_r_k, a_r_b = [a[:, :rows] for a in a_r], [a[:, rows:] for a in a_r]

    lm = [jnp.where(strict, a, 0.0) for a in a_kk_b]
    inv = [eye - jnp.where(base, l, 0.0) for l in lm]
    size = 2
    for m in merges:
        if size >= SUBLANES:
            starts = range(size, rows, 2 * size)

            def gather(x, starts=starts, size=size):
                return jnp.concatenate([x[r0:r0 + size] for r0 in starts], axis=0)

            def scatter(xh, n=len(starts), size=size):
                zero = jnp.zeros((size, rows), F32)
                return jnp.concatenate([p_ for j in range(n) for p_ in (zero, xh[j * size:(j + 1) * size])],
                                       axis=0)

            inv_b = [t.astype(BF16) for t in inv]
            mid = each(lambda l, t: scatter(_dot(gather(jnp.where(m, l, 0.0)).astype(BF16), t)).astype(BF16),
                       lm, inv_b)
            inv = each(lambda t, md: t - scatter(_dot(gather(t).astype(BF16), md)), inv, mid)
        else:
            inv_b = [t.astype(BF16) for t in inv]
            mid = each(lambda l, t: _dot(jnp.where(m, l, 0.0).astype(BF16), t).astype(BF16), lm, inv_b)
            inv = each(lambda t, tb, md: t - _dot(tb, md), inv, inv_b, mid)
        size *= 2

    def state_dot(x, s_list):
        outs = [_dot_nt(jnp.concatenate([x['kk'][rs], x['r'][rs]], axis=0), s.astype(BF16))
                for rs, s in zip(seq_rows, s_list)]
        h = 2 * c
        if nseq == 1:
            return outs[0][:h], outs[0][h:]
        return (jnp.concatenate([t[:h] for t in outs], axis=0), jnp.concatenate([t[h:] for t in outs], axis=0))

    sd = each(state_dot, o, states)
    kkh, rh = [t[0] for t in sd], [t[1] for t in sd]
    rhs = each(lambda h, a, x: h + _dot(jnp.where(strict, a, 0.0).astype(BF16), x['v']), kkh, a_kk_k, o)
    u_b = each(lambda t, z: _dot(t.astype(BF16), z.astype(BF16)).astype(BF16), inv, rhs)
    y_s = each(lambda h, ak, ab, x, u: h + _dot(jnp.where(incl, ak, 0.0).astype(BF16), x['v'])
               - _dot(jnp.where(incl, ab, 0.0).astype(BF16), u), rh, a_r_k, a_r_b, o, u_b)

    def fold(ys):
        return [ys[rs][:c] + ys[rs][c:] for rs in seq_rows]

    def new_state(x, u, s_list):
        return [s * pc + _dot_tn(jnp.concatenate([x['v'][rs], u[rs]], axis=0),
                                 jnp.concatenate([x['kd'][rs], x['nbd'][rs]], axis=0))
                for rs, s, pc in zip(seq_rows, s_list, x['p_c'])]

    return [fold(ys) for ys in y_s], each(new_state, o, u_b, states)


def _rows_to_blockdiag(s):
    ext = jnp.concatenate([s, jnp.zeros_like(s)], axis=1)
    row = lax.broadcasted_iota(jnp.int32, ext.shape, 0)
    return jnp.where(row < HEAD_DIM, ext, pltpu.roll(ext, HEAD_DIM, axis=1))


def _blockdiag_to_rows(s_bd):
    row = lax.broadcasted_iota(jnp.int32, s_bd.shape, 0)
    return jnp.where(row < HEAD_DIM, s_bd, pltpu.roll(s_bd, HEAD_DIM, axis=1))[:, :HEAD_DIM]


def _wkv_prompt_kernel(r_ref, lw_ref, k_ref, v_ref, kk_ref, b_ref, s0_ref, *refs, c, t_len, nb, n_cast):
    cast_in, (y_ref, sout_ref), cast_out, s_ref = (refs[:n_cast], refs[n_cast:n_cast + 2],
                                                   refs[n_cast + 2:2 * n_cast + 2], refs[2 * n_cast + 2])
    for w_in, w_out in zip(cast_in, cast_out):
        w_out[...] = w_in[...].astype(BF16)
    ci = pl.program_id(1)
    units = [(i, q) for i in range(nb) for q in range(N_PAIRS)]

    @pl.when(ci == 0)
    def _():
        for i, q in units:
            s_ref[i, q] = _rows_to_blockdiag(s0_ref[i, q])

    masks = _wkv_masks(c)
    lanes = [slice(q * PAIR, (q + 1) * PAIR) for q in range(N_PAIRS)]

    def load(ref, i, q):
        x = ref[i, :, lanes[q]].astype(F32)
        if t_len % c:
            valid = lax.broadcasted_iota(jnp.int32, (c, PAIR), 0) + ci * c < t_len
            x = jnp.where(valid, x, 0.0)
        return x

    ins = [[[load(ref, i, q) for ref in (r_ref, lw_ref, k_ref, v_ref, kk_ref, b_ref)]] for i, q in units]
    ys, s_new = _wkv_units(ins, [[s_ref[i, q]] for i, q in units], c, masks)
    for (i, q), y, s in zip(units, ys, s_new):
        y_ref[i, :, lanes[q]] = y[0].astype(y_ref.dtype)
        s_ref[i, q] = s[0]

    @pl.when(ci == pl.num_programs(1) - 1)
    def _():
        for i, q in units:
            sout_ref[i, q] = _blockdiag_to_rows(s_ref[i, q])


def _wkv_sample_kernel(r_ref, lw_ref, k_ref, v_ref, kk_ref, b_ref, s0_ref, y_ref, sout_ref, *, c, nseq, groups):
    masks = _wkv_masks(c)
    lanes = [slice(q * PAIR, (q + 1) * PAIR) for q in range(N_PAIRS)]
    full = [ref[...].astype(F32) for ref in (r_ref, lw_ref, k_ref, v_ref, kk_ref, b_ref)]
    units = [(g, q) for g in range(groups) for q in range(N_PAIRS)]
    seqs = lambda g: range(g * nseq, (g + 1) * nseq)
    ins = [[[a[i * c:(i + 1) * c, lanes[q]] for a in full] for i in seqs(g)] for g, q in units]
    states = [[_rows_to_blockdiag(s0_ref[i, q]) for i in seqs(g)] for g, q in units]
    ys, s_new = _wkv_units(ins, states, c, masks)
    for (g, q), y, s in zip(units, ys, s_new):
        for j, i in enumerate(seqs(g)):
            y_ref[i * c:(i + 1) * c, lanes[q]] = y[j]
            sout_ref[i, q] = _blockdiag_to_rows(s[j])


def _cast_slabs(w, skip, n_steps, nc):
    n_rows = w.shape[0] - skip
    slab = next(r for r in range(BF16_ROWS, n_rows + 1, BF16_ROWS)
                if n_rows % r == 0 and skip % r == 0 and n_rows // r <= n_steps)
    last = n_rows // slab - 1
    first = skip // slab
    in_spec = pl.BlockSpec((slab, w.shape[1]), lambda b, i: (first + jnp.minimum(b * nc + i, last), 0))
    out_spec = pl.BlockSpec((slab, w.shape[1]), lambda b, i: (jnp.minimum(b * nc + i, last), 0))
    return in_spec, out_spec, jax.ShapeDtypeStruct((n_rows, w.shape[1]), BF16)


def _wkv(ins, state, bsz, t_len, t_valid, cast=()):
    m = bsz * t_len
    y_shape = jax.ShapeDtypeStruct((m, D_MODEL), F32)
    s_rows = state.reshape(bsz, N_PAIRS, PAIR, HEAD_DIM)
    s_shape = jax.ShapeDtypeStruct(s_rows.shape, F32)
    if t_len >= WKV_CHUNK:
        c, nb = WKV_CHUNK, min(bsz, WKV_BATCH)
        nc = t_len // c
        ins3 = [a.reshape(bsz, t_len, D_MODEL) for a in ins]
        blk = pl.BlockSpec((nb, c, D_MODEL), lambda b, i: (b, i, 0))
        sblk = pl.BlockSpec((nb, N_PAIRS, PAIR, HEAD_DIM), lambda b, i: (b, 0, 0, 0))
        slabs = [_cast_slabs(w, skip, (bsz // nb) * nc, nc) for w, skip in cast]
        y, s_out, *cast_out = pl.pallas_call(
            functools.partial(_wkv_prompt_kernel, c=c, t_len=t_valid, nb=nb, n_cast=len(cast)),
            grid=(bsz // nb, nc),
            in_specs=[blk] * 6 + [sblk] + [s[0] for s in slabs],
            out_specs=[blk, sblk] + [s[1] for s in slabs],
            out_shape=[jax.ShapeDtypeStruct((bsz, t_len, D_MODEL), BF16), s_shape] + [s[2] for s in slabs],
            scratch_shapes=[pltpu.VMEM((nb, N_PAIRS, PAIR, PAIR), F32)],
            compiler_params=_params(("arbitrary", "arbitrary")),
            name="wkv_prompt",
        )(*ins3, s_rows, *[w for w, _ in cast])
        return y.reshape(m, D_MODEL), s_out.reshape(state.shape), cast_out
    assert t_valid == t_len and not cast
    c, nseq = t_len, WKV_CHUNK // t_len
    groups = WKV_SHORT_GROUPS if bsz % (WKV_SHORT_GROUPS * nseq) == 0 else 1
    blk = pl.BlockSpec((groups * nseq * c, D_MODEL), lambda i: (i, 0))
    sblk = pl.BlockSpec((groups * nseq, N_PAIRS, PAIR, HEAD_DIM), lambda i: (i, 0, 0, 0))
    y, s_out = pl.pallas_call(
        functools.partial(_wkv_sample_kernel, c=c, nseq=nseq, groups=groups),
        grid=(bsz // (groups * nseq),),
        in_specs=[blk] * 6 + [sblk],
        out_specs=[blk, sblk],
        out_shape=[y_shape, s_shape],
        compiler_params=_params(("parallel",)),
        name="wkv_sample",
    )(*ins, s_rows)
    return y, s_out.reshape(state.shape), []


def _rk_post_kernel(x_ref, y_ref, bonus_ref, v_ref, g_ref, lnw_ref, lnb_ref, wo_ref, o_ref):
    ones = _head_ones()
    y = y_ref[...].astype(F32)
    inv_n = 1.0 / HEAD_DIM
    mean = _head_sum(y, ones) * inv_n
    yc = y - mean
    var = _head_sum(yc * yc, ones) * inv_n
    yn = yc * lax.rsqrt(var + GN_EPS) * lnw_ref[...] + lnb_ref[...]
    coef = bonus_ref[...]
    hi = coef.astype(BF16)
    lo = (coef - hi.astype(F32)).astype(BF16)
    spread = _head_select(transpose=True)
    bonus = (_dot(hi, spread) + _dot(lo, spread)) * v_ref[...].astype(F32)
    o = ((yn + bonus) * g_ref[...].astype(F32)).astype(BF16)
    o_ref[...] = x_ref[...] + _dot(o, wo_ref[...])


def _rk_post(x, y, bonus, v, g, ln_w, ln_b, w_o, tm):
    m = x.shape[0]
    row = pl.BlockSpec((tm, D_MODEL), lambda i: (i, 0))
    vec = _const_spec((1, D_MODEL))
    return pl.pallas_call(
        _rk_post_kernel,
        grid=(m // tm,),
        in_specs=[row, row, pl.BlockSpec((tm, PAIR), lambda i: (i, 0)), row, row, vec, vec,
                  _const_spec((D_MODEL, D_MODEL))],
        out_specs=row,
        out_shape=jax.ShapeDtypeStruct((m, D_MODEL), F32),
        compiler_params=_params(("parallel",)),
        name="rk_post",
    )(x, y, bonus, v, g, ln_w.reshape(1, D_MODEL), ln_b.reshape(1, D_MODEL), w_o)


def _conv_core(x, h, shifted, win_ref, cw_ref, wout_ref):
    z = _dot(h.astype(BF16), win_ref[...])
    gate_b = z[:, 0:D_MODEL]
    u = z[:, D_MODEL:2 * D_MODEL] * z[:, 2 * D_MODEL:3 * D_MODEL]
    cw = cw_ref[...]
    u1, u2 = shifted(u)
    conv = u2 * cw[0:1, :] + u1 * cw[1:2, :] + u * cw[2:3, :]
    out = x + _dot((gate_b * conv).astype(BF16), wout_ref[...])
    return out, u


def _conv_prompt_kernel(x_ref, h_ref, c0_ref, win_ref, cw_ref, wout_ref, o_ref, st_ref, carry_ref, *, tm):
    @pl.when(pl.program_id(1) == 0)
    def _():
        carry_ref[0:2, :] = c0_ref[...]

    def shifted(u):
        row = lax.broadcasted_iota(jnp.int32, u.shape, 0)
        c0 = carry_ref[0:1, :]
        c1 = carry_ref[1:2, :]
        u1 = jnp.where(row == 0, c1, pltpu.roll(u, 1, axis=0))
        u2 = jnp.where(row == 0, c0, jnp.where(row == 1, c1, pltpu.roll(u, 2, axis=0)))
        return u1, u2

    out, u = _conv_core(x_ref[...], h_ref[...], shifted, win_ref, cw_ref, wout_ref)
    o_ref[...] = out
    carry_ref[0:2, :] = u[tm - 2:tm, :]
    st_ref[...] = u[tm - SUBLANES:tm, :]


def _conv_sample_kernel(x_ref, h_ref, c0_ref, win_ref, cw_ref, wout_ref, o_ref, u_ref, *, ns, t_len):
    rows = ns * t_len
    c0 = jnp.broadcast_to(c0_ref[:, 0:1, :], (ns, t_len, D_MODEL)).reshape(rows, D_MODEL)
    c1 = jnp.broadcast_to(c0_ref[:, 1:2, :], (ns, t_len, D_MODEL)).reshape(rows, D_MODEL)

    def shifted(u):
        t = lax.broadcasted_iota(jnp.int32, u.shape, 0) & (t_len - 1)
        u1 = jnp.where(t == 0, c1, pltpu.roll(u, 1, axis=0))
        u2 = jnp.where(t == 0, c0, jnp.where(t == 1, c1, pltpu.roll(u, 2, axis=0)))
        return u1, u2

    out, u = _conv_core(x_ref[...], h_ref[...], shifted, win_ref, cw_ref, wout_ref)
    o_ref[...] = out
    u_ref[...] = u


def _conv_mix(x, hn, conv0, w_in, conv_w, w_out, bsz, t_len, t_valid, tm):
    m = bsz * t_len
    consts = [_const_spec((D_MODEL, 3 * D_MODEL)), _const_spec((CONV_W, D_MODEL)), _const_spec((D_MODEL, D_MODEL))]
    if t_len > tm:
        assert t_valid == t_len
        nt = t_len // tm
        row = pl.BlockSpec((tm, D_MODEL), lambda b, t: (b * nt + t, 0))
        out, tail = pl.pallas_call(
            functools.partial(_conv_prompt_kernel, tm=tm),
            grid=(bsz, nt),
            in_specs=[row, row, pl.BlockSpec((None, CONV_W - 1, D_MODEL), lambda b, t: (b, 0, 0))] + consts,
            out_specs=[row, pl.BlockSpec((None, SUBLANES, D_MODEL), lambda b, t: (b, 0, 0))],
            out_shape=[jax.ShapeDtypeStruct((m, D_MODEL), F32),
                       jax.ShapeDtypeStruct((bsz, SUBLANES, D_MODEL), F32)],
            scratch_shapes=[pltpu.VMEM((SUBLANES, D_MODEL), F32)],
            compiler_params=_params(("arbitrary", "arbitrary")),
            name="conv_prompt",
        )(x, hn, conv0, w_in, conv_w, w_out)
        return out, tail[:, SUBLANES - (CONV_W - 1):, :]
    ns = tm // t_len
    row = pl.BlockSpec((tm, D_MODEL), lambda i: (i, 0))
    out, u = pl.pallas_call(
        functools.partial(_conv_sample_kernel, ns=ns, t_len=t_len),
        grid=(m // tm,),
        in_specs=[row, row, pl.BlockSpec((ns, CONV_W - 1, D_MODEL), lambda i: (i, 0, 0))] + consts,
        out_specs=[row, row],
        out_shape=[jax.ShapeDtypeStruct((m, D_MODEL), F32)] * 2,
        compiler_params=_params(("parallel",)),
        name="conv_sample",
    )(x, hn, conv0, w_in, conv_w, w_out)
    return out, u.reshape(bsz, t_len, D_MODEL)[:, t_valid - (CONV_W - 1):t_valid, :]


def _trunk_head(streams, p, w00, tm, cast=()):
    heads, cast_out = [], []

    def start_state(s, name):
        if 'prefix' not in s:
            return s[name][0]
        state = heads[s['prefix']]['new_' + name]
        return jnp.broadcast_to(state, (1, s['bsz']) + state.shape[2:])[0]

    ffn_out = _ffn([s['x'] for s in streams], p['ffn_norm'][0, 0], w00, p['mix_norm'][0], tm)
    for s, (x, hn) in zip(streams, ffn_out):
        bsz, t_len, t_valid = s['bsz'], s['t_len'], s['t_valid']
        new_shift = hn.reshape(bsz, t_len, D_MODEL)[:, t_valid - 1]
        r, lw, k, v, kk, b, g, bonus = _rk_pre(hn, start_state(s, 'shift'), p['rk_pre'], bsz, t_len,
                                               min(tm, bsz * t_len))
        y, new_wkv, casts = _wkv([r, lw, k, v, kk, b], start_state(s, 'wkv'), bsz, t_len, t_valid,
                                 cast if s.get('casts') else ())
        cast_out += casts
        heads.append(dict(s, acts=(x, y, bonus, v, g), new_wkv=new_wkv[None], new_shift=new_shift[None]))
    return heads, cast_out


def _trunk_tail(heads, p, w_rest, tm):
    norm = p['ffn_norm']
    xs, tiles = [], []
    for s in heads:
        tile = min(tm, s['bsz'] * s['t_len'])
        tiles.append(2 * tile if s['t_len'] >= 2 * tile else tile)
        x, y, bonus, v, g = s['acts']
        xs.append(_rk_post(x, y, bonus, v, g, p['rk_ln_w'][0], p['rk_ln_b'][0], p['rk_w_o'][0], tiles[-1]))
    xs = [o[0] for o in _ffn(xs, norm[0, 1], w_rest[0], norm[1, 0], tm, emit_n=False)]
    mixed = _ffn(xs, norm[1, 0], w_rest[1], p['mix_norm'][1], tm, n_dtype=BF16)
    xs, convs = [], []
    for s, (x, hn), tile in zip(heads, mixed, tiles):
        if 'prefix' in s:
            conv0 = jnp.broadcast_to(convs[s['prefix']], (1, s['bsz']) + convs[s['prefix']].shape[2:])[0]
        else:
            conv0 = s['conv'][0]
        x, new_conv = _conv_mix(x, hn, conv0, p['sc_w_in'][0], p['sc_conv_w'][0], p['sc_w_out'][0],
                                s['bsz'], s['t_len'], s['t_valid'], tile)
        xs.append(x)
        convs.append(new_conv[None])
    wanted = [i for i, s in enumerate(heads) if s['want_y']]
    y_out = _ffn([xs[i] for i in wanted], norm[1, 1], w_rest[2], p['final_norm'], tm, emit_x=False)
    ys = [None] * len(heads)
    for i, o in zip(wanted, y_out):
        ys[i] = o[0]
    return list(zip(ys, convs))


def kernel(x_prompt, x_sample, state_wkv, state_shift, state_conv, meta, ffn_norm, ffn_w_gu, ffn_w_down,
           mix_norm, final_norm, rk_mu, rk_w_rkv, rk_w0, rk_w1, rk_w2, rk_a0, rk_a1, rk_a2, rk_g1, rk_g2,
           rk_k_k, rk_k_a, rk_r_k, rk_ln_w, rk_ln_b, rk_w_o, sc_w_in, sc_conv_w, sc_w_out):
    assert DEPTH == 2 and rk_mu.shape[0] == 1 and sc_w_in.shape[0] == 1
    bf = lambda a: a.astype(BF16)
    vec = lambda a: a.reshape(1, D_MODEL)
    p = {
        'ffn_norm': ffn_norm, 'mix_norm': mix_norm, 'final_norm': final_norm,
        'rk_pre': [rk_mu[0], bf(rk_w_rkv[0, 0]), bf(rk_w_rkv[0, 1]), bf(rk_w_rkv[0, 2]), vec(rk_w0[0]),
                   bf(rk_w1[0]), bf(rk_w2[0]), vec(rk_a0[0]), bf(rk_a1[0]), bf(rk_a2[0]), bf(rk_g1[0]),
                   bf(rk_g2[0]), vec(rk_k_k[0]), vec(rk_k_a[0]), rk_r_k.reshape(1, D_MODEL)],
        'rk_ln_w': rk_ln_w, 'rk_ln_b': rk_ln_b, 'rk_w_o': bf(rk_w_o),
        'sc_w_in': bf(sc_w_in), 'sc_conv_w': sc_conv_w, 'sc_w_out': bf(sc_w_out),
    }
    w00 = (bf(ffn_w_gu[0, 0])[None], bf(ffn_w_down[0, 0])[None], 0)
    later = ((ffn_w_gu.reshape(2 * DEPTH * D_MODEL, 2 * D_FF), D_MODEL),
             (ffn_w_down.reshape(2 * DEPTH * D_FF, D_MODEL), D_FF))
    bs, t_s, _ = x_sample.shape
    bp, seq, _ = x_prompt.shape
    meta_rows = jnp.pad(meta.astype(x_prompt.dtype), ((0, WKV_CHUNK - N_META), (0, 0)))
    zero_wkv = jnp.zeros((1, 1, N_HEADS, HEAD_DIM, HEAD_DIM), state_wkv.dtype)
    zero_shift = jnp.zeros((1, 1, D_MODEL), state_shift.dtype)
    zero_conv = jnp.zeros((1, 1, CONV_W - 1, D_MODEL), state_conv.dtype)
    streams = [dict(x=meta_rows, bsz=1, t_len=WKV_CHUNK, t_valid=N_META, want_y=False,
                    wkv=zero_wkv, shift=zero_shift, conv=zero_conv),
               dict(x=x_prompt.reshape(bp * seq, D_MODEL), bsz=bp, t_len=seq, t_valid=seq, want_y=True,
                    prefix=0, casts=True),
               dict(x=x_sample.reshape(bs * t_s, D_MODEL), bsz=bs, t_len=t_s, t_valid=t_s, want_y=True,
                    wkv=state_wkv, shift=state_shift, conv=state_conv)]
    heads, (wgu_rest, wd_rest) = _trunk_head(streams, p, w00, ROW_TILE, cast=later)
    wgu_rest = wgu_rest.reshape(2 * DEPTH - 1, D_MODEL, 2 * D_FF)
    wd_rest = wd_rest.reshape(2 * DEPTH - 1, D_FF, D_MODEL)
    w_rest = [(wgu_rest, wd_rest, i) for i in range(2 * DEPTH - 1)]
    _, (yp, conv_p), (ys, conv_s) = _trunk_tail(heads, p, w_rest, ROW_TILE)
    _, head_p, head_s = heads
    return (yp.reshape(bp, seq, D_MODEL), ys.reshape(bs, t_s, D_MODEL), head_p['new_wkv'], head_p['new_shift'],
            conv_p, head_s['new_wkv'], head_s['new_shift'], conv_s)
```

```python
import functools
import math

import jax
import jax.numpy as jnp
from jax import lax
from jax.experimental import pallas as pl
from jax.experimental.pallas import tpu as pltpu

D_MODEL = 1024
HEAD_DIM = 64
N_HEADS = D_MODEL // HEAD_DIM
D_FF = 2816
N_META = 16
DEPTH = 2
CONV_W = 3
RMS_EPS = 1e-6
GN_EPS = 64e-5
DECAY_SCALE = math.exp(-0.5)

SUBLANES = 8
BF16_ROWS = 16
MXU_DIM = 256
PAIR = 2 * HEAD_DIM
HEAD_SHIFT = HEAD_DIM.bit_length() - 1
KK_FLOOR = 1e-12
WKV_CHUNK = 64
STACK_ROWS = 2 * WKV_CHUNK
WKV_BATCH = 4
WKV_SHORT_GROUPS = 2
N_PAIRS = D_MODEL // PAIR
VMEM_LIMIT = 56 * 1024 * 1024
ROW_TILE = 512

FFN_CHUNKS = ((0, 6 * MXU_DIM), (6 * MXU_DIM, D_FF))

F32 = jnp.float32
BF16 = jnp.bfloat16


def _dot(a, b):
    return jnp.dot(a, b, preferred_element_type=F32)


def _dot_nt(a, b):
    return lax.dot_general(a, b, (((1,), (1,)), ((), ())), preferred_element_type=F32)


def _dot_tn(a, b):
    return lax.dot_general(a, b, (((0,), (0,)), ((), ())), preferred_element_type=F32)


def _rms(x, g):
    ms = jnp.mean(x * x, axis=-1, keepdims=True)
    return x * lax.rsqrt(ms + RMS_EPS) * g


def _sigmoid(x):
    return 1.0 / (1.0 + jnp.exp(-x))


def _head_ones():
    r = lax.broadcasted_iota(jnp.int32, (PAIR, PAIR), 0) >> HEAD_SHIFT
    c = lax.broadcasted_iota(jnp.int32, (PAIR, PAIR), 1) >> HEAD_SHIFT
    return jnp.where(r == c, 1.0, 0.0).astype(BF16)


def _head_sum(x, ones):
    xb = x.astype(BF16)
    return jnp.concatenate([_dot(xb[:, j * PAIR:(j + 1) * PAIR], ones) for j in range(x.shape[1] // PAIR)],
                           axis=1)


def _const_spec(shape):
    nd = len(shape)
    return pl.BlockSpec(shape, lambda *_: (0,) * nd, pipeline_mode=pl.Buffered(1))


def _params(sem):
    return pltpu.CompilerParams(dimension_semantics=sem, vmem_limit_bytes=VMEM_LIMIT)


def _ffn_kernel(g_ref, wgu_ref, wd_ref, g2_ref, *refs, bounds, emit_x, emit_n):
    n_out = int(emit_x) + int(emit_n)
    x_refs, out_refs = refs[:len(bounds)], refs[len(bounds):]

    def tile(x_ref, outs):
        x = x_ref[...]
        xn = _rms(x, g_ref[...]).astype(BF16)
        acc = None
        for lo, hi in FFN_CHUNKS:
            gate = _dot(xn, wgu_ref[:, lo:hi])
            up = _dot(xn, wgu_ref[:, D_FF + lo:D_FF + hi])
            act = (gate * _sigmoid(gate) * up).astype(BF16)
            part = _dot(act, wd_ref[lo:hi, :])
            acc = part if acc is None else acc + part
        out = x + 0.5 * acc
        if emit_x:
            outs[0][...] = out
        if emit_n:
            outs[-1][...] = _rms(out, g2_ref[...]).astype(outs[-1].dtype)

    step = pl.program_id(0)
    for s, (lo, hi) in enumerate(bounds):
        args = (x_refs[s], out_refs[s * n_out:(s + 1) * n_out])
        if len(bounds) == 1:
            tile(*args)
        else:
            pl.when((step >= lo) & (step < hi))(functools.partial(tile, *args))


def _ffn(xs, g, w, g2, tm, emit_x=True, emit_n=True, n_dtype=F32):
    wgu, wd, idx = w
    tiles = [min(tm, x.shape[0]) for x in xs]
    n_tiles = [x.shape[0] // t for x, t in zip(xs, tiles)]
    bounds, lo = [], 0
    for nt in n_tiles:
        bounds.append((lo, lo + nt))
        lo += nt
    dtypes = [F32] * int(emit_x) + [n_dtype] * int(emit_n)

    def weight_spec(rows, cols):
        return pl.BlockSpec((None, rows, cols), lambda i: (idx, 0, 0), pipeline_mode=pl.Buffered(1))

    def row_spec(first, nt, tile):
        return pl.BlockSpec((tile, D_MODEL), lambda i: (jnp.clip(i - first, 0, nt - 1), 0))

    rows = [row_spec(b[0], nt, t) for b, nt, t in zip(bounds, n_tiles, tiles)]
    vec = _const_spec((1, D_MODEL))
    outs = pl.pallas_call(
        functools.partial(_ffn_kernel, bounds=tuple(bounds), emit_x=emit_x, emit_n=emit_n),
        grid=(lo,),
        in_specs=[vec, weight_spec(D_MODEL, 2 * D_FF), weight_spec(D_FF, D_MODEL), vec] + rows,
        out_specs=[r for r in rows for _ in dtypes],
        out_shape=[jax.ShapeDtypeStruct(x.shape, dt) for x in xs for dt in dtypes],
        compiler_params=_params(("arbitrary",)),
        name="ffn",
    )(g.reshape(1, D_MODEL), wgu, wd, g2.reshape(1, D_MODEL), *xs)
    return [tuple(outs[s * len(dtypes):(s + 1) * len(dtypes)]) for s in range(len(xs))]


N_RK_IN = 15
RK_COLS = MXU_DIM
N_RK_OUT = 8


def _head_select(transpose=False):
    shape = (PAIR, D_MODEL) if transpose else (D_MODEL, PAIR)
    chan = lax.broadcasted_iota(jnp.int32, shape, 1 if transpose else 0) >> HEAD_SHIFT
    head = lax.broadcasted_iota(jnp.int32, shape, 0 if transpose else 1)
    return jnp.where(chan == head, 1.0, 0.0).astype(BF16)


def _rk_pre_core(h, prev, p, out_refs):
    mu_ref, wr, wk, wv, w0, w1, w2, a0, a1, a2, g1, g2, k_k, k_a, r_k = p
    mu = mu_ref[...]
    xx = prev - h
    xr, xw, xk, xv, xa, xg = [(h + xx * mu[c:c + 1, :]).astype(BF16) for c in range(6)]
    w_lo = jnp.tanh(_dot(xw, w1[...])).astype(BF16)
    a_lo = _dot(xa, a1[...]).astype(BF16)
    g_lo = _sigmoid(_dot(xg, g1[...])).astype(BF16)
    ones, select = _head_ones(), _head_select()
    bonus = None
    for j in range(D_MODEL // RK_COLS):
        sl = slice(j * RK_COLS, (j + 1) * RK_COLS)
        r = _dot(xr, wr[:, sl])
        k = _dot(xk, wk[:, sl])
        v = _dot(xv, wv[:, sl])
        lw = -DECAY_SCALE * _sigmoid(w0[:, sl] + _dot(w_lo, w2[:, sl]))
        a = _sigmoid(a0[:, sl] + _dot(a_lo, a2[:, sl]))
        g = _dot(g_lo, g2[:, sl])
        kk = k * k_k[:, sl]
        ss = _head_sum(kk * kk, ones)
        kk = kk * lax.rsqrt(jnp.maximum(ss, KK_FLOOR * KK_FLOOR))
        kmod = k * (1.0 + (a - 1.0) * k_a[:, sl])
        part = _dot((r * kmod * r_k[:, sl]).astype(BF16), select[sl, :])
        bonus = part if bonus is None else bonus + part
        for ref, val in zip(out_refs[:-1], (r, lw, kmod, v, kk, kk * a, g)):
            ref[:, sl] = val.astype(ref.dtype)
    out_refs[-1][...] = bonus


def _rk_pre_prompt_kernel(h_ref, s0_ref, *refs, tm):
    p, out_refs, carry_ref = refs[:N_RK_IN], refs[N_RK_IN:N_RK_IN + N_RK_OUT], refs[N_RK_IN + N_RK_OUT]

    @pl.when(pl.program_id(1) == 0)
    def _():
        carry_ref[0:1, :] = s0_ref[...]

    h = h_ref[...]
    row = lax.broadcasted_iota(jnp.int32, h.shape, 0)
    prev = jnp.where(row == 0, carry_ref[0:1, :], pltpu.roll(h, 1, axis=0))
    carry_ref[0:1, :] = h[tm - 1:tm, :]
    _rk_pre_core(h, prev, p, out_refs)


def _rk_pre_sample_kernel(h_ref, s0_ref, *refs, ns, t_len):
    p, out_refs = refs[:N_RK_IN], refs[N_RK_IN:N_RK_IN + N_RK_OUT]
    h = h_ref[...]
    rows = ns * t_len
    s0 = jnp.broadcast_to(s0_ref[...], (ns, t_len, D_MODEL)).reshape(rows, D_MODEL)
    t = lax.broadcasted_iota(jnp.int32, h.shape, 0) & (t_len - 1)
    prev = jnp.where(t == 0, s0, pltpu.roll(h, 1, axis=0))
    _rk_pre_core(h, prev, p, out_refs)


def _rk_param_specs(p):
    return [_const_spec(a.shape) for a in p]


def _rk_pre(hn, shift0, p, bsz, t_len, tm):
    m = bsz * t_len
    assert len(p) == N_RK_IN
    out_shape = ([jax.ShapeDtypeStruct((m, D_MODEL), F32 if i == 1 else BF16) for i in range(N_RK_OUT - 1)]
                 + [jax.ShapeDtypeStruct((m, PAIR), F32)])
    s0 = shift0.reshape(bsz, 1, D_MODEL)
    if t_len > tm:
        nt = t_len // tm
        row = pl.BlockSpec((tm, D_MODEL), lambda b, t: (b * nt + t, 0))
        return pl.pallas_call(
            functools.partial(_rk_pre_prompt_kernel, tm=tm),
            grid=(bsz, nt),
            in_specs=[row, pl.BlockSpec((None, 1, D_MODEL), lambda b, t: (b, 0, 0))] + _rk_param_specs(p),
            out_specs=[row] * (N_RK_OUT - 1) + [pl.BlockSpec((tm, PAIR), lambda b, t: (b * nt + t, 0))],
            out_shape=out_shape,
            scratch_shapes=[pltpu.VMEM((SUBLANES, D_MODEL), F32)],
            compiler_params=_params(("arbitrary", "arbitrary")),
            name="rk_pre_prompt",
        )(hn, s0, *p)
    ns = tm // t_len
    row = pl.BlockSpec((tm, D_MODEL), lambda i: (i, 0))
    return pl.pallas_call(
        functools.partial(_rk_pre_sample_kernel, ns=ns, t_len=t_len),
        grid=(m // tm,),
        in_specs=[row, pl.BlockSpec((ns, 1, D_MODEL), lambda i: (i, 0, 0))] + _rk_param_specs(p),
        out_specs=[row] * (N_RK_OUT - 1) + [pl.BlockSpec((tm, PAIR), lambda i: (i, 0))],
        out_shape=out_shape,
        compiler_params=_params(("parallel",)),
        name="rk_pre_sample",
    )(hn, s0, *p)


def _wkv_masks(c):
    rows = STACK_ROWS
    sh = c.bit_length() - 1
    row = lax.broadcasted_iota(jnp.int32, (rows, rows), 0)
    col = lax.broadcasted_iota(jnp.int32, (rows, rows), 1)
    same = (row >> sh) == (col >> sh)
    strict = same & (row > col)
    incl = same & (row >= col)
    base = (row >> 1) == (col >> 1)
    merges = []
    s = 2
    while s < c:
        b = s.bit_length() - 1
        merges.append(((row >> (b + 1)) == (col >> (b + 1))) & (((row >> b) & 1) == 1) & (((col >> b) & 1) == 0))
        s *= 2
    eye = jnp.where(row == col, 1.0, 0.0).astype(F32)
    return strict, incl, base, merges, eye


def _cumsum_rows(x, c):
    t = lax.broadcasted_iota(jnp.int32, x.shape, 0)
    s = 1
    while s < c:
        x = x + jnp.where(t >= s, pltpu.roll(x, s, axis=0), 0.0)
        s *= 2
    return x


def _wkv_units(ins, states, c, masks):
    strict, incl, base, merges, eye = masks
    nseq = len(ins[0])
    rows = STACK_ROWS
    assert nseq * 2 * c == rows
    lane_h = lax.broadcasted_iota(jnp.int32, (c, PAIR), 1) >> HEAD_SHIFT
    seq_rows = [slice(i * 2 * c, (i + 1) * 2 * c) for i in range(nseq)]

    def stack(xs):
        return jnp.concatenate([jnp.where(lane_h == h, x, 0.0) for x in xs for h in range(2)],
                               axis=0).astype(BF16)

    def each(f, *lists):
        return [f(*xs) for xs in zip(*lists)]

    def prep(seqs):
        per = []
        for r, lw, k, v, kk, b in seqs:
            cum = _cumsum_rows(lw, c)
            last = cum[c - 1:c, :]
            to_end = jnp.exp(last - cum)
            p_inv = jnp.exp(-cum)
            per.append(dict(p_c=jnp.exp(last), kk=kk * jnp.exp(cum - lw), r=r * jnp.exp(cum), k=k * p_inv,
                            b=b * p_inv, v=v, kd=k * to_end, nbd=-(b * to_end)))
        out = {name: stack([d[name] for d in per]) for name in ('kk', 'r', 'k', 'b', 'v', 'kd', 'nbd')}
        out['p_c'] = [d['p_c'] for d in per]
        return out

    o = [prep(x) for x in ins]
    kb = [jnp.concatenate([x['k'], x['b']], axis=0) for x in o]
    a_kk = each(lambda x, w_: _dot_nt(x['kk'], w_), o, kb)
    a_r = each(lambda x, w_: _dot_nt(x['r'], w_), o, kb)
    a_kk_k, a_kk_b = [a[:, :rows] for a in a_kk], [a[:, rows:] for a in a_kk]
    a_r_k, a_r_b = [a[:, :rows] for a in a_r], [a[:, rows:] for a in a_r]

    lm = [jnp.where(strict, a, 0.0) for a in a_kk_b]
    inv = [eye - jnp.where(base, l, 0.0) for l in lm]
    size = 2
    for m in merges:
        if size >= SUBLANES:
            starts = range(size, rows, 2 * size)

            def gather(x, starts=starts, size=size):
                return jnp.concatenate([x[r0:r0 + size] for r0 in starts], axis=0)

            def scatter(xh, n=len(starts), size=size):
                zero = jnp.zeros((size, rows), F32)
                return jnp.concatenate([p_ for j in range(n) for p_ in (zero, xh[j * size:(j + 1) * size])],
                                       axis=0)

            inv_b = [t.astype(BF16) for t in inv]
            mid = each(lambda l, t: scatter(_dot(gather(jnp.where(m, l, 0.0)).astype(BF16), t)).astype(BF16),
                       lm, inv_b)
            inv = each(lambda t, md: t - scatter(_dot(gather(t).astype(BF16), md)), inv, mid)
        else:
            inv_b = [t.astype(BF16) for t in inv]
            mid = each(lambda l, t: _dot(jnp.where(m, l, 0.0).astype(BF16), t).astype(BF16), lm, inv_b)
            inv = each(lambda t, tb, md: t - _dot(tb, md), inv, inv_b, mid)
        size *= 2

    def state_dot(x, s_list):
        outs = [_dot_nt(jnp.concatenate([x['kk'][rs], x['r'][rs]], axis=0), s.astype(BF16))
                for rs, s in zip(seq_rows, s_list)]
        h = 2 * c
        if nseq == 1:
            return outs[0][:h], outs[0][h:]
        return (jnp.concatenate([t[:h] for t in outs], axis=0), jnp.concatenate([t[h:] for t in outs], axis=0))

    sd = each(state_dot, o, states)
    kkh, rh = [t[0] for t in sd], [t[1] for t in sd]
    rhs = each(lambda h, a, x: h + _dot(jnp.where(strict, a, 0.0).astype(BF16), x['v']), kkh, a_kk_k, o)
    u_b = each(lambda t, z: _dot(t.astype(BF16), z.astype(BF16)).astype(BF16), inv, rhs)
    y_s = each(lambda h, ak, ab, x, u: h + _dot(jnp.where(incl, ak, 0.0).astype(BF16), x['v'])
               - _dot(jnp.where(incl, ab, 0.0).astype(BF16), u), rh, a_r_k, a_r_b, o, u_b)

    def fold(ys):
        return [ys[rs][:c] + ys[rs][c:] for rs in seq_rows]

    def new_state(x, u, s_list):
        return [s * pc + _dot_tn(jnp.concatenate([x['v'][rs], u[rs]], axis=0),
                                 jnp.concatenate([x['kd'][rs], x['nbd'][rs]], axis=0))
                for rs, s, pc in zip(seq_rows, s_list, x['p_c'])]

    return [fold(ys) for ys in y_s], each(new_state, o, u_b, states)


def _rows_to_blockdiag(s):
    ext = jnp.concatenate([s, jnp.zeros_like(s)], axis=1)
    row = lax.broadcasted_iota(jnp.int32, ext.shape, 0)
    return jnp.where(row < HEAD_DIM, ext, pltpu.roll(ext, HEAD_DIM, axis=1))


def _blockdiag_to_rows(s_bd):
    row = lax.broadcasted_iota(jnp.int32, s_bd.shape, 0)
    return jnp.where(row < HEAD_DIM, s_bd, pltpu.roll(s_bd, HEAD_DIM, axis=1))[:, :HEAD_DIM]


def _wkv_prompt_kernel(r_ref, lw_ref, k_ref, v_ref, kk_ref, b_ref, s0_ref, *refs, c, t_len, nb, n_cast):
    cast_in, (y_ref, sout_ref), cast_out, s_ref = (refs[:n_cast], refs[n_cast:n_cast + 2],
                                                   refs[n_cast + 2:2 * n_cast + 2], refs[2 * n_cast + 2])
    for w_in, w_out in zip(cast_in, cast_out):
        w_out[...] = w_in[...].astype(BF16)
    ci = pl.program_id(1)
    units = [(i, q) for i in range(nb) for q in range(N_PAIRS)]

    @pl.when(ci == 0)
    def _():
        for i, q in units:
            s_ref[i, q] = _rows_to_blockdiag(s0_ref[i, q])

    masks = _wkv_masks(c)
    lanes = [slice(q * PAIR, (q + 1) * PAIR) for q in range(N_PAIRS)]

    def load(ref, i, q):
        x = ref[i, :, lanes[q]].astype(F32)
        if t_len % c:
            valid = lax.broadcasted_iota(jnp.int32, (c, PAIR), 0) + ci * c < t_len
            x = jnp.where(valid, x, 0.0)
        return x

    ins = [[[load(ref, i, q) for ref in (r_ref, lw_ref, k_ref, v_ref, kk_ref, b_ref)]] for i, q in units]
    ys, s_new = _wkv_units(ins, [[s_ref[i, q]] for i, q in units], c, masks)
    for (i, q), y, s in zip(units, ys, s_new):
        y_ref[i, :, lanes[q]] = y[0].astype(y_ref.dtype)
        s_ref[i, q] = s[0]

    @pl.when(ci == pl.num_programs(1) - 1)
    def _():
        for i, q in units:
            sout_ref[i, q] = _blockdiag_to_rows(s_ref[i, q])


def _wkv_sample_kernel(r_ref, lw_ref, k_ref, v_ref, kk_ref, b_ref, s0_ref, y_ref, sout_ref, *, c, nseq, groups):
    masks = _wkv_masks(c)
    lanes = [slice(q * PAIR, (q + 1) * PAIR) for q in range(N_PAIRS)]
    full = [ref[...].astype(F32) for ref in (r_ref, lw_ref, k_ref, v_ref, kk_ref, b_ref)]
    units = [(g, q) for g in range(groups) for q in range(N_PAIRS)]
    seqs = lambda g: range(g * nseq, (g + 1) * nseq)
    ins = [[[a[i * c:(i + 1) * c, lanes[q]] for a in full] for i in seqs(g)] for g, q in units]
    states = [[_rows_to_blockdiag(s0_ref[i, q]) for i in seqs(g)] for g, q in units]
    ys, s_new = _wkv_units(ins, states, c, masks)
    for (g, q), y, s in zip(units, ys, s_new):
        for j, i in enumerate(seqs(g)):
            y_ref[i * c:(i + 1) * c, lanes[q]] = y[j]
            sout_ref[i, q] = _blockdiag_to_rows(s[j])


def _cast_slabs(w, skip, n_steps, nc):
    n_rows = w.shape[0] - skip
    slab = next(r for r in range(BF16_ROWS, n_rows + 1, BF16_ROWS)
                if n_rows % r == 0 and skip % r == 0 and n_rows // r <= n_steps)
    last = n_rows // slab - 1
    first = skip // slab
    in_spec = pl.BlockSpec((slab, w.shape[1]), lambda b, i: (first + jnp.minimum(b * nc + i, last), 0))
    out_spec = pl.BlockSpec((slab, w.shape[1]), lambda b, i: (jnp.minimum(b * nc + i, last), 0))
    return in_spec, out_spec, jax.ShapeDtypeStruct((n_rows, w.shape[1]), BF16)


def _wkv(ins, state, bsz, t_len, t_valid, cast=()):
    m = bsz * t_len
    y_shape = jax.ShapeDtypeStruct((m, D_MODEL), F32)
    s_rows = state.reshape(bsz, N_PAIRS, PAIR, HEAD_DIM)
    s_shape = jax.ShapeDtypeStruct(s_rows.shape, F32)
    if t_len >= WKV_CHUNK:
        c, nb = WKV_CHUNK, min(bsz, WKV_BATCH)
        nc = t_len // c
        ins3 = [a.reshape(bsz, t_len, D_MODEL) for a in ins]
        blk = pl.BlockSpec((nb, c, D_MODEL), lambda b, i: (b, i, 0))
        sblk = pl.BlockSpec((nb, N_PAIRS, PAIR, HEAD_DIM), lambda b, i: (b, 0, 0, 0))
        slabs = [_cast_slabs(w, skip, (bsz // nb) * nc, nc) for w, skip in cast]
        y, s_out, *cast_out = pl.pallas_call(
            functools.partial(_wkv_prompt_kernel, c=c, t_len=t_valid, nb=nb, n_cast=len(cast)),
            grid=(bsz // nb, nc),
            in_specs=[blk] * 6 + [sblk] + [s[0] for s in slabs],
            out_specs=[blk, sblk] + [s[1] for s in slabs],
            out_shape=[jax.ShapeDtypeStruct((bsz, t_len, D_MODEL), BF16), s_shape] + [s[2] for s in slabs],
            scratch_shapes=[pltpu.VMEM((nb, N_PAIRS, PAIR, PAIR), F32)],
            compiler_params=_params(("arbitrary", "arbitrary")),
            name="wkv_prompt",
        )(*ins3, s_rows, *[w for w, _ in cast])
        return y.reshape(m, D_MODEL), s_out.reshape(state.shape), cast_out
    assert t_valid == t_len and not cast
    c, nseq = t_len, WKV_CHUNK // t_len
    groups = WKV_SHORT_GROUPS if bsz % (WKV_SHORT_GROUPS * nseq) == 0 else 1
    blk = pl.BlockSpec((groups * nseq * c, D_MODEL), lambda i: (i, 0))
    sblk = pl.BlockSpec((groups * nseq, N_PAIRS, PAIR, HEAD_DIM), lambda i: (i, 0, 0, 0))
    y, s_out = pl.pallas_call(
        functools.partial(_wkv_sample_kernel, c=c, nseq=nseq, groups=groups),
        grid=(bsz // (groups * nseq),),
        in_specs=[blk] * 6 + [sblk],
        out_specs=[blk, sblk],
        out_shape=[y_shape, s_shape],
        compiler_params=_params(("parallel",)),
        name="wkv_sample",
    )(*ins, s_rows)
    return y, s_out.reshape(state.shape), []


def _rk_post_kernel(x_ref, y_ref, bonus_ref, v_ref, g_ref, lnw_ref, lnb_ref, wo_ref, o_ref):
    ones = _head_ones()
    y = y_ref[...].astype(F32)
    inv_n = 1.0 / HEAD_DIM
    mean = _head_sum(y, ones) * inv_n
    yc = y - mean
    var = _head_sum(yc * yc, ones) * inv_n
    yn = yc * lax.rsqrt(var + GN_EPS) * lnw_ref[...] + lnb_ref[...]
    coef = bonus_ref[...]
    hi = coef.astype(BF16)
    lo = (coef - hi.astype(F32)).astype(BF16)
    spread = _head_select(transpose=True)
    bonus = (_dot(hi, spread) + _dot(lo, spread)) * v_ref[...].astype(F32)
    o = ((yn + bonus) * g_ref[...].astype(F32)).astype(BF16)
    o_ref[...] = x_ref[...] + _dot(o, wo_ref[...])


def _rk_post(x, y, bonus, v, g, ln_w, ln_b, w_o, tm):
    m = x.shape[0]
    row = pl.BlockSpec((tm, D_MODEL), lambda i: (i, 0))
    vec = _const_spec((1, D_MODEL))
    return pl.pallas_call(
        _rk_post_kernel,
        grid=(m // tm,),
        in_specs=[row, row, pl.BlockSpec((tm, PAIR), lambda i: (i, 0)), row, row, vec, vec,
                  _const_spec((D_MODEL, D_MODEL))],
        out_specs=row,
        out_shape=jax.ShapeDtypeStruct((m, D_MODEL), F32),
        compiler_params=_params(("parallel",)),
        name="rk_post",
    )(x, y, bonus, v, g, ln_w.reshape(1, D_MODEL), ln_b.reshape(1, D_MODEL), w_o)


def _conv_core(x, h, shifted, win_ref, cw_ref, wout_ref):
    z = _dot(h.astype(BF16), win_ref[...])
    gate_b = z[:, 0:D_MODEL]
    u = z[:, D_MODEL:2 * D_MODEL] * z[:, 2 * D_MODEL:3 * D_MODEL]
    cw = cw_ref[...]
    u1, u2 = shifted(u)
    conv = u2 * cw[0:1, :] + u1 * cw[1:2, :] + u * cw[2:3, :]
    out = x + _dot((gate_b * conv).astype(BF16), wout_ref[...])
    return out, u


def _conv_prompt_kernel(x_ref, h_ref, c0_ref, win_ref, cw_ref, wout_ref, o_ref, st_ref, carry_ref, *, tm):
    @pl.when(pl.program_id(1) == 0)
    def _():
        carry_ref[0:2, :] = c0_ref[...]

    def shifted(u):
        row = lax.broadcasted_iota(jnp.int32, u.shape, 0)
        c0 = carry_ref[0:1, :]
        c1 = carry_ref[1:2, :]
        u1 = jnp.where(row == 0, c1, pltpu.roll(u, 1, axis=0))
        u2 = jnp.where(row == 0, c0, jnp.where(row == 1, c1, pltpu.roll(u, 2, axis=0)))
        return u1, u2

    out, u = _conv_core(x_ref[...], h_ref[...], shifted, win_ref, cw_ref, wout_ref)
    o_ref[...] = out
    carry_ref[0:2, :] = u[tm - 2:tm, :]
    st_ref[...] = u[tm - SUBLANES:tm, :]


def _conv_sample_kernel(x_ref, h_ref, c0_ref, win_ref, cw_ref, wout_ref, o_ref, u_ref, *, ns, t_len):
    rows = ns * t_len
    c0 = jnp.broadcast_to(c0_ref[:, 0:1, :], (ns, t_len, D_MODEL)).reshape(rows, D_MODEL)
    c1 = jnp.broadcast_to(c0_ref[:, 1:2, :], (ns, t_len, D_MODEL)).reshape(rows, D_MODEL)

    def shifted(u):
        t = lax.broadcasted_iota(jnp.int32, u.shape, 0) & (t_len - 1)
        u1 = jnp.where(t == 0, c1, pltpu.roll(u, 1, axis=0))
        u2 = jnp.where(t == 0, c0, jnp.where(t == 1, c1, pltpu.roll(u, 2, axis=0)))
        return u1, u2

    out, u = _conv_core(x_ref[...], h_ref[...], shifted, win_ref, cw_ref, wout_ref)
    o_ref[...] = out
    u_ref[...] = u


def _conv_mix(x, hn, conv0, w_in, conv_w, w_out, bsz, t_len, t_valid, tm):
    m = bsz * t_len
    consts = [_const_spec((D_MODEL, 3 * D_MODEL)), _const_spec((CONV_W, D_MODEL)), _const_spec((D_MODEL, D_MODEL))]
    if t_len > tm:
        assert t_valid == t_len
        nt = t_len // tm
        row = pl.BlockSpec((tm, D_MODEL), lambda b, t: (b * nt + t, 0))
        out, tail = pl.pallas_call(
            functools.partial(_conv_prompt_kernel, tm=tm),
            grid=(bsz, nt),
            in_specs=[row, row, pl.BlockSpec((None, CONV_W - 1, D_MODEL), lambda b, t: (b, 0, 0))] + consts,
            out_specs=[row, pl.BlockSpec((None, SUBLANES, D_MODEL), lambda b, t: (b, 0, 0))],
            out_shape=[jax.ShapeDtypeStruct((m, D_MODEL), F32),
                       jax.ShapeDtypeStruct((bsz, SUBLANES, D_MODEL), F32)],
            scratch_shapes=[pltpu.VMEM((SUBLANES, D_MODEL), F32)],
            compiler_params=_params(("arbitrary", "arbitrary")),
            name="conv_prompt",
        )(x, hn, conv0, w_in, conv_w, w_out)
        return out, tail[:, SUBLANES - (CONV_W - 1):, :]
    ns = tm // t_len
    row = pl.BlockSpec((tm, D_MODEL), lambda i: (i, 0))
    out, u = pl.pallas_call(
        functools.partial(_conv_sample_kernel, ns=ns, t_len=t_len),
        grid=(m // tm,),
        in_specs=[row, row, pl.BlockSpec((ns, CONV_W - 1, D_MODEL), lambda i: (i, 0, 0))] + consts,
        out_specs=[row, row],
        out_shape=[jax.ShapeDtypeStruct((m, D_MODEL), F32)] * 2,
        compiler_params=_params(("parallel",)),
        name="conv_sample",
    )(x, hn, conv0, w_in, conv_w, w_out)
    return out, u.reshape(bsz, t_len, D_MODEL)[:, t_valid - (CONV_W - 1):t_valid, :]


def _trunk_head(streams, p, w00, tm, cast=()):
    heads, cast_out = [], []

    def start_state(s, name):
        if 'prefix' not in s:
            return s[name][0]
        state = heads[s['prefix']]['new_' + name]
        return jnp.broadcast_to(state, (1, s['bsz']) + state.shape[2:])[0]

    ffn_out = _ffn([s['x'] for s in streams], p['ffn_norm'][0, 0], w00, p['mix_norm'][0], tm)
    for s, (x, hn) in zip(streams, ffn_out):
        bsz, t_len, t_valid = s['bsz'], s['t_len'], s['t_valid']
        new_shift = hn.reshape(bsz, t_len, D_MODEL)[:, t_valid - 1]
        r, lw, k, v, kk, b, g, bonus = _rk_pre(hn, start_state(s, 'shift'), p['rk_pre'], bsz, t_len,
                                               min(tm, bsz * t_len))
        y, new_wkv, casts = _wkv([r, lw, k, v, kk, b], start_state(s, 'wkv'), bsz, t_len, t_valid,
                                 cast if s.get('casts') else ())
        cast_out += casts
        heads.append(dict(s, acts=(x, y, bonus, v, g), new_wkv=new_wkv[None], new_shift=new_shift[None]))
    return heads, cast_out


def _trunk_tail(heads, p, w_rest, tm):
    norm = p['ffn_norm']
    xs, tiles = [], []
    for s in heads:
        tile = min(tm, s['bsz'] * s['t_len'])
        tiles.append(2 * tile if s['t_len'] >= 2 * tile else tile)
        x, y, bonus, v, g = s['acts']
        xs.append(_rk_post(x, y, bonus, v, g, p['rk_ln_w'][0], p['rk_ln_b'][0], p['rk_w_o'][0], tiles[-1]))
    xs = [o[0] for o in _ffn(xs, norm[0, 1], w_rest[0], norm[1, 0], tm, emit_n=False)]
    mixed = _ffn(xs, norm[1, 0], w_rest[1], p['mix_norm'][1], tm, n_dtype=BF16)
    xs, convs = [], []
    for s, (x, hn), tile in zip(heads, mixed, tiles):
        if 'prefix' in s:
            conv0 = jnp.broadcast_to(convs[s['prefix']], (1, s['bsz']) + convs[s['prefix']].shape[2:])[0]
        else:
            conv0 = s['conv'][0]
        x, new_conv = _conv_mix(x, hn, conv0, p['sc_w_in'][0], p['sc_conv_w'][0], p['sc_w_out'][0],
                                s['bsz'], s['t_len'], s['t_valid'], tile)
        xs.append(x)
        convs.append(new_conv[None])
    wanted = [i for i, s in enumerate(heads) if s['want_y']]
    y_out = _ffn([xs[i] for i in wanted], norm[1, 1], w_rest[2], p['final_norm'], tm, emit_x=False)
    ys = [None] * len(heads)
    for i, o in zip(wanted, y_out):
        ys[i] = o[0]
    return list(zip(ys, convs))


def kernel(x_prompt, x_sample, state_wkv, state_shift, state_conv, meta, ffn_norm, ffn_w_gu, ffn_w_down,
           mix_norm, final_norm, rk_mu, rk_w_rkv, rk_w0, rk_w1, rk_w2, rk_a0, rk_a1, rk_a2, rk_g1, rk_g2,
           rk_k_k, rk_k_a, rk_r_k, rk_ln_w, rk_ln_b, rk_w_o, sc_w_in, sc_conv_w, sc_w_out):
    assert DEPTH == 2 and rk_mu.shape[0] == 1 and sc_w_in.shape[0] == 1
    bf = lambda a: a.astype(BF16)
    vec = lambda a: a.reshape(1, D_MODEL)
    p = {
        'ffn_norm': ffn_norm, 'mix_norm': mix_norm, 'final_norm': final_norm,
        'rk_pre': [rk_mu[0], bf(rk_w_rkv[0, 0]), bf(rk_w_rkv[0, 1]), bf(rk_w_rkv[0, 2]), vec(rk_w0[0]),
                   bf(rk_w1[0]), bf(rk_w2[0]), vec(rk_a0[0]), bf(rk_a1[0]), bf(rk_a2[0]), bf(rk_g1[0]),
                   bf(rk_g2[0]), vec(rk_k_k[0]), vec(rk_k_a[0]), rk_r_k.reshape(1, D_MODEL)],
        'rk_ln_w': rk_ln_w, 'rk_ln_b': rk_ln_b, 'rk_w_o': bf(rk_w_o),
        'sc_w_in': bf(sc_w_in), 'sc_conv_w': sc_conv_w, 'sc_w_out': bf(sc_w_out),
    }
    w00 = (bf(ffn_w_gu[0, 0])[None], bf(ffn_w_down[0, 0])[None], 0)
    later = ((ffn_w_gu.reshape(2 * DEPTH * D_MODEL, 2 * D_FF), D_MODEL),
             (ffn_w_down.reshape(2 * DEPTH * D_FF, D_MODEL), D_FF))
    bs, t_s, _ = x_sample.shape
    bp, seq, _ = x_prompt.shape
    meta_rows = jnp.pad(meta.astype(x_prompt.dtype), ((0, WKV_CHUNK - N_META), (0, 0)))
    zero_wkv = jnp.zeros((1, 1, N_HEADS, HEAD_DIM, HEAD_DIM), state_wkv.dtype)
    zero_shift = jnp.zeros((1, 1, D_MODEL), state_shift.dtype)
    zero_conv = jnp.zeros((1, 1, CONV_W - 1, D_MODEL), state_conv.dtype)
    streams = [dict(x=meta_rows, bsz=1, t_len=WKV_CHUNK, t_valid=N_META, want_y=False,
                    wkv=zero_wkv, shift=zero_shift, conv=zero_conv),
               dict(x=x_prompt.reshape(bp * seq, D_MODEL), bsz=bp, t_len=seq, t_valid=seq, want_y=True,
                    prefix=0, casts=True),
               dict(x=x_sample.reshape(bs * t_s, D_MODEL), bsz=bs, t_len=t_s, t_valid=t_s, want_y=True,
                    wkv=state_wkv, shift=state_shift, conv=state_conv)]
    heads, (wgu_rest, wd_rest) = _trunk_head(streams, p, w00, ROW_TILE, cast=later)
    wgu_rest = wgu_rest.reshape(2 * DEPTH - 1, D_MODEL, 2 * D_FF)
    wd_rest = wd_rest.reshape(2 * DEPTH - 1, D_FF, D_MODEL)
    w_rest = [(wgu_rest, wd_rest, i) for i in range(2 * DEPTH - 1)]
    _, (yp, conv_p), (ys, conv_s) = _trunk_tail(heads, p, w_rest, ROW_TILE)
    _, head_p, head_s = heads
    return (yp.reshape(bp, seq, D_MODEL), ys.reshape(bs, t_s, D_MODEL), head_p['new_wkv'], head_p['new_shift'],
            conv_p, head_s['new_wkv'], head_s['new_shift'], conv_s)
```
